```python
import jax, jax.numpy as jnp
from jax import lax
import numpy as np

D_MODEL = 1024
BATCH = 2
SEQ = 8192
DEPTH = 1

N_META = 16
D_MIX = D_MODEL
GLA_HEADS = 4
GLA_DK = D_MIX // 4 // GLA_HEADS
GLA_DV = D_MIX // 2 // GLA_HEADS
GLA_RANK = 16
GLA_TAU = 16.0
GLA_CHUNK = 64
GLA_PAD = GLA_CHUNK - N_META
SWA_HEADS = 8
SWA_KV_HEADS = 2
SWA_GROUP = SWA_HEADS // SWA_KV_HEADS
SWA_HD = D_MIX // 2 // SWA_HEADS
SWA_WINDOW = 128
SWA_BLOCK = 128
ROPE_DIM = SWA_HD // 4
ROPE_THETA = 500000.0
D_FF = 4 * D_MODEL
EPS = 1e-5

IN_SIZES = (GLA_HEADS * GLA_DK,
            GLA_HEADS * GLA_DK,
            GLA_HEADS * GLA_DV,
            GLA_HEADS * GLA_DV,
            GLA_RANK,
            SWA_HEADS * SWA_HD,
            SWA_KV_HEADS * SWA_HD,
            SWA_KV_HEADS * SWA_HD)
D_IN = sum(IN_SIZES)

kernel_name = "hybrid_gla_swa_sink_meta_layer"


def rmsnorm(x, w):
    xf = x.astype(jnp.float32)
    y = xf * lax.rsqrt(jnp.mean(jnp.square(xf), axis=-1, keepdims=True) + EPS)
    return (y * w.astype(jnp.float32)).astype(x.dtype)


def partial_rope(x, pos):
    inv_freq = 1.0 / (ROPE_THETA ** (jnp.arange(0, ROPE_DIM, 2, dtype=jnp.float32) / ROPE_DIM))
    ang = pos.astype(jnp.float32)[:, None] * inv_freq[None, :]
    ang = jnp.concatenate([ang, ang], axis=-1)[:, None, :]
    cos, sin = jnp.cos(ang), jnp.sin(ang)
    xr = x[..., :ROPE_DIM].astype(jnp.float32)
    half = ROPE_DIM // 2
    rot = jnp.concatenate([-xr[..., half:], xr[..., :half]], axis=-1)
    xr = (xr * cos + rot * sin).astype(x.dtype)
    return jnp.concatenate([xr, x[..., ROPE_DIM:]], axis=-1)


def gla_chunk_step(S, inp):
    q, k, v, g = inp
    b = jnp.cumsum(g, axis=2)
    causal = jnp.tril(jnp.ones((GLA_CHUNK, GLA_CHUNK), dtype=bool))
    diff = b[:, :, :, None, :] - b[:, :, None, :, :]
    decay = jnp.exp(jnp.where(causal[None, None, :, :, None], diff, -jnp.inf))
    A = jnp.einsum('bhid,bhjd,bhijd->bhij', q, k, decay)
    o = (jnp.einsum('bhij,bhjv->bhiv', A, v)
         + jnp.einsum('bhid,bhdv->bhiv', q * jnp.exp(b), S))
    b_last = b[:, :, -1:, :]
    S = (jnp.exp(b_last[:, :, 0, :])[..., None] * S
         + jnp.einsum('bhjd,bhjv->bhdv', k * jnp.exp(b_last - b), v))
    return S, o


def gla_mixer(q, k, v, r, lr, w_gate_up, b_gate, gla_norm_w):
    B, L, _ = q.shape
    dtype = q.dtype
    g = jax.nn.log_sigmoid((lr @ w_gate_up + b_gate).astype(jnp.float32)) / GLA_TAU
    q = q.astype(jnp.float32).reshape(B, L, GLA_HEADS, GLA_DK) * (GLA_DK ** -0.5)
    k = k.astype(jnp.float32).reshape(B, L, GLA_HEADS, GLA_DK)
    v = v.astype(jnp.float32).reshape(B, L, GLA_HEADS, GLA_DV)
    g = g.reshape(B, L, GLA_HEADS, GLA_DK)
    pad = ((0, 0), (GLA_PAD, 0), (0, 0), (0, 0))
    Lp = L + GLA_PAD
    n_chunks = Lp // GLA_CHUNK

    def to_chunks(t):
        t = jnp.pad(t, pad).reshape(B, n_chunks, GLA_CHUNK, GLA_HEADS, t.shape[-1])
        return t.transpose(1, 0, 3, 2, 4)

    S0 = jnp.zeros((B, GLA_HEADS, GLA_DK, GLA_DV), jnp.float32)
    _, o = lax.scan(gla_chunk_step, S0, (to_chunks(q), to_chunks(k), to_chunks(v), to_chunks(g)))
    o = o.transpose(1, 0, 3, 2, 4).reshape(B, Lp, GLA_HEADS, GLA_DV)[:, GLA_PAD:]
    o = rmsnorm(o.astype(dtype), gla_norm_w)
    o = o * jax.nn.silu(r.reshape(B, L, GLA_HEADS, GLA_DV))
    return o.reshape(B, L, GLA_HEADS * GLA_DV)


def sink_softmax(scores, sink):
    sink_b = jnp.broadcast_to(sink, scores.shape[:-1] + (1,))
    p = jax.nn.softmax(jnp.concatenate([scores, sink_b], axis=-1), axis=-1)
    return p[..., :-1]


def swa_mixer(q, k, v, sinks, pos):
    B, L, _ = q.shape
    dtype = q.dtype
    q = partial_rope(q.reshape(B, L, SWA_HEADS, SWA_HD), pos) * (SWA_HD ** -0.5)
    k = partial_rope(k.reshape(B, L, SWA_KV_HEADS, SWA_HD), pos)
    v = v.reshape(B, L, SWA_KV_HEADS, SWA_HD)
    q = q.reshape(B, L, SWA_KV_HEADS, SWA_GROUP, SWA_HD)
    qm, qr = q[:, :N_META], q[:, N_META:]
    km, kr = k[:, :N_META], k[:, N_META:]
    vm, vr = v[:, :N_META], v[:, N_META:]
    sink = sinks.astype(jnp.float32).reshape(SWA_KV_HEADS, SWA_GROUP)

    sm = jnp.einsum('bqkgd,bjkd->bkgqj', qm, km).astype(jnp.float32)
    mmask = jnp.tril(jnp.ones((N_META, N_META), dtype=bool))
    pm = sink_softmax(jnp.where(mmask, sm, -jnp.inf), sink[None, :, :, None, None])
    om = jnp.einsum('bkgqj,bjkd->bqkgd', pm.astype(dtype), vm).reshape(B, N_META, SWA_HEADS * SWA_HD)

    S = L - N_META
    nb = S // SWA_BLOCK
    qb = qr.reshape(B, nb, SWA_BLOCK, SWA_KV_HEADS, SWA_GROUP, SWA_HD)

    def band(t):
        cur = t.reshape(B, nb, SWA_BLOCK, SWA_KV_HEADS, SWA_HD)
        prev = jnp.pad(t, ((0, 0), (SWA_BLOCK, 0), (0, 0), (0, 0)))[:, :S]
        prev = prev.reshape(B, nb, SWA_BLOCK, SWA_KV_HEADS, SWA_HD)
        return jnp.concatenate([prev, cur], axis=2)

    kw, vw = band(kr), band(vr)
    s_meta = jnp.einsum('bnqkgd,bjkd->bnkgqj', qb, km).astype(jnp.float32)
    s_win = jnp.einsum('bnqkgd,bnjkd->bnkgqj', qb, kw).astype(jnp.float32)
    rr = jnp.arange(SWA_BLOCK)[:, None]
    jj = jnp.arange(2 * SWA_BLOCK)[None, :]
    dist = SWA_BLOCK + rr - jj
    in_band = (dist >= 0) & (dist < SWA_WINDOW)
    blk = jnp.arange(nb)[:, None, None]
    wmask = in_band[None] & ((blk > 0) | (jj[None] >= SWA_BLOCK))
    s_win = jnp.where(wmask[None, :, None, None], s_win, -jnp.inf)
    p = sink_softmax(jnp.concatenate([s_meta, s_win], axis=-1),
                     sink[None, None, :, :, None, None]).astype(dtype)
    orr = (jnp.einsum('bnkgqj,bjkd->bnqkgd', p[..., :N_META], vm)
           + jnp.einsum('bnkgqj,bnjkd->bnqkgd', p[..., N_META:], vw))
    orr = orr.reshape(B, S, SWA_HEADS * SWA_HD)
    return jnp.concatenate([om, orr], axis=1)


def hybrid_layer(h, pos, norm_mix_w, w_in, w_gate_up, b_gate, gla_norm_w, sinks,
                 w_out, norm_ff_w, w_ff1, w_ff2):
    u = rmsnorm(h, norm_mix_w)
    proj = u @ w_in
    split_points = np.cumsum(IN_SIZES)[:-1].tolist()
    gq, gk, gv, gr, glr, sq, sk, sv = jnp.split(proj, split_points, axis=-1)
    o_gla = gla_mixer(gq, gk, gv, gr, glr, w_gate_up, b_gate, gla_norm_w)
    o_swa = swa_mixer(sq, sk, sv, sinks, pos)
    h = h + jnp.concatenate([o_gla, o_swa], axis=-1) @ w_out
    f = rmsnorm(h, norm_ff_w)
    return h + jnp.square(jax.nn.relu(f @ w_ff1)) @ w_ff2


def setup_inputs(seed: int = 0) -> dict:
    key = jax.random.key(seed)
    ks = jax.random.split(key, 14)
    f32 = jnp.float32
    nrm = lambda k, shape, s: jax.random.normal(k, shape, f32) * s
    return {
        "x": nrm(ks[0], (BATCH, SEQ, D_MODEL), 1.0),
        "meta_tokens": nrm(ks[1], (N_META, D_MODEL), 1.0),
        "norm_mix_w": 1.0 + nrm(ks[2], (DEPTH, D_MODEL), 0.02),
        "w_in": nrm(ks[3], (DEPTH, D_MODEL, D_IN), D_MODEL ** -0.5),
        "w_gate_up": nrm(ks[4], (DEPTH, GLA_RANK, GLA_HEADS * GLA_DK), GLA_RANK ** -0.5),
        "b_gate": nrm(ks[5], (DEPTH, GLA_HEADS * GLA_DK), 0.1),
        "gla_norm_w": 1.0 + nrm(ks[6], (DEPTH, GLA_DV), 0.02),
        "sinks": nrm(ks[7], (DEPTH, SWA_HEADS), 1.0),
        "w_out": nrm(ks[8], (DEPTH, D_MIX, D_MODEL), D_MIX ** -0.5),
        "norm_ff_w": 1.0 + nrm(ks[9], (DEPTH, D_MODEL), 0.02),
        "w_ff1": nrm(ks[10], (DEPTH, D_MODEL, D_FF), D_MODEL ** -0.5),
        "w_ff2": nrm(ks[11], (DEPTH, D_FF, D_MODEL), D_FF ** -0.5),
        "final_norm_w": 1.0 + nrm(ks[12], (D_MODEL,), 0.02),
    }


def reference(x, meta_tokens, norm_mix_w, w_in, w_gate_up, b_gate, gla_norm_w, sinks,
              w_out, norm_ff_w, w_ff1, w_ff2, final_norm_w):
    B = x.shape[0]
    meta = jnp.broadcast_to(meta_tokens[None].astype(x.dtype), (B, N_META, D_MODEL))
    h = jnp.concatenate([meta, x], axis=1)
    pos = jnp.arange(h.shape[1], dtype=jnp.int32)
    for layer in range(DEPTH):
        h = hybrid_layer(h, pos, norm_mix_w[layer], w_in[layer], w_gate_up[layer], b_gate[layer],
                         gla_norm_w[layer], sinks[layer], w_out[layer], norm_ff_w[layer],
                         w_ff1[layer], w_ff2[layer])
    return rmsnorm(h, final_norm_w)[:, N_META:]
```

```python
import functools

import jax
import jax.numpy as jnp
from jax import lax
from jax.experimental import pallas as pl
from jax.experimental.pallas import tpu as pltpu

F32 = jnp.float32
BF16 = jnp.bfloat16

D_MODEL = 1024
N_META = 16
GLA_HEADS = 4
GLA_DK = 64
GLA_DV = 128
GLA_RANK = 16
GLA_TAU = 16.0
GLA_CHUNK = 64
GLA_LEVELS = 6
SWA_HEADS = 8
SWA_KV_HEADS = 2
SWA_HD = 64
SWA_BLOCK = 128
SWA_HALF = SWA_BLOCK // 2
ROPE_DIM = 16
ROPE_THETA = 500000.0
D_FF = 4096
EPS = 1e-5

LANES = 128
GQ = GLA_HEADS * GLA_DK
GV = GLA_HEADS * GLA_DV
SQ = SWA_HEADS * SWA_HD
SKV = SWA_KV_HEADS * SWA_HD
FF_CHUNK = 1024

ROWS_IN = 512
ROWS_GLA = 512
ROWS_OUT = 512
VMEM_LIMIT = 56 * 1024 * 1024


def _nt_dot(a, b):
    return lax.dot_general(a, b, (((1,), (1,)), ((), ())), preferred_element_type=F32)


def _tn_dot(a, b):
    return lax.dot_general(a, b, (((0,), (0,)), ((), ())), preferred_element_type=F32)


def _rms(x, w):
    return x * lax.rsqrt(jnp.mean(x * x, axis=-1, keepdims=True) + EPS) * w


def _inproj_kernel(x_ref, nw_ref, wq_ref, wk_ref, wv_ref, wr_ref, wsq_ref, wsk_ref, wsv_ref,
                   wlr_ref, wg_ref, bg_ref, cq_ref, cos_ref, sup_ref, sdn_ref,
                   gq_ref, gk_ref, gv_ref, gr_ref, g_ref, sq_ref, sk_ref, sv_ref):
    u = _rms(x_ref[...], nw_ref[...]).astype(BF16)

    def proj(w_ref):
        return jnp.dot(u, w_ref[...], preferred_element_type=F32)

    gq_ref[...] = (proj(wq_ref) * (GLA_DK ** -0.5)).astype(BF16)
    gk_ref[...] = proj(wk_ref).astype(BF16)
    gv_ref[...] = proj(wv_ref).astype(BF16)
    gr_ref[...] = proj(wr_ref).astype(BF16)
    sv_ref[...] = proj(wsv_ref).astype(BF16)

    lr = proj(wlr_ref).astype(BF16)
    z = jnp.dot(lr, wg_ref[...], preferred_element_type=F32) + bg_ref[...]
    g_ref[...] = (jnp.minimum(z, 0.0) - jnp.log1p(jnp.exp(-jnp.abs(z)))) * (1.0 / GLA_TAU)

    cos, s_up, s_dn = cos_ref[...], sup_ref[...], sdn_ref[...]
    half = ROPE_DIM // 2

    def rope(t):
        return t * cos + pltpu.roll(t, LANES - half, axis=1) * s_up + pltpu.roll(t, half, axis=1) * s_dn

    sq = proj(wsq_ref)
    for s in range(SQ // LANES):
        sl = slice(s * LANES, (s + 1) * LANES)
        sq_ref[:, sl] = (rope(sq[:, sl]) * cq_ref[...]).astype(BF16)
    sk_ref[...] = rope(proj(wsk_ref)).astype(BF16)


def _in_projection(x2, pos_tables, wts, rows):
    n = x2.shape[0]
    grid = (n // rows,)
    nblk_pos = pos_tables[0].shape[0] // rows

    def row_spec(width):
        return pl.BlockSpec((rows, width), lambda i: (i, 0))

    def pos_spec():
        return pl.BlockSpec((rows, LANES), lambda i: (i % nblk_pos, 0))

    def full(a):
        return pl.BlockSpec(a.shape, lambda i: (0,) * a.ndim)

    (nw, wq, wk, wv, wr, wsq, wsk, wsv, wlr, wg, bg, cq) = wts
    cos_t, sup_t, sdn_t = pos_tables
    in_specs = ([row_spec(D_MODEL)] + [full(a) for a in wts]
                + [pos_spec(), pos_spec(), pos_spec()])
    widths = (GQ, GQ, GV, GV, GQ, SQ, SKV, SKV)
    dtypes = (BF16, BF16, BF16, BF16, F32, BF16, BF16, BF16)
    return pl.pallas_call(
        _inproj_kernel,
        grid=grid,
        in_specs=in_specs,
        out_specs=[row_spec(w) for w in widths],
        out_shape=[jax.ShapeDtypeStruct((n, w), d) for w, d in zip(widths, dtypes)],
        compiler_params=pltpu.CompilerParams(dimension_semantics=("arbitrary",),
                                             vmem_limit_bytes=VMEM_LIMIT),
        name="in_projection",
    )(x2, *wts, cos_t, sup_t, sdn_t)


def _block_cumsums(g):
    rows = g.shape[0]
    row = lax.broadcasted_iota(jnp.int32, g.shape, 0)
    w, t = [g], [g]
    for k in range(GLA_LEVELS):
        s = 1 << k
        upper = (row & s) != 0
        below = pltpu.roll(t[k], s, axis=0)
        above = pltpu.roll(t[k], rows - s, axis=0)
        w.append(w[k] + jnp.where(upper, below, 0.0))
        t.append(t[k] + jnp.where(upper, below, above))
    return w, t


def _gla_chunk(q, k, v, g, st, want_out):
    w, t = _block_cumsums(g)
    lane_q = lax.broadcasted_iota(jnp.int32, (GLA_CHUNK, LANES), 1)
    head_in_slab = [lane_q < GLA_DK, lane_q >= GLA_DK]
    lane_s = lax.broadcasted_iota(jnp.int32, (GLA_DV, GQ), 1)

    o = None
    if want_out:
        qm = [jnp.where(head_in_slab[h % 2], q[:, (h // 2) * LANES:(h // 2 + 1) * LANES], 0.0)
              for h in range(GLA_HEADS)]
        ri = lax.broadcasted_iota(jnp.int32, (GLA_CHUNK, GLA_CHUNK), 0)
        ci = lax.broadcasted_iota(jnp.int32, (GLA_CHUNK, GLA_CHUNK), 1)
        x = ri ^ ci
        level = jnp.zeros_like(x)
        for b in range(GLA_LEVELS):
            level = level + (x >= (1 << b)).astype(jnp.int32)
        level = jnp.where(ci > ri, -1, level)

        a = [jnp.zeros((GLA_CHUNK, GLA_CHUNK), F32) for _ in range(GLA_HEADS)]
        for lv in range(GLA_LEVELS + 1):
            if lv == 0:
                eq, kt = None, k.astype(BF16)
            else:
                eq = jnp.exp(w[lv - 1])
                kt = (k * jnp.exp(t[lv - 1] - w[lv - 1])).astype(BF16)
            for h in range(GLA_HEADS):
                sl = slice((h // 2) * LANES, (h // 2 + 1) * LANES)
                qt = (qm[h] if eq is None else qm[h] * eq[:, sl]).astype(BF16)
                a[h] = a[h] + jnp.where(level == lv, _nt_dot(qt, kt[:, sl]), 0.0)

        eb = jnp.exp(w[GLA_LEVELS])
        st_b = st.astype(BF16)
        outs = []
        for h in range(GLA_HEADS):
            sl = slice((h // 2) * LANES, (h // 2 + 1) * LANES)
            qb = (qm[h] * eb[:, sl]).astype(BF16)
            vh = v[:, h * GLA_DV:(h + 1) * GLA_DV]
            outs.append(jnp.dot(a[h].astype(BF16), vh, preferred_element_type=F32)
                        + _nt_dot(qb, st_b[:, sl]))
        o = jnp.concatenate(outs, axis=1)

    tot = t[GLA_LEVELS]
    ku = (k * jnp.exp(tot - w[GLA_LEVELS])).astype(BF16)
    new_st = st * jnp.exp(tot[0:1, :])
    for h in range(GLA_HEADS):
        upd = _tn_dot(v[:, h * GLA_DV:(h + 1) * GLA_DV], ku)
        new_st = new_st + jnp.where((lane_s >= h * GLA_DK) & (lane_s < (h + 1) * GLA_DK), upd, 0.0)
    return o, new_st


def _gla_kernel(q_ref, k_ref, v_ref, r_ref, g_ref, km_ref, vm_ref, gm_ref, nw_ref, o_ref, st_ref):
    @pl.when(pl.program_id(1) == 0)
    def _():
        _, st0 = _gla_chunk(None, km_ref[...].astype(F32), vm_ref[...], gm_ref[...],
                            jnp.zeros(st_ref.shape, F32), want_out=False)
        st_ref[...] = st0

    nw = nw_ref[...]

    def body(c, carry):
        rows = pl.ds(pl.multiple_of(c * GLA_CHUNK, GLA_CHUNK), GLA_CHUNK)
        o, st = _gla_chunk(q_ref[0, rows, :].astype(F32), k_ref[0, rows, :].astype(F32),
                           v_ref[0, rows, :], g_ref[0, rows, :], st_ref[...], want_out=True)
        st_ref[...] = st
        r = r_ref[0, rows, :].astype(F32)
        for h in range(GLA_HEADS):
            sl = slice(h * GLA_DV, (h + 1) * GLA_DV)
            rh = r[:, sl]
            o_ref[0, rows, sl] = (_rms(o[:, sl], nw) * (rh * jax.nn.sigmoid(rh))).astype(BF16)
        return carry

    lax.fori_loop(0, q_ref.shape[1] // GLA_CHUNK, body, 0)


def _gla_mixer(gq, gk, gv, gr, g, km, vm, gm, gla_norm_w, rows):
    b, s, _ = gq.shape

    def seq_spec(width):
        return pl.BlockSpec((1, rows, width), lambda i, j: (i, j, 0))

    def full(a):
        return pl.BlockSpec(a.shape, lambda i, j: (0,) * a.ndim)

    return pl.pallas_call(
        _gla_kernel,
        grid=(b, s // rows),
        in_specs=[seq_spec(GQ), seq_spec(GQ), seq_spec(GV), seq_spec(GV), seq_spec(GQ),
                  full(km), full(vm), full(gm), full(gla_norm_w)],
        out_specs=seq_spec(GV),
        out_shape=jax.ShapeDtypeStruct((b, s, GV), BF16),
        scratch_shapes=[pltpu.VMEM((GLA_DV, GQ), F32)],
        compiler_params=pltpu.CompilerParams(dimension_semantics=("arbitrary", "arbitrary"),
                                             vmem_limit_bytes=VMEM_LIMIT),
        name="gla_mixer",
    )(gq, gk, gv, gr, g, km, vm, gm, gla_norm_w)


def _swa_kernel(sink_ref, q_ref, kc_ref, kp_ref, vc_ref, vp_ref, km_ref, vm_ref, o_ref):
    has_prev = pl.program_id(1) > 0
    kc, kp, vc, vp = kc_ref[0], kp_ref[0], vc_ref[0], vp_ref[0]
    km, vm = km_ref[...], vm_ref[...]
    n_win = SWA_BLOCK + SWA_HALF
    n_keys = n_win + N_META
    r = lax.broadcasted_iota(jnp.int32, (SWA_HALF, n_keys), 0)
    c = lax.broadcasted_iota(jnp.int32, (SWA_HALF, n_keys), 1)
    in_band = (c > r) & (c <= r + SWA_BLOCK)
    lane = lax.broadcasted_iota(jnp.int32, (SWA_HALF, LANES), 1)
    low = lane < SWA_HD

    for half in range(2):
        rows = slice(half * SWA_HALF, (half + 1) * SWA_HALF)
        if half == 0:
            k_all = jnp.concatenate([kp, kc[:SWA_HALF], km], axis=0)
            v_all = jnp.concatenate([vp, vc[:SWA_HALF], vm], axis=0)
            n_prev = SWA_BLOCK
        else:
            k_all = jnp.concatenate([kp[SWA_HALF:], kc, km], axis=0)
            v_all = jnp.concatenate([vp[SWA_HALF:], vc, vm], axis=0)
            n_prev = SWA_HALF
        valid = (c >= n_win) | (in_band & (has_prev | (c >= n_prev)))
        for t in range(SWA_HEADS // 2):
            q = q_ref[0, rows, t * LANES:(t + 1) * LANES]
            side_out = []
            for side in range(2):
                head = t + side * (SWA_HEADS // 2)
                qh = jnp.where(low if side == 0 else ~low, q, jnp.zeros_like(q))
                s = jnp.where(valid, _nt_dot(qh, k_all), -jnp.inf)
                sink = sink_ref[head]
                m = jnp.maximum(jnp.max(s, axis=-1, keepdims=True), sink)
                p = jnp.exp(s - m)
                denom = jnp.sum(p, axis=-1, keepdims=True) + jnp.exp(sink - m)
                side_out.append(jnp.dot(p.astype(BF16), v_all, preferred_element_type=F32) / denom)
            o_ref[0, rows, t * LANES:(t + 1) * LANES] = jnp.where(low, side_out[0], side_out[1]).astype(BF16)


def _swa_mixer(sinks, sq, sk, sv, km, vm):
    b, s, _ = sq.shape
    nb = s // SWA_BLOCK

    def cur(width):
        return pl.BlockSpec((1, SWA_BLOCK, width), lambda i, j: (i, j, 0))

    def prev(width):
        return pl.BlockSpec((1, SWA_BLOCK, width), lambda i, j: (i, jnp.maximum(j - 1, 0), 0))

    def full(a):
        return pl.BlockSpec(a.shape, lambda i, j: (0,) * a.ndim)

    return pl.pallas_call(
        _swa_kernel,
        grid=(b, nb),
        in_specs=[pl.BlockSpec(memory_space=pltpu.SMEM),
                  cur(SQ), cur(SKV), prev(SKV), cur(SKV), prev(SKV), full(km), full(vm)],
        out_specs=cur(SQ),
        out_shape=jax.ShapeDtypeStruct((b, s, SQ), BF16),
        compiler_params=pltpu.CompilerParams(dimension_semantics=("arbitrary", "arbitrary"),
                                             vmem_limit_bytes=VMEM_LIMIT),
        name="swa_mixer",
    )(sinks, sq, sk, sk, sv, sv, km, vm)


def _out_ffn_kernel(x_ref, og_ref, os_ref, wo_ref, nf_ref, w1_ref, w2_ref, nl_ref, y_ref):
    h = (x_ref[...]
         + jnp.dot(og_ref[...], wo_ref[0:GV, :], preferred_element_type=F32)
         + jnp.dot(os_ref[...], wo_ref[GV:GV + SQ, :], preferred_element_type=F32))
    f = _rms(h, nf_ref[...]).astype(BF16)
    ff = None
    for c in range(D_FF // FF_CHUNK):
        sl = slice(c * FF_CHUNK, (c + 1) * FF_CHUNK)
        a = jnp.maximum(jnp.dot(f, w1_ref[:, sl], preferred_element_type=F32), 0.0)
        d = jnp.dot((a * a).astype(BF16), w2_ref[sl, :], preferred_element_type=F32)
        ff = d if ff is None else ff + d
    y_ref[...] = _rms(h + ff, nl_ref[...])


def _out_ffn(x2, og, osw, wo, nf, w1, w2, nl, rows):
    n = x2.shape[0]

    def row_spec(width):
        return pl.BlockSpec((rows, width), lambda i: (i, 0))

    def full(a):
        return pl.BlockSpec(a.shape, lambda i: (0,) * a.ndim, pipeline_mode=pl.Buffered(1))

    return pl.pallas_call(
        _out_ffn_kernel,
        grid=(n // rows,),
        in_specs=[row_spec(D_MODEL), row_spec(GV), row_spec(SQ),
                  full(wo), full(nf), full(w1), full(w2), full(nl)],
        out_specs=row_spec(D_MODEL),
        out_shape=jax.ShapeDtypeStruct((n, D_MODEL), F32),
        compiler_params=pltpu.CompilerParams(dimension_semantics=("arbitrary",),
                                             vmem_limit_bytes=VMEM_LIMIT),
        name="out_ffn",
    )(x2, og, osw, wo, nf, w1, w2, nl)


def _rope_tables(n_pos):
    half = ROPE_DIM // 2
    inv_freq = 1.0 / (ROPE_THETA ** (jnp.arange(0, ROPE_DIM, 2, dtype=F32) / ROPE_DIM))
    ang = jnp.arange(n_pos, dtype=jnp.int32).astype(F32)[:, None] * inv_freq[None, :]
    cos, sin = jnp.cos(ang), jnp.sin(ang)
    pad = jnp.zeros((n_pos, SWA_HD - ROPE_DIM), F32)
    zero = jnp.zeros((n_pos, half), F32)
    cos_t = jnp.concatenate([cos, cos, pad + 1.0], axis=1)
    sup_t = jnp.concatenate([-sin, zero, pad], axis=1)
    sdn_t = jnp.concatenate([zero, sin, pad], axis=1)
    rep = LANES // SWA_HD
    return tuple(jnp.tile(a, (1, rep)) for a in (cos_t, sup_t, sdn_t))


def _swa_head_order():
    order = []
    for t in range(SWA_HEADS // 2):
        order += [t, t + SWA_HEADS // 2]
    return order


def kernel(x, meta_tokens, norm_mix_w, w_in, w_gate_up, b_gate, gla_norm_w, sinks, w_out, norm_ff_w,
           w_ff1, w_ff2, final_norm_w):
    b, s, d = x.shape
    assert d == D_MODEL and s % ROWS_IN == 0 and s % ROWS_GLA == 0 and s % SWA_BLOCK == 0
    assert (b * s) % ROWS_OUT == 0 and w_in.shape[0] == 1

    w = w_in[0]
    o_gq, o_gk, o_gv, o_gr = 0, GQ, 2 * GQ, 2 * GQ + GV
    o_lr = o_gr + GV
    o_sq = o_lr + GLA_RANK
    o_sk = o_sq + SQ
    o_sv = o_sk + SKV
    order = _swa_head_order()
    sq_cols = jnp.concatenate([jnp.arange(o_sq + hh * SWA_HD, o_sq + (hh + 1) * SWA_HD) for hh in order])
    wq = w[:, o_gq:o_gq + GQ].astype(BF16)
    wk = w[:, o_gk:o_gk + GQ].astype(BF16)
    wv = w[:, o_gv:o_gv + GV].astype(BF16)
    wr = w[:, o_gr:o_gr + GV].astype(BF16)
    wsq = w[:, sq_cols].astype(BF16)
    wsk = w[:, o_sk:o_sk + SKV].astype(BF16)
    wsv = w[:, o_sv:o_sv + SKV].astype(BF16)
    wlr = jnp.pad(w[:, o_lr:o_lr + GLA_RANK], ((0, 0), (0, LANES - GLA_RANK))).astype(BF16)
    wg = jnp.pad(w_gate_up[0], ((0, LANES - GLA_RANK), (0, 0))).astype(BF16)
    bg = b_gate[0].reshape(1, GQ)
    nw = norm_mix_w[0].reshape(1, D_MODEL)
    cq = jnp.full((1, LANES), SWA_HD ** -0.5, F32)
    wts = (nw, wq, wk, wv, wr, wsq, wsk, wsv, wlr, wg, bg, cq)

    wo = w_out[0]
    swa_rows = jnp.concatenate([jnp.arange(GV + hh * SWA_HD, GV + (hh + 1) * SWA_HD) for hh in order])
    wo = jnp.concatenate([wo[:GV], wo[swa_rows]], axis=0).astype(BF16)
    w1 = w_ff1[0].astype(BF16)
    w2 = w_ff2[0].astype(BF16)

    cos_t, sup_t, sdn_t = _rope_tables(N_META + s)
    tok_tables = tuple(a[N_META:] for a in (cos_t, sup_t, sdn_t))
    meta_tables = tuple(a[:N_META] for a in (cos_t, sup_t, sdn_t))

    m_out = _in_projection(meta_tokens.astype(F32), meta_tables, wts, N_META)
    _, m_gk, m_gv, _, m_g, _, m_sk, m_sv = m_out
    front = ((GLA_CHUNK - N_META, 0), (0, 0))
    km, vm, gm = jnp.pad(m_gk, front), jnp.pad(m_gv, front), jnp.pad(m_g, front)

    x2 = x.reshape(b * s, d)
    gq, gk, gv, gr, g, sq, sk, sv = _in_projection(x2, tok_tables, wts, ROWS_IN)

    def seq(a):
        return a.reshape(b, s, a.shape[-1])

    o_gla = _gla_mixer(seq(gq), seq(gk), seq(gv), seq(gr), seq(g), km, vm, gm,
                       gla_norm_w[0].reshape(1, GLA_DV), ROWS_GLA)
    o_swa = _swa_mixer(sinks[0].astype(F32), seq(sq), seq(sk), seq(sv), m_sk, m_sv)

    y = _out_ffn(x2, o_gla.reshape(b * s, GV), o_swa.reshape(b * s, SQ), wo,
                 norm_ff_w[0].reshape(1, D_MODEL), w1, w2, final_norm_w.reshape(1, D_MODEL), ROWS_OUT)
    return y.reshape(b, s, d)
```

```python
import functools

import jax
import jax.numpy as jnp
from jax import lax
from jax.experimental import pallas as pl
from jax.experimental.pallas import tpu as pltpu

F32 = jnp.float32
BF16 = jnp.bfloat16

D_MODEL = 1024
N_META = 16
GLA_HEADS = 4
GLA_DK = 64
GLA_DV = 128
GLA_RANK = 16
GLA_TAU = 16.0
GLA_CHUNK = 64
GLA_LEVELS = 6
SWA_HEADS = 8
SWA_KV_HEADS = 2
SWA_HD = 64
SWA_BLOCK = 128
SWA_HALF = SWA_BLOCK // 2
ROPE_DIM = 16
ROPE_THETA = 500000.0
D_FF = 4096
EPS = 1e-5

LANES = 128
GQ = GLA_HEADS * GLA_DK
GV = GLA_HEADS * GLA_DV
SQ = SWA_HEADS * SWA_HD
SKV = SWA_KV_HEADS * SWA_HD
FF_CHUNK = 1024

ROWS_IN = 512
ROWS_GLA = 512
ROWS_OUT = 512
VMEM_LIMIT = 56 * 1024 * 1024


def _nt_dot(a, b):
    return lax.dot_general(a, b, (((1,), (1,)), ((), ())), preferred_element_type=F32)


def _tn_dot(a, b):
    return lax.dot_general(a, b, (((0,), (0,)), ((), ())), preferred_element_type=F32)


def _rms(x, w):
    return x * lax.rsqrt(jnp.mean(x * x, axis=-1, keepdims=True) + EPS) * w


def _inproj_kernel(x_ref, nw_ref, wq_ref, wk_ref, wv_ref, wr_ref, wsq_ref, wsk_ref, wsv_ref,
                   wlr_ref, wg_ref, bg_ref, cq_ref, cos_ref, sup_ref, sdn_ref,
                   gq_ref, gk_ref, gv_ref, gr_ref, g_ref, sq_ref, sk_ref, sv_ref):
    u = _rms(x_ref[...], nw_ref[...]).astype(BF16)

    def proj(w_ref):
        return jnp.dot(u, w_ref[...], preferred_element_type=F32)

    gq_ref[...] = (proj(wq_ref) * (GLA_DK ** -0.5)).astype(BF16)
    gk_ref[...] = proj(wk_ref).astype(BF16)
    gv_ref[...] = proj(wv_ref).astype(BF16)
    gr_ref[...] = proj(wr_ref).astype(BF16)
    sv_ref[...] = proj(wsv_ref).astype(BF16)

    lr = proj(wlr_ref).astype(BF16)
    z = jnp.dot(lr, wg_ref[...], preferred_element_type=F32) + bg_ref[...]
    g_ref[...] = (jnp.minimum(z, 0.0) - jnp.log1p(jnp.exp(-jnp.abs(z)))) * (1.0 / GLA_TAU)

    cos, s_up, s_dn = cos_ref[...], sup_ref[...], sdn_ref[...]
    half = ROPE_DIM // 2

    def rope(t):
        return t * cos + pltpu.roll(t, LANES - half, axis=1) * s_up + pltpu.roll(t, half, axis=1) * s_dn

    sq = proj(wsq_ref)
    for s in range(SQ // LANES):
        sl = slice(s * LANES, (s + 1) * LANES)
        sq_ref[:, sl] = (rope(sq[:, sl]) * cq_ref[...]).astype(BF16)
    sk_ref[...] = rope(proj(wsk_ref)).astype(BF16)


def _in_projection(x2, pos_tables, wts, rows):
    n = x2.shape[0]
    grid = (n // rows,)
    nblk_pos = pos_tables[0].shape[0] // rows

    def row_spec(width):
        return pl.BlockSpec((rows, width), lambda i: (i, 0))

    def pos_spec():
        return pl.BlockSpec((rows, LANES), lambda i: (i % nblk_pos, 0))

    def full(a):
        return pl.BlockSpec(a.shape, lambda i: (0,) * a.ndim)

    (nw, wq, wk, wv, wr, wsq, wsk, wsv, wlr, wg, bg, cq) = wts
    cos_t, sup_t, sdn_t = pos_tables
    in_specs = ([row_spec(D_MODEL)] + [full(a) for a in wts]
                + [pos_spec(), pos_spec(), pos_spec()])
    widths = (GQ, GQ, GV, GV, GQ, SQ, SKV, SKV)
    dtypes = (BF16, BF16, BF16, BF16, F32, BF16, BF16, BF16)
    return pl.pallas_call(
        _inproj_kernel,
        grid=grid,
        in_specs=in_specs,
        out_specs=[row_spec(w) for w in widths],
        out_shape=[jax.ShapeDtypeStruct((n, w), d) for w, d in zip(widths, dtypes)],
        compiler_params=pltpu.CompilerParams(dimension_semantics=("arbitrary",),
                                             vmem_limit_bytes=VMEM_LIMIT),
        name="in_projection",
    )(x2, *wts, cos_t, sup_t, sdn_t)


def _block_cumsums(g):
    rows = g.shape[0]
    row = lax.broadcasted_iota(jnp.int32, g.shape, 0)
    w, t = [g], [g]
    for k in range(GLA_LEVELS):
        s = 1 << k
        upper = (row & s) != 0
        below = pltpu.roll(t[k], s, axis=0)
        above = pltpu.roll(t[k], rows - s, axis=0)
        w.append(w[k] + jnp.where(upper, below, 0.0))
        t.append(t[k] + jnp.where(upper, below, above))
    return w, t


def _gla_chunk(q, k, v, g, st, want_out):
    w, t = _block_cumsums(g)
    lane_q = lax.broadcasted_iota(jnp.int32, (GLA_CHUNK, LANES), 1)
    head_in_slab = [lane_q < GLA_DK, lane_q >= GLA_DK]
    lane_s = lax.broadcasted_iota(jnp.int32, (GLA_DV, GQ), 1)

    o = None
    if want_out:
        qm = [jnp.where(head_in_slab[h % 2], q[:, (h // 2) * LANES:(h // 2 + 1) * LANES], 0.0)
              for h in range(GLA_HEADS)]
        ri = lax.broadcasted_iota(jnp.int32, (GLA_CHUNK, GLA_CHUNK), 0)
        ci = lax.broadcasted_iota(jnp.int32, (GLA_CHUNK, GLA_CHUNK), 1)
        x = ri ^ ci
        level = jnp.zeros_like(x)
        for b in range(GLA_LEVELS):
            level = level + (x >= (1 << b)).astype(jnp.int32)
        level = jnp.where(ci > ri, -1, level)

        a = [jnp.zeros((GLA_CHUNK, GLA_CHUNK), F32) for _ in range(GLA_HEADS)]
        for lv in range(GLA_LEVELS + 1):
            if lv == 0:
                eq, kt = None, k.astype(BF16)
            else:
                eq = jnp.exp(w[lv - 1])
                kt = (k * jnp.exp(t[lv - 1] - w[lv - 1])).astype(BF16)
            for h in range(GLA_HEADS):
                sl = slice((h // 2) * LANES, (h // 2 + 1) * LANES)
                qt = (qm[h] if eq is None else qm[h] * eq[:, sl]).astype(BF16)
                a[h] = a[h] + jnp.where(level == lv, _nt_dot(qt, kt[:, sl]), 0.0)

        eb = jnp.exp(w[GLA_LEVELS])
        st_b = st.astype(BF16)
        outs = []
        for h in range(GLA_HEADS):
            sl = slice((h // 2) * LANES, (h // 2 + 1) * LANES)
            qb = (qm[h] * eb[:, sl]).astype(BF16)
            vh = v[:, h * GLA_DV:(h + 1) * GLA_DV]
            outs.append(jnp.dot(a[h].astype(BF16), vh, preferred_element_type=F32)
                        + _nt_dot(qb, st_b[:, sl]))
        o = jnp.concatenate(outs, axis=1)

    tot = t[GLA_LEVELS]
    ku = (k * jnp.exp(tot - w[GLA_LEVELS])).astype(BF16)
    new_st = st * jnp.exp(tot[0:1, :])
    for h in range(GLA_HEADS):
        upd = _tn_dot(v[:, h * GLA_DV:(h + 1) * GLA_DV], ku)
        new_st = new_st + jnp.where((lane_s >= h * GLA_DK) & (lane_s < (h + 1) * GLA_DK), upd, 0.0)
    return o, new_st


def _gla_kernel(q_ref, k_ref, v_ref, r_ref, g_ref, km_ref, vm_ref, gm_ref, nw_ref, o_ref, st_ref):
    @pl.when(pl.program_id(1) == 0)
    def _():
        _, st0 = _gla_chunk(None, km_ref[...].astype(F32), vm_ref[...], gm_ref[...],
                            jnp.zeros(st_ref.shape, F32), want_out=False)
        st_ref[...] = st0

    nw = nw_ref[...]

    def body(c, carry):
        rows = pl.ds(pl.multiple_of(c * GLA_CHUNK, GLA_CHUNK), GLA_CHUNK)
        o, st = _gla_chunk(q_ref[0, rows, :].astype(F32), k_ref[0, rows, :].astype(F32),
                           v_ref[0, rows, :], g_ref[0, rows, :], st_ref[...], want_out=True)
        st_ref[...] = st
        r = r_ref[0, rows, :].astype(F32)
        for h in range(GLA_HEADS):
            sl = slice(h * GLA_DV, (h + 1) * GLA_DV)
            rh = r[:, sl]
            o_ref[0, rows, sl] = (_rms(o[:, sl], nw) * (rh * jax.nn.sigmoid(rh))).astype(BF16)
        return carry

    lax.fori_loop(0, q_ref.shape[1] // GLA_CHUNK, body, 0)


def _gla_mixer(gq, gk, gv, gr, g, km, vm, gm, gla_norm_w, rows):
    b, s, _ = gq.shape

    def seq_spec(width):
        return pl.BlockSpec((1, rows, width), lambda i, j: (i, j, 0))

    def full(a):
        return pl.BlockSpec(a.shape, lambda i, j: (0,) * a.ndim)

    return pl.pallas_call(
        _gla_kernel,
        grid=(b, s // rows),
        in_specs=[seq_spec(GQ), seq_spec(GQ), seq_spec(GV), seq_spec(GV), seq_spec(GQ),
                  full(km), full(vm), full(gm), full(gla_norm_w)],
        out_specs=seq_spec(GV),
        out_shape=jax.ShapeDtypeStruct((b, s, GV), BF16),
        scratch_shapes=[pltpu.VMEM((GLA_DV, GQ), F32)],
        compiler_params=pltpu.CompilerParams(dimension_semantics=("arbitrary", "arbitrary"),
                                             vmem_limit_bytes=VMEM_LIMIT),
        name="gla_mixer",
    )(gq, gk, gv, gr, g, km, vm, gm, gla_norm_w)


def _swa_kernel(sink_ref, q_ref, kc_ref, kp_ref, vc_ref, vp_ref, km_ref, vm_ref, o_ref):
    has_prev = pl.program_id(1) > 0
    kc, kp, vc, vp = kc_ref[0], kp_ref[0], vc_ref[0], vp_ref[0]
    km, vm = km_ref[...], vm_ref[...]
    n_win = SWA_BLOCK + SWA_HALF
    n_keys = n_win + N_META
    n_q = SWA_HEADS * SWA_HALF
    r = lax.broadcasted_iota(jnp.int32, (n_q, n_keys), 0) & (SWA_HALF - 1)
    c = lax.broadcasted_iota(jnp.int32, (n_q, n_keys), 1)
    in_band = (c > r) & (c <= r + SWA_BLOCK)
    lane = lax.broadcasted_iota(jnp.int32, (SWA_HALF, LANES), 1)
    low = lane < SWA_HD

    for half in range(2):
        rows = slice(half * SWA_HALF, (half + 1) * SWA_HALF)
        if half == 0:
            k_all = jnp.concatenate([kp, kc[:SWA_HALF], km], axis=0)
            v_all = jnp.concatenate([vp, vc[:SWA_HALF], vm], axis=0)
            n_prev = SWA_BLOCK
        else:
            k_all = jnp.concatenate([kp[SWA_HALF:], kc, km], axis=0)
            v_all = jnp.concatenate([vp[SWA_HALF:], vc, vm], axis=0)
            n_prev = SWA_HALF
        valid = (c >= n_win) | (in_band & (has_prev | (c >= n_prev)))
        pieces = []
        for t in range(SWA_HEADS // 2):
            q = q_ref[0, rows, t * LANES:(t + 1) * LANES]
            pieces += [jnp.where(low, q, jnp.zeros_like(q)), jnp.where(low, jnp.zeros_like(q), q)]
        q_all = jnp.concatenate(pieces, axis=0)
        s = jnp.where(valid, _nt_dot(q_all, k_all), -jnp.inf)
        m_keys = jnp.max(s, axis=-1, keepdims=True)
        p, denom = [], []
        for i in range(SWA_HEADS):
            blk = slice(i * SWA_HALF, (i + 1) * SWA_HALF)
            sink = sink_ref[(i // 2) + (i % 2) * (SWA_HEADS // 2)]
            m = jnp.maximum(m_keys[blk], sink)
            p.append(jnp.exp(s[blk] - m))
            denom.append(jnp.sum(p[i], axis=-1, keepdims=True) + jnp.exp(sink - m))
        o = jnp.dot(jnp.concatenate(p, axis=0).astype(BF16), v_all, preferred_element_type=F32)
        for t in range(SWA_HEADS // 2):
            lo = o[(2 * t) * SWA_HALF:(2 * t + 1) * SWA_HALF] / denom[2 * t]
            hi = o[(2 * t + 1) * SWA_HALF:(2 * t + 2) * SWA_HALF] / denom[2 * t + 1]
            o_ref[0, rows, t * LANES:(t + 1) * LANES] = jnp.where(low, lo, hi).astype(BF16)


def _swa_mixer(sinks, sq, sk, sv, km, vm):
    b, s, _ = sq.shape
    nb = s // SWA_BLOCK

    def cur(width):
        return pl.BlockSpec((1, SWA_BLOCK, width), lambda i, j: (i, j, 0))

    def prev(width):
        return pl.BlockSpec((1, SWA_BLOCK, width), lambda i, j: (i, jnp.maximum(j - 1, 0), 0))

    def full(a):
        return pl.BlockSpec(a.shape, lambda i, j: (0,) * a.ndim)

    return pl.pallas_call(
        _swa_kernel,
        grid=(b, nb),
        in_specs=[pl.BlockSpec(memory_space=pltpu.SMEM),
                  cur(SQ), cur(SKV), prev(SKV), cur(SKV), prev(SKV), full(km), full(vm)],
        out_specs=cur(SQ),
        out_shape=jax.ShapeDtypeStruct((b, s, SQ), BF16),
        compiler_params=pltpu.CompilerParams(dimension_semantics=("arbitrary", "arbitrary"),
                                             vmem_limit_bytes=VMEM_LIMIT),
        name="swa_mixer",
    )(sinks, sq, sk, sk, sv, sv, km, vm)


def _out_ffn_kernel(x_ref, og_ref, os_ref, wo_ref, nf_ref, w1_ref, w2_ref, nl_ref, y_ref):
    h = (x_ref[...]
         + jnp.dot(og_ref[...], wo_ref[0:GV, :], preferred_element_type=F32)
         + jnp.dot(os_ref[...], wo_ref[GV:GV + SQ, :], preferred_element_type=F32))
    f = _rms(h, nf_ref[...]).astype(BF16)
    ff = None
    for c in range(D_FF // FF_CHUNK):
        sl = slice(c * FF_CHUNK, (c + 1) * FF_CHUNK)
        a = jnp.maximum(jnp.dot(f, w1_ref[:, sl], preferred_element_type=F32), 0.0)
        d = jnp.dot((a * a).astype(BF16), w2_ref[sl, :], preferred_element_type=F32)
        ff = d if ff is None else ff + d
    y_ref[...] = _rms(h + ff, nl_ref[...])


def _out_ffn(x2, og, osw, wo, nf, w1, w2, nl, rows):
    n = x2.shape[0]

    def row_spec(width):
        return pl.BlockSpec((rows, width), lambda i: (i, 0))

    def full(a):
        return pl.BlockSpec(a.shape, lambda i: (0,) * a.ndim, pipeline_mode=pl.Buffered(1))

    return pl.pallas_call(
        _out_ffn_kernel,
        grid=(n // rows,),
        in_specs=[row_spec(D_MODEL), row_spec(GV), row_spec(SQ),
                  full(wo), full(nf), full(w1), full(w2), full(nl)],
        out_specs=row_spec(D_MODEL),
        out_shape=jax.ShapeDtypeStruct((n, D_MODEL), F32),
        compiler_params=pltpu.CompilerParams(dimension_semantics=("arbitrary",),
                                             vmem_limit_bytes=VMEM_LIMIT),
        name="out_ffn",
    )(x2, og, osw, wo, nf, w1, w2, nl)


def _rope_tables(n_pos):
    half = ROPE_DIM // 2
    inv_freq = 1.0 / (ROPE_THETA ** (jnp.arange(0, ROPE_DIM, 2, dtype=F32) / ROPE_DIM))
    ang = jnp.arange(n_pos, dtype=jnp.int32).astype(F32)[:, None] * inv_freq[None, :]
    cos, sin = jnp.cos(ang), jnp.sin(ang)
    pad = jnp.zeros((n_pos, SWA_HD - ROPE_DIM), F32)
    zero = jnp.zeros((n_pos, half), F32)
    cos_t = jnp.concatenate([cos, cos, pad + 1.0], axis=1)
    sup_t = jnp.concatenate([-sin, zero, pad], axis=1)
    sdn_t = jnp.concatenate([zero, sin, pad], axis=1)
    rep = LANES // SWA_HD
    return tuple(jnp.tile(a, (1, rep)) for a in (cos_t, sup_t, sdn_t))


def _swa_head_order():
    order = []
    for t in range(SWA_HEADS // 2):
        order += [t, t + SWA_HEADS // 2]
    return order


def kernel(x, meta_tokens, norm_mix_w, w_in, w_gate_up, b_gate, gla_norm_w, sinks, w_out, norm_ff_w,
           w_ff1, w_ff2, final_norm_w):
    b, s, d = x.shape
    assert d == D_MODEL and s % ROWS_IN == 0 and s % ROWS_GLA == 0 and s % SWA_BLOCK == 0
    assert (b * s) % ROWS_OUT == 0 and w_in.shape[0] == 1

    w = w_in[0]
    o_gq, o_gk, o_gv, o_gr = 0, GQ, 2 * GQ, 2 * GQ + GV
    o_lr = o_gr + GV
    o_sq = o_lr + GLA_RANK
    o_sk = o_sq + SQ
    o_sv = o_sk + SKV
    order = _swa_head_order()
    sq_cols = jnp.concatenate([jnp.arange(o_sq + hh * SWA_HD, o_sq + (hh + 1) * SWA_HD) for hh in order])
    wq = w[:, o_gq:o_gq + GQ].astype(BF16)
    wk = w[:, o_gk:o_gk + GQ].astype(BF16)
    wv = w[:, o_gv:o_gv + GV].astype(BF16)
    wr = w[:, o_gr:o_gr + GV].astype(BF16)
    wsq = w[:, sq_cols].astype(BF16)
    wsk = w[:, o_sk:o_sk + SKV].astype(BF16)
    wsv = w[:, o_sv:o_sv + SKV].astype(BF16)
    wlr = jnp.pad(w[:, o_lr:o_lr + GLA_RANK], ((0, 0), (0, LANES - GLA_RANK))).astype(BF16)
    wg = jnp.pad(w_gate_up[0], ((0, LANES - GLA_RANK), (0, 0))).astype(BF16)
    bg = b_gate[0].reshape(1, GQ)
    nw = norm_mix_w[0].reshape(1, D_MODEL)
    cq = jnp.full((1, LANES), SWA_HD ** -0.5, F32)
    wts = (nw, wq, wk, wv, wr, wsq, wsk, wsv, wlr, wg, bg, cq)

    wo = w_out[0]
    swa_rows = jnp.concatenate([jnp.arange(GV + hh * SWA_HD, GV + (hh + 1) * SWA_HD) for hh in order])
    wo = jnp.concatenate([wo[:GV], wo[swa_rows]], axis=0).astype(BF16)
    w1 = w_ff1[0].astype(BF16)
    w2 = w_ff2[0].astype(BF16)

    cos_t, sup_t, sdn_t = _rope_tables(N_META + s)
    tok_tables = tuple(a[N_META:] for a in (cos_t, sup_t, sdn_t))
    meta_tables = tuple(a[:N_META] for a in (cos_t, sup_t, sdn_t))

    m_out = _in_projection(meta_tokens.astype(F32), meta_tables, wts, N_META)
    _, m_gk, m_gv, _, m_g, _, m_sk, m_sv = m_out
    front = ((GLA_CHUNK - N_META, 0), (0, 0))
    km, vm, gm = jnp.pad(m_gk, front), jnp.pad(m_gv, front), jnp.pad(m_g, front)

    x2 = x.reshape(b * s, d)
    gq, gk, gv, gr, g, sq, sk, sv = _in_projection(x2, tok_tables, wts, ROWS_IN)

    def seq(a):
        return a.reshape(b, s, a.shape[-1])

    o_gla = _gla_mixer(seq(gq), seq(gk), seq(gv), seq(gr), seq(g), km, vm, gm,
                       gla_norm_w[0].reshape(1, GLA_DV), ROWS_GLA)
    o_swa = _swa_mixer(sinks[0].astype(F32), seq(sq), seq(sk), seq(sv), m_sk, m_sv)

    y = _out_ffn(x2, o_gla.reshape(b * s, GV), o_swa.reshape(b * s, SQ), wo,
                 norm_ff_w[0].reshape(1, D_MODEL), w1, w2, final_norm_w.reshape(1, D_MODEL), ROWS_OUT)
    return y.reshape(b, s, d)
```

```python
import functools

import jax
import jax.numpy as jnp
from jax import lax
from jax.experimental import pallas as pl
from jax.experimental.pallas import tpu as pltpu

F32 = jnp.float32
BF16 = jnp.bfloat16

D_MODEL = 1024
N_META = 16
GLA_HEADS = 4
GLA_DK = 64
GLA_DV = 128
GLA_RANK = 16
GLA_TAU = 16.0
GLA_CHUNK = 64
GLA_LEVELS = 6
GLA_UNROLL = 4
SWA_HEADS = 8
SWA_KV_HEADS = 2
SWA_HD = 64
SWA_BLOCK = 128
SWA_HALF = SWA_BLOCK // 2
ROPE_DIM = 16
ROPE_THETA = 500000.0
D_FF = 4096
EPS = 1e-5
LOG2_E = 1.4426950408889634

LANES = 128
GQ = GLA_HEADS * GLA_DK
GV = GLA_HEADS * GLA_DV
SQ = SWA_HEADS * SWA_HD
SKV = SWA_KV_HEADS * SWA_HD
FF_CHUNK = 1024

ROWS_IN = 1024
ROWS_IN_SUB = 256
W_SQ = 2 * GQ + 2 * GV
W_SK = W_SQ + SQ
W_LR = W_SK + 2 * SKV
ROWS_GLA = 512
ROWS_OUT = 512
VMEM_LIMIT = 56 * 1024 * 1024


def _nt_dot(a, b):
    return lax.dot_general(a, b, (((1,), (1,)), ((), ())), preferred_element_type=F32)


def _tn_dot(a, b):
    return lax.dot_general(a, b, (((0,), (0,)), ((), ())), preferred_element_type=F32)


def _rms(x, w):
    return x * lax.rsqrt(jnp.mean(x * x, axis=-1, keepdims=True) + EPS) * w


def _inproj_kernel(x_ref, nw_ref, w_ref, wg_ref, bg_ref, cq_ref, cos_ref, sup_ref, sdn_ref,
                   gq_ref, gk_ref, gv_ref, gr_ref, g_ref, sq_ref, sk_ref, sv_ref):
    half = ROPE_DIM // 2
    n_sub = x_ref.shape[0] // ROWS_IN_SUB if x_ref.shape[0] > ROWS_IN_SUB else 1
    sub = x_ref.shape[0] // n_sub
    for i in range(n_sub):
        rows = slice(i * sub, (i + 1) * sub)
        u = _rms(x_ref[rows, :], nw_ref[...]).astype(BF16)

        def proj(lo, width):
            return jnp.dot(u, w_ref[:, lo:lo + width], preferred_element_type=F32)

        qk = proj(0, 2 * GQ)
        gq_ref[rows, :] = (qk[:, :GQ] * (GLA_DK ** -0.5)).astype(BF16)
        gk_ref[rows, :] = qk[:, GQ:].astype(BF16)
        gv_ref[rows, :] = proj(2 * GQ, GV).astype(BF16)
        gr_ref[rows, :] = proj(2 * GQ + GV, GV).astype(BF16)

        lr = proj(W_LR, LANES).astype(BF16)
        z = jnp.dot(lr, wg_ref[...], preferred_element_type=F32) + bg_ref[...]
        g_ref[rows, :] = (jnp.minimum(z, 0.0) - jnp.log1p(jnp.exp(-jnp.abs(z)))) * (LOG2_E / GLA_TAU)

        cos, s_up, s_dn = cos_ref[rows, :], sup_ref[rows, :], sdn_ref[rows, :]

        def rope(t):
            return t * cos + pltpu.roll(t, LANES - half, axis=1) * s_up + pltpu.roll(t, half, axis=1) * s_dn

        sq = proj(W_SQ, SQ)
        for s in range(SQ // LANES):
            sl = slice(s * LANES, (s + 1) * LANES)
            sq_ref[rows, sl] = (rope(sq[:, sl]) * cq_ref[...]).astype(BF16)
        kv = proj(W_SK, 2 * SKV)
        sk_ref[rows, :] = rope(kv[:, :SKV]).astype(BF16)
        sv_ref[rows, :] = kv[:, SKV:].astype(BF16)


def _in_projection(x2, pos_tables, wts, rows):
    n = x2.shape[0]
    grid = (n // rows,)
    nblk_pos = pos_tables[0].shape[0] // rows

    def row_spec(width):
        return pl.BlockSpec((rows, width), lambda i: (i, 0))

    def pos_spec():
        return pl.BlockSpec((rows, LANES), lambda i: (i % nblk_pos, 0))

    def full(a):
        return pl.BlockSpec(a.shape, lambda i: (0,) * a.ndim)

    cos_t, sup_t, sdn_t = pos_tables
    in_specs = ([row_spec(D_MODEL)] + [full(a) for a in wts]
                + [pos_spec(), pos_spec(), pos_spec()])
    widths = (GQ, GQ, GV, GV, GQ, SQ, SKV, SKV)
    dtypes = (BF16, BF16, BF16, BF16, F32, BF16, BF16, BF16)
    return pl.pallas_call(
        _inproj_kernel,
        grid=grid,
        in_specs=in_specs,
        out_specs=[row_spec(w) for w in widths],
        out_shape=[jax.ShapeDtypeStruct((n, w), d) for w, d in zip(widths, dtypes)],
        compiler_params=pltpu.CompilerParams(dimension_semantics=("arbitrary",),
                                             vmem_limit_bytes=VMEM_LIMIT),
        name="in_projection",
    )(x2, *wts, cos_t, sup_t, sdn_t)


def _block_cumsums(g):
    rows = g.shape[0]
    row = lax.broadcasted_iota(jnp.int32, g.shape, 0)
    w, t = [g], [g]
    for k in range(GLA_LEVELS):
        s = 1 << k
        upper = (row & s) != 0
        below = pltpu.roll(t[k], s, axis=0)
        above = pltpu.roll(t[k], rows - s, axis=0)
        w.append(w[k] + jnp.where(upper, below, 0.0))
        t.append(t[k] + jnp.where(upper, below, above))
    return w, t


def _gla_levels():
    ri = lax.broadcasted_iota(jnp.int32, (GLA_CHUNK, LANES), 0)
    ci = lax.broadcasted_iota(jnp.int32, (GLA_CHUNK, LANES), 1) & (GLA_CHUNK - 1)
    x = ri ^ ci
    level = jnp.zeros_like(x)
    for b in range(GLA_LEVELS):
        level = level + (x >= (1 << b)).astype(jnp.int32)
    return jnp.where(ci > ri, -1, level)


def _gla_chunk(q, k, v, g, st, level):
    w, t = _block_cumsums(g)
    n_slab = GQ // LANES

    def low_lanes(rows):
        return lax.broadcasted_iota(jnp.int32, (rows, LANES), 1) < GLA_DK

    def block_diag(m):
        lo = low_lanes(m.shape[0])
        zero = jnp.zeros_like(m)
        return jnp.concatenate([jnp.where(lo, m, zero), jnp.where(lo, zero, m)], axis=0)

    o = None
    if level is not None:
        a = [jnp.zeros((GLA_CHUNK, LANES), F32) for _ in range(n_slab)]
        for lv in range(GLA_LEVELS + 1):
            if lv == 0:
                qt, kt = q.astype(BF16), k.astype(BF16)
            else:
                qt = (q * jnp.exp2(w[lv - 1])).astype(BF16)
                kt = (k * jnp.exp2(t[lv - 1] - w[lv - 1])).astype(BF16)
            for s in range(n_slab):
                sl = slice(s * LANES, (s + 1) * LANES)
                a[s] = jnp.where(level == lv, _nt_dot(qt[:, sl], block_diag(kt[:, sl])), a[s])

        qb = (q * jnp.exp2(w[GLA_LEVELS])).astype(BF16)
        st_b = st.astype(BF16)
        outs = []
        for s in range(n_slab):
            sl = slice(s * LANES, (s + 1) * LANES)
            v_ab = v[:, (2 * s) * GLA_DV:(2 * s + 2) * GLA_DV]
            first = lax.broadcasted_iota(jnp.int32, v_ab.shape, 1) < GLA_DV
            zero = jnp.zeros_like(v_ab)
            v_bd = jnp.concatenate([jnp.where(first, v_ab, zero), jnp.where(first, zero, v_ab)], axis=0)
            outs.append(jnp.dot(a[s].astype(BF16), v_bd, preferred_element_type=F32)
                        + _nt_dot(qb[:, sl], block_diag(st_b[:, sl])))
        o = jnp.concatenate(outs, axis=1)

    tot = t[GLA_LEVELS]
    ku = (k * jnp.exp2(tot - w[GLA_LEVELS])).astype(BF16)
    decayed = st * jnp.exp2(tot[0:1, :])
    lo = low_lanes(GLA_DV)
    new_st = []
    for s in range(n_slab):
        sl = slice(s * LANES, (s + 1) * LANES)
        upd = _tn_dot(v[:, (2 * s) * GLA_DV:(2 * s + 2) * GLA_DV], ku[:, sl])
        new_st.append(decayed[:, sl] + jnp.where(lo, upd[:GLA_DV], upd[GLA_DV:]))
    return o, jnp.concatenate(new_st, axis=1)


def _gla_kernel(q_ref, k_ref, v_ref, r_ref, g_ref, km_ref, vm_ref, gm_ref, nw_ref, o_ref, st_ref):
    @pl.when(pl.program_id(1) == 0)
    def _():
        st_ref[...] = jnp.zeros(st_ref.shape, F32)
        _, st0 = _gla_chunk(None, km_ref[...].astype(F32), vm_ref[...], gm_ref[...], st_ref[...], None)
        st_ref[...] = st0

    nw = nw_ref[...]
    level = _gla_levels()

    def body(c, carry):
        st = st_ref[...]
        for u in range(GLA_UNROLL):
            rows = pl.ds(pl.multiple_of((c * GLA_UNROLL + u) * GLA_CHUNK, GLA_CHUNK), GLA_CHUNK)
            o, st = _gla_chunk(q_ref[0, rows, :].astype(F32), k_ref[0, rows, :].astype(F32),
                               v_ref[0, rows, :], g_ref[0, rows, :], st, level)
            r = r_ref[0, rows, :].astype(F32)
            for h in range(GLA_HEADS):
                sl = slice(h * GLA_DV, (h + 1) * GLA_DV)
                rh = r[:, sl]
                o_ref[0, rows, sl] = (_rms(o[:, sl], nw) * (rh * jax.nn.sigmoid(rh))).astype(BF16)
        st_ref[...] = st
        return carry

    lax.fori_loop(0, q_ref.shape[1] // (GLA_CHUNK * GLA_UNROLL), body, 0)


def _gla_mixer(gq, gk, gv, gr, g, km, vm, gm, gla_norm_w, rows):
    b, s, _ = gq.shape

    def seq_spec(width):
        return pl.BlockSpec((1, rows, width), lambda i, j: (i, j, 0))

    def full(a):
        return pl.BlockSpec(a.shape, lambda i, j: (0,) * a.ndim)

    return pl.pallas_call(
        _gla_kernel,
        grid=(b, s // rows),
        in_specs=[seq_spec(GQ), seq_spec(GQ), seq_spec(GV), seq_spec(GV), seq_spec(GQ),
                  full(km), full(vm), full(gm), full(gla_norm_w)],
        out_specs=seq_spec(GV),
        out_shape=jax.ShapeDtypeStruct((b, s, GV), BF16),
        scratch_shapes=[pltpu.VMEM((GLA_DV, GQ), F32)],
        compiler_params=pltpu.CompilerParams(dimension_semantics=("arbitrary", "arbitrary"),
                                             vmem_limit_bytes=VMEM_LIMIT),
        name="gla_mixer",
    )(gq, gk, gv, gr, g, km, vm, gm, gla_norm_w)


def _swa_kernel(sink_ref, q_ref, kc_ref, kp_ref, vc_ref, vp_ref, km_ref, vm_ref, o_ref):
    has_prev = pl.program_id(1) > 0
    kc, kp, vc, vp = kc_ref[0], kp_ref[0], vc_ref[0], vp_ref[0]
    km, vm = km_ref[...], vm_ref[...]
    n_win = SWA_BLOCK + SWA_HALF
    n_keys = n_win + N_META
    n_q = SWA_HEADS * SWA_HALF
    r = lax.broadcasted_iota(jnp.int32, (n_q, n_keys), 0) & (SWA_HALF - 1)
    c = lax.broadcasted_iota(jnp.int32, (n_q, n_keys), 1)
    in_band = (c > r) & (c <= r + SWA_BLOCK)
    lane = lax.broadcasted_iota(jnp.int32, (SWA_HALF, LANES), 1)
    low = lane < SWA_HD

    for half in range(2):
        rows = slice(half * SWA_HALF, (half + 1) * SWA_HALF)
        if half == 0:
            k_all = jnp.concatenate([kp, kc[:SWA_HALF], km], axis=0)
            v_all = jnp.concatenate([vp, vc[:SWA_HALF], vm], axis=0)
            n_prev = SWA_BLOCK
        else:
            k_all = jnp.concatenate([kp[SWA_HALF:], kc, km], axis=0)
            v_all = jnp.concatenate([vp[SWA_HALF:], vc, vm], axis=0)
            n_prev = SWA_HALF
        valid = (c >= n_win) | (in_band & (has_prev | (c >= n_prev)))
        pieces = []
        for t in range(SWA_HEADS // 2):
            q = q_ref[0, rows, t * LANES:(t + 1) * LANES]
            pieces += [jnp.where(low, q, jnp.zeros_like(q)), jnp.where(low, jnp.zeros_like(q), q)]
        q_all = jnp.concatenate(pieces, axis=0)
        s = jnp.where(valid, _nt_dot(q_all, k_all), -jnp.inf)
        m_keys = jnp.max(s, axis=-1, keepdims=True)
        p, denom = [], []
        for i in range(SWA_HEADS):
            blk = slice(i * SWA_HALF, (i + 1) * SWA_HALF)
            sink = sink_ref[(i // 2) + (i % 2) * (SWA_HEADS // 2)] * LOG2_E
            m = jnp.maximum(m_keys[blk], sink)
            p.append(jnp.exp2(s[blk] - m))
            denom.append(jnp.sum(p[i], axis=-1, keepdims=True) + jnp.exp2(sink - m))
        o = jnp.dot(jnp.concatenate(p, axis=0).astype(BF16), v_all, preferred_element_type=F32)
        for t in range(SWA_HEADS // 2):
            lo = o[(2 * t) * SWA_HALF:(2 * t + 1) * SWA_HALF] / denom[2 * t]
            hi = o[(2 * t + 1) * SWA_HALF:(2 * t + 2) * SWA_HALF] / denom[2 * t + 1]
            o_ref[0, rows, t * LANES:(t + 1) * LANES] = jnp.where(low, lo, hi).astype(BF16)


def _swa_mixer(sinks, sq, sk, sv, km, vm):
    b, s, _ = sq.shape
    nb = s // SWA_BLOCK

    def cur(width):
        return pl.BlockSpec((1, SWA_BLOCK, width), lambda i, j: (i, j, 0))

    def prev(width):
        return pl.BlockSpec((1, SWA_BLOCK, width), lambda i, j: (i, jnp.maximum(j - 1, 0), 0))

    def full(a):
        return pl.BlockSpec(a.shape, lambda i, j: (0,) * a.ndim)

    return pl.pallas_call(
        _swa_kernel,
        grid=(b, nb),
        in_specs=[pl.BlockSpec(memory_space=pltpu.SMEM),
                  cur(SQ), cur(SKV), prev(SKV), cur(SKV), prev(SKV), full(km), full(vm)],
        out_specs=cur(SQ),
        out_shape=jax.ShapeDtypeStruct((b, s, SQ), BF16),
        compiler_params=pltpu.CompilerParams(dimension_semantics=("arbitrary", "arbitrary"),
                                             vmem_limit_bytes=VMEM_LIMIT),
        name="swa_mixer",
    )(sinks, sq, sk, sk, sv, sv, km, vm)


def _out_ffn_kernel(x_ref, og_ref, os_ref, wo_ref, nf_ref, w1_ref, w2_ref, nl_ref, y_ref):
    h = (x_ref[...]
         + jnp.dot(og_ref[...], wo_ref[0:GV, :], preferred_element_type=F32)
         + jnp.dot(os_ref[...], wo_ref[GV:GV + SQ, :], preferred_element_type=F32))
    f = _rms(h, nf_ref[...]).astype(BF16)
    ff = None
    for c in range(D_FF // FF_CHUNK):
        sl = slice(c * FF_CHUNK, (c + 1) * FF_CHUNK)
        a = jnp.maximum(jnp.dot(f, w1_ref[:, sl], preferred_element_type=F32), 0.0)
        d = jnp.dot((a * a).astype(BF16), w2_ref[sl, :], preferred_element_type=F32)
        ff = d if ff is None else ff + d
    y_ref[...] = _rms(h + ff, nl_ref[...])


def _out_ffn(x2, og, osw, wo, nf, w1, w2, nl, rows):
    n = x2.shape[0]

    def row_spec(width):
        return pl.BlockSpec((rows, width), lambda i: (i, 0))

    def full(a):
        return pl.BlockSpec(a.shape, lambda i: (0,) * a.ndim, pipeline_mode=pl.Buffered(1))

    return pl.pallas_call(
        _out_ffn_kernel,
        grid=(n // rows,),
        in_specs=[row_spec(D_MODEL), row_spec(GV), row_spec(SQ),
                  full(wo), full(nf), full(w1), full(w2), full(nl)],
        out_specs=row_spec(D_MODEL),
        out_shape=jax.ShapeDtypeStruct((n, D_MODEL), F32),
        compiler_params=pltpu.CompilerParams(dimension_semantics=("arbitrary",),
                                             vmem_limit_bytes=VMEM_LIMIT),
        name="out_ffn",
    )(x2, og, osw, wo, nf, w1, w2, nl)


def _rope_tables(n_pos):
    half = ROPE_DIM // 2
    inv_freq = 1.0 / (ROPE_THETA ** (jnp.arange(0, ROPE_DIM, 2, dtype=F32) / ROPE_DIM))
    ang = jnp.arange(n_pos, dtype=jnp.int32).astype(F32)[:, None] * inv_freq[None, :]
    cos, sin = jnp.cos(ang), jnp.sin(ang)
    pad = jnp.zeros((n_pos, SWA_HD - ROPE_DIM), F32)
    zero = jnp.zeros((n_pos, half), F32)
    cos_t = jnp.concatenate([cos, cos, pad + 1.0], axis=1)
    sup_t = jnp.concatenate([-sin, zero, pad], axis=1)
    sdn_t = jnp.concatenate([zero, sin, pad], axis=1)
    rep = LANES // SWA_HD
    return tuple(jnp.tile(a, (1, rep)) for a in (cos_t, sup_t, sdn_t))


def _swa_head_order():
    order = []
    for t in range(SWA_HEADS // 2):
        order += [t, t + SWA_HEADS // 2]
    return order


def kernel(x, meta_tokens, norm_mix_w, w_in, w_gate_up, b_gate, gla_norm_w, sinks, w_out, norm_ff_w,
           w_ff1, w_ff2, final_norm_w):
    b, s, d = x.shape
    assert d == D_MODEL and s % ROWS_IN == 0 and s % ROWS_GLA == 0 and s % SWA_BLOCK == 0
    assert (b * s) % ROWS_OUT == 0 and w_in.shape[0] == 1

    w = w_in[0]
    o_gq, o_gk, o_gv, o_gr = 0, GQ, 2 * GQ, 2 * GQ + GV
    o_lr = o_gr + GV
    o_sq = o_lr + GLA_RANK
    o_sk = o_sq + SQ
    o_sv = o_sk + SKV
    order = _swa_head_order()
    sq_cols = jnp.concatenate([jnp.arange(o_sq + hh * SWA_HD, o_sq + (hh + 1) * SWA_HD) for hh in order])
    cols = jnp.concatenate([jnp.arange(o_gq, o_lr), sq_cols, jnp.arange(o_sk, o_sv + SKV),
                            jnp.arange(o_lr, o_lr + GLA_RANK)])
    w_all = jnp.pad(w[:, cols].astype(BF16), ((0, 0), (0, LANES - GLA_RANK)))
    wg = jnp.pad(w_gate_up[0], ((0, LANES - GLA_RANK), (0, 0))).astype(BF16)
    bg = b_gate[0].reshape(1, GQ)
    nw = norm_mix_w[0].reshape(1, D_MODEL)
    cq = jnp.full((1, LANES), SWA_HD ** -0.5 * LOG2_E, F32)
    wts = (nw, w_all, wg, bg, cq)

    wo = w_out[0]
    swa_rows = jnp.concatenate([jnp.arange(GV + hh * SWA_HD, GV + (hh + 1) * SWA_HD) for hh in order])
    wo = jnp.concatenate([wo[:GV], wo[swa_rows]], axis=0).astype(BF16)
    w1 = w_ff1[0].astype(BF16)
    w2 = w_ff2[0].astype(BF16)

    cos_t, sup_t, sdn_t = _rope_tables(N_META + s)
    tok_tables = tuple(a[N_META:] for a in (cos_t, sup_t, sdn_t))
    meta_tables = tuple(a[:N_META] for a in (cos_t, sup_t, sdn_t))

    m_out = _in_projection(meta_tokens.astype(F32), meta_tables, wts, N_META)
    _, m_gk, m_gv, _, m_g, _, m_sk, m_sv = m_out
    front = ((GLA_CHUNK - N_META, 0), (0, 0))
    km, vm, gm = jnp.pad(m_gk, front), jnp.pad(m_gv, front), jnp.pad(m_g, front)

    x2 = x.reshape(b * s, d)
    gq, gk, gv, gr, g, sq, sk, sv = _in_projection(x2, tok_tables, wts, ROWS_IN)

    def seq(a):
        return a.reshape(b, s, a.shape[-1])

    o_gla = _gla_mixer(seq(gq), seq(gk), seq(gv), seq(gr), seq(g), km, vm, gm,
                       gla_norm_w[0].reshape(1, GLA_DV), ROWS_GLA)
    o_swa = _swa_mixer(sinks[0].astype(F32), seq(sq), seq(sk), seq(sv), m_sk, m_sv)

    y = _out_ffn(x2, o_gla.reshape(b * s, GV), o_swa.reshape(b * s, SQ), wo,
                 norm_ff_w[0].reshape(1, D_MODEL), w1, w2, final_norm_w.reshape(1, D_MODEL), ROWS_OUT)
    return y.reshape(b, s, d)
```

```python
import jax
import jax.numpy as jnp
from jax import lax
from jax.experimental import pallas as pl
from jax.experimental.pallas import tpu as pltpu

F32 = jnp.float32
BF16 = jnp.bfloat16

D_MODEL = 1024
N_META = 16
GLA_HEADS = 4
GLA_DK = 64
GLA_DV = 128
GLA_RANK = 16
GLA_TAU = 16.0
GLA_CHUNK = 64
GLA_LEVELS = 6
GLA_UNROLL = 4
SWA_HEADS = 8
SWA_KV_HEADS = 2
SWA_HD = 64
SWA_BLOCK = 128
SWA_HALF = SWA_BLOCK // 2
ROPE_DIM = 16
ROPE_THETA = 500000.0
D_FF = 4096
EPS = 1e-5
LOG2_E = 1.4426950408889634

LANES = 128
GQ = GLA_HEADS * GLA_DK
GV = GLA_HEADS * GLA_DV
SQ = SWA_HEADS * SWA_HD
SKV = SWA_KV_HEADS * SWA_HD
FF_CHUNK = 1024

ROWS_IN = 1024
ROWS_IN_SUB = 256
W_SQ = 2 * GQ + 2 * GV
W_SK = W_SQ + SQ
W_LR = W_SK + 2 * SKV
ROWS_GLA = 512
ROWS_OUT = 512
VMEM_LIMIT = 56 * 1024 * 1024


def _nt_dot(a, b):
    return lax.dot_general(a, b, (((1,), (1,)), ((), ())), preferred_element_type=F32)


def _tn_dot(a, b):
    return lax.dot_general(a, b, (((0,), (0,)), ((), ())), preferred_element_type=F32)


def _rms(x, w):
    return x * lax.rsqrt(jnp.mean(x * x, axis=-1, keepdims=True) + EPS) * w


def _inproj_kernel(x_ref, nw_ref, w_ref, wg_ref, bg_ref, cq_ref, cos_ref, sup_ref, sdn_ref,
                   gq_ref, gk_ref, gv_ref, gr_ref, g_ref, sq_ref, sk_ref, sv_ref):
    half = ROPE_DIM // 2
    n_sub = x_ref.shape[0] // ROWS_IN_SUB if x_ref.shape[0] > ROWS_IN_SUB else 1
    sub = x_ref.shape[0] // n_sub
    for i in range(n_sub):
        rows = slice(i * sub, (i + 1) * sub)
        u = _rms(x_ref[rows, :], nw_ref[...]).astype(BF16)

        def proj(lo, width):
            return jnp.dot(u, w_ref[:, lo:lo + width], preferred_element_type=F32)

        qk = proj(0, 2 * GQ)
        gq_ref[rows, :] = (qk[:, :GQ] * (GLA_DK ** -0.5)).astype(BF16)
        gk_ref[rows, :] = qk[:, GQ:].astype(BF16)
        gv_ref[rows, :] = proj(2 * GQ, GV).astype(BF16)
        gr_ref[rows, :] = proj(2 * GQ + GV, GV).astype(BF16)

        lr = proj(W_LR, LANES).astype(BF16)
        z = jnp.dot(lr, wg_ref[...], preferred_element_type=F32) + bg_ref[...]
        g_ref[rows, :] = (jnp.minimum(z, 0.0) - jnp.log1p(jnp.exp(-jnp.abs(z)))) * (LOG2_E / GLA_TAU)

        cos, s_up, s_dn = cos_ref[rows, :], sup_ref[rows, :], sdn_ref[rows, :]

        def rope(t):
            return t * cos + pltpu.roll(t, LANES - half, axis=1) * s_up + pltpu.roll(t, half, axis=1) * s_dn

        sq = proj(W_SQ, SQ)
        for s in range(SQ // LANES):
            sl = slice(s * LANES, (s + 1) * LANES)
            sq_ref[rows, sl] = (rope(sq[:, sl]) * cq_ref[...]).astype(BF16)
        kv = proj(W_SK, 2 * SKV)
        sk_ref[rows, :] = rope(kv[:, :SKV]).astype(BF16)
        sv_ref[rows, :] = kv[:, SKV:].astype(BF16)


def _in_projection(x2, pos_tables, wts, rows):
    n = x2.shape[0]
    grid = (n // rows,)
    nblk_pos = pos_tables[0].shape[0] // rows

    def row_spec(width):
        return pl.BlockSpec((rows, width), lambda i: (i, 0))

    def pos_spec():
        return pl.BlockSpec((rows, LANES), lambda i: (i % nblk_pos, 0))

    def full(a):
        return pl.BlockSpec(a.shape, lambda i: (0,) * a.ndim)

    cos_t, sup_t, sdn_t = pos_tables
    in_specs = ([row_spec(D_MODEL)] + [full(a) for a in wts]
                + [pos_spec(), pos_spec(), pos_spec()])
    widths = (GQ, GQ, GV, GV, GQ, SQ, SKV, SKV)
    dtypes = (BF16, BF16, BF16, BF16, F32, BF16, BF16, BF16)
    return pl.pallas_call(
        _inproj_kernel,
        grid=grid,
        in_specs=in_specs,
        out_specs=[row_spec(w) for w in widths],
        out_shape=[jax.ShapeDtypeStruct((n, w), d) for w, d in zip(widths, dtypes)],
        compiler_params=pltpu.CompilerParams(dimension_semantics=("arbitrary",),
                                             vmem_limit_bytes=VMEM_LIMIT),
        name="in_projection",
    )(x2, *wts, cos_t, sup_t, sdn_t)


def _block_cumsums(g):
    rows = g.shape[0]
    row = lax.broadcasted_iota(jnp.int32, g.shape, 0)
    w, t = [g], [g]
    for k in range(GLA_LEVELS):
        s = 1 << k
        upper = (row & s) != 0
        below = pltpu.roll(t[k], s, axis=0)
        above = pltpu.roll(t[k], rows - s, axis=0)
        w.append(w[k] + jnp.where(upper, below, 0.0))
        t.append(t[k] + jnp.where(upper, below, above))
    return w, t


def _gla_levels():
    ri = lax.broadcasted_iota(jnp.int32, (GLA_CHUNK, LANES), 0)
    ci = lax.broadcasted_iota(jnp.int32, (GLA_CHUNK, LANES), 1) & (GLA_CHUNK - 1)
    x = ri ^ ci
    level = jnp.zeros_like(x)
    for b in range(GLA_LEVELS):
        level = level + (x >= (1 << b)).astype(jnp.int32)
    return jnp.where(ci > ri, -1, level)


def _gla_chunk(q, k, v, g, st, level):
    w, t = _block_cumsums(g)
    n_slab = GQ // LANES

    def low_lanes(rows):
        return lax.broadcasted_iota(jnp.int32, (rows, LANES), 1) < GLA_DK

    def block_diag(m):
        lo = low_lanes(m.shape[0])
        zero = jnp.zeros_like(m)
        return jnp.concatenate([jnp.where(lo, m, zero), jnp.where(lo, zero, m)], axis=0)

    o = None
    if level is not None:
        a = [jnp.zeros((GLA_CHUNK, LANES), F32) for _ in range(n_slab)]
        for lv in range(GLA_LEVELS + 1):
            if lv == 0:
                qt, kt = q.astype(BF16), k.astype(BF16)
            else:
                qt = (q * jnp.exp2(w[lv - 1])).astype(BF16)
                kt = (k * jnp.exp2(t[lv - 1] - w[lv - 1])).astype(BF16)
            for s in range(n_slab):
                sl = slice(s * LANES, (s + 1) * LANES)
                a[s] = jnp.where(level == lv, _nt_dot(qt[:, sl], block_diag(kt[:, sl])), a[s])

        qb = (q * jnp.exp2(w[GLA_LEVELS])).astype(BF16)
        st_b = st.astype(BF16)
        outs = []
        for s in range(n_slab):
            sl = slice(s * LANES, (s + 1) * LANES)
            v_ab = v[:, (2 * s) * GLA_DV:(2 * s + 2) * GLA_DV]
            first = lax.broadcasted_iota(jnp.int32, v_ab.shape, 1) < GLA_DV
            zero = jnp.zeros_like(v_ab)
            v_bd = jnp.concatenate([jnp.where(first, v_ab, zero), jnp.where(first, zero, v_ab)], axis=0)
            outs.append(jnp.dot(a[s].astype(BF16), v_bd, preferred_element_type=F32)
                        + _nt_dot(qb[:, sl], block_diag(st_b[:, sl])))
        o = jnp.concatenate(outs, axis=1)

    tot = t[GLA_LEVELS]
    ku = (k * jnp.exp2(tot - w[GLA_LEVELS])).astype(BF16)
    decayed = st * jnp.exp2(tot[0:1, :])
    lo = low_lanes(GLA_DV)
    new_st = []
    for s in range(n_slab):
        sl = slice(s * LANES, (s + 1) * LANES)
        upd = _tn_dot(v[:, (2 * s) * GLA_DV:(2 * s + 2) * GLA_DV], ku[:, sl])
        new_st.append(decayed[:, sl] + jnp.where(lo, upd[:GLA_DV], upd[GLA_DV:]))
    return o, jnp.concatenate(new_st, axis=1)


def _gla_kernel(q_ref, k_ref, v_ref, r_ref, g_ref, km_ref, vm_ref, gm_ref, nw_ref, o_ref, st_ref):
    @pl.when(pl.program_id(1) == 0)
    def _():
        st_ref[...] = jnp.zeros(st_ref.shape, F32)
        _, st0 = _gla_chunk(None, km_ref[...].astype(F32), vm_ref[...], gm_ref[...], st_ref[...], None)
        st_ref[...] = st0

    nw = nw_ref[...]
    level = _gla_levels()

    def body(c, carry):
        st = st_ref[...]
        for u in range(GLA_UNROLL):
            rows = pl.ds(pl.multiple_of((c * GLA_UNROLL + u) * GLA_CHUNK, GLA_CHUNK), GLA_CHUNK)
            o, st = _gla_chunk(q_ref[0, rows, :].astype(F32), k_ref[0, rows, :].astype(F32),
                               v_ref[0, rows, :], g_ref[0, rows, :], st, level)
            r = r_ref[0, rows, :].astype(F32)
            for h in range(GLA_HEADS):
                sl = slice(h * GLA_DV, (h + 1) * GLA_DV)
                rh = r[:, sl]
                o_ref[0, rows, sl] = (_rms(o[:, sl], nw) * (rh * jax.nn.sigmoid(rh))).astype(BF16)
        st_ref[...] = st
        return carry

    lax.fori_loop(0, q_ref.shape[1] // (GLA_CHUNK * GLA_UNROLL), body, 0)


def _gla_mixer(gq, gk, gv, gr, g, km, vm, gm, gla_norm_w, rows):
    b, s, _ = gq.shape

    def seq_spec(width):
        return pl.BlockSpec((1, rows, width), lambda i, j: (i, j, 0))

    def full(a):
        return pl.BlockSpec(a.shape, lambda i, j: (0,) * a.ndim)

    return pl.pallas_call(
        _gla_kernel,
        grid=(b, s // rows),
        in_specs=[seq_spec(GQ), seq_spec(GQ), seq_spec(GV), seq_spec(GV), seq_spec(GQ),
                  full(km), full(vm), full(gm), full(gla_norm_w)],
        out_specs=seq_spec(GV),
        out_shape=jax.ShapeDtypeStruct((b, s, GV), BF16),
        scratch_shapes=[pltpu.VMEM((GLA_DV, GQ), F32)],
        compiler_params=pltpu.CompilerParams(dimension_semantics=("arbitrary", "arbitrary"),
                                             vmem_limit_bytes=VMEM_LIMIT),
        name="gla_mixer",
    )(gq, gk, gv, gr, g, km, vm, gm, gla_norm_w)


def _swa_kernel(sink_ref, q_ref, kc_ref, kp_ref, vc_ref, vp_ref, km_ref, vm_ref, o_ref):
    has_prev = pl.program_id(1) > 0
    kc, kp, vc, vp = kc_ref[0], kp_ref[0], vc_ref[0], vp_ref[0]
    km, vm = km_ref[...], vm_ref[...]
    n_win = SWA_BLOCK + SWA_HALF
    n_keys = n_win + N_META
    r = lax.broadcasted_iota(jnp.int32, (SWA_HALF, n_keys), 0)
    c = lax.broadcasted_iota(jnp.int32, (SWA_HALF, n_keys), 1)
    in_band = (c > r) & (c <= r + SWA_BLOCK)
    lane = lax.broadcasted_iota(jnp.int32, (SWA_HALF, LANES), 1)
    low = lane < SWA_HD

    for half in range(2):
        rows = slice(half * SWA_HALF, (half + 1) * SWA_HALF)
        if half == 0:
            k_all = jnp.concatenate([kp, kc[:SWA_HALF], km], axis=0)
            v_all = jnp.concatenate([vp, vc[:SWA_HALF], vm], axis=0)
            n_prev = SWA_BLOCK
        else:
            k_all = jnp.concatenate([kp[SWA_HALF:], kc, km], axis=0)
            v_all = jnp.concatenate([vp[SWA_HALF:], vc, vm], axis=0)
            n_prev = SWA_HALF
        valid = (c >= n_win) | (in_band & (has_prev | (c >= n_prev)))
        bias = jnp.where(valid, 0.0, -jnp.inf)
        pieces = []
        for t in range(SWA_HEADS // 2):
            q = q_ref[0, rows, t * LANES:(t + 1) * LANES]
            pieces += [jnp.where(low, q, jnp.zeros_like(q)), jnp.where(low, jnp.zeros_like(q), q)]
        q_all = jnp.concatenate(pieces, axis=0)
        s_all = _nt_dot(q_all, k_all)
        p, denom = [], []
        for i in range(SWA_HEADS):
            s = s_all[i * SWA_HALF:(i + 1) * SWA_HALF] + bias
            sink = sink_ref[(i // 2) + (i % 2) * (SWA_HEADS // 2)] * LOG2_E
            m = jnp.maximum(jnp.max(s, axis=-1, keepdims=True), sink)
            p.append(jnp.exp2(s - m))
            denom.append(jnp.sum(p[i], axis=-1, keepdims=True) + jnp.exp2(sink - m))
        o = jnp.dot(jnp.concatenate(p, axis=0).astype(BF16), v_all, preferred_element_type=F32)
        for t in range(SWA_HEADS // 2):
            lo = o[(2 * t) * SWA_HALF:(2 * t + 1) * SWA_HALF] / denom[2 * t]
            hi = o[(2 * t + 1) * SWA_HALF:(2 * t + 2) * SWA_HALF] / denom[2 * t + 1]
            o_ref[0, rows, t * LANES:(t + 1) * LANES] = jnp.where(low, lo, hi).astype(BF16)


def _swa_mixer(sinks, sq, sk, sv, km, vm):
    b, s, _ = sq.shape
    nb = s // SWA_BLOCK

    def cur(width):
        return pl.BlockSpec((1, SWA_BLOCK, width), lambda i, j: (i, j, 0))

    def prev(width):
        return pl.BlockSpec((1, SWA_BLOCK, width), lambda i, j: (i, jnp.maximum(j - 1, 0), 0))

    def full(a):
        return pl.BlockSpec(a.shape, lambda i, j: (0,) * a.ndim)

    return pl.pallas_call(
        _swa_kernel,
        grid=(b, nb),
        in_specs=[pl.BlockSpec(memory_space=pltpu.SMEM),
                  cur(SQ), cur(SKV), prev(SKV), cur(SKV), prev(SKV), full(km), full(vm)],
        out_specs=cur(SQ),
        out_shape=jax.ShapeDtypeStruct((b, s, SQ), BF16),
        compiler_params=pltpu.CompilerParams(dimension_semantics=("arbitrary", "arbitrary"),
                                             vmem_limit_bytes=VMEM_LIMIT),
        name="swa_mixer",
    )(sinks, sq, sk, sk, sv, sv, km, vm)


def _out_ffn_kernel(x_ref, og_ref, os_ref, wo_ref, nf_ref, w1_ref, w2_ref, nl_ref, y_ref):
    h = (x_ref[...]
         + jnp.dot(og_ref[...], wo_ref[0:GV, :], preferred_element_type=F32)
         + jnp.dot(os_ref[...], wo_ref[GV:GV + SQ, :], preferred_element_type=F32))
    f = _rms(h, nf_ref[...]).astype(BF16)
    ff = None
    for c in range(D_FF // FF_CHUNK):
        sl = slice(c * FF_CHUNK, (c + 1) * FF_CHUNK)
        a = jnp.maximum(jnp.dot(f, w1_ref[:, sl], preferred_element_type=F32), 0.0)
        d = jnp.dot((a * a).astype(BF16), w2_ref[sl, :], preferred_element_type=F32)
        ff = d if ff is None else ff + d
    y_ref[...] = _rms(h + ff, nl_ref[...])


def _out_ffn(x2, og, osw, wo, nf, w1, w2, nl, rows):
    n = x2.shape[0]

    def row_spec(width):
        return pl.BlockSpec((rows, width), lambda i: (i, 0))

    def full(a):
        return pl.BlockSpec(a.shape, lambda i: (0,) * a.ndim, pipeline_mode=pl.Buffered(1))

    return pl.pallas_call(
        _out_ffn_kernel,
        grid=(n // rows,),
        in_specs=[row_spec(D_MODEL), row_spec(GV), row_spec(SQ),
                  full(wo), full(nf), full(w1), full(w2), full(nl)],
        out_specs=row_spec(D_MODEL),
        out_shape=jax.ShapeDtypeStruct((n, D_MODEL), F32),
        compiler_params=pltpu.CompilerParams(dimension_semantics=("arbitrary",),
                                             vmem_limit_bytes=VMEM_LIMIT),
        name="out_ffn",
    )(x2, og, osw, wo, nf, w1, w2, nl)


def _rope_tables(first_pos, n_pos):
    half = ROPE_DIM // 2
    inv_freq = 1.0 / (ROPE_THETA ** (jnp.arange(0, ROPE_DIM, 2, dtype=F32) / ROPE_DIM))
    ang = jnp.arange(first_pos, first_pos + n_pos, dtype=jnp.int32).astype(F32)[:, None] * inv_freq[None, :]
    cos, sin = jnp.cos(ang), jnp.sin(ang)
    pad = jnp.zeros((n_pos, SWA_HD - ROPE_DIM), F32)
    zero = jnp.zeros((n_pos, half), F32)
    cos_t = jnp.concatenate([cos, cos, pad + 1.0], axis=1)
    sup_t = jnp.concatenate([-sin, zero, pad], axis=1)
    sdn_t = jnp.concatenate([zero, sin, pad], axis=1)
    rep = LANES // SWA_HD
    return tuple(jnp.tile(a, (1, rep)) for a in (cos_t, sup_t, sdn_t))


def _swa_head_order():
    order = []
    for t in range(SWA_HEADS // 2):
        order += [t, t + SWA_HEADS // 2]
    return order


def kernel(x, meta_tokens, norm_mix_w, w_in, w_gate_up, b_gate, gla_norm_w, sinks, w_out, norm_ff_w,
           w_ff1, w_ff2, final_norm_w):
    b, s, d = x.shape
    assert d == D_MODEL and s % ROWS_IN == 0 and s % ROWS_GLA == 0 and s % SWA_BLOCK == 0
    assert (b * s) % ROWS_OUT == 0 and w_in.shape[0] == 1

    w = w_in[0]
    o_lr = 2 * GQ + 2 * GV
    o_sq = o_lr + GLA_RANK
    o_sk = o_sq + SQ
    order = _swa_head_order()
    w_all = jnp.concatenate(
        [w[:, :o_lr]] + [w[:, o_sq + hh * SWA_HD:o_sq + (hh + 1) * SWA_HD] for hh in order]
        + [w[:, o_sk:o_sk + 2 * SKV], w[:, o_lr:o_sq], jnp.zeros((D_MODEL, LANES - GLA_RANK), F32)],
        axis=1).astype(BF16)
    wg = jnp.pad(w_gate_up[0], ((0, LANES - GLA_RANK), (0, 0))).astype(BF16)
    bg = b_gate[0].reshape(1, GQ)
    nw = norm_mix_w[0].reshape(1, D_MODEL)
    cq = jnp.full((1, LANES), SWA_HD ** -0.5 * LOG2_E, F32)
    wts = (nw, w_all, wg, bg, cq)

    wo = w_out[0]
    wo = jnp.concatenate([wo[:GV]] + [wo[GV + hh * SWA_HD:GV + (hh + 1) * SWA_HD] for hh in order],
                         axis=0).astype(BF16)
    w1 = w_ff1[0].astype(BF16)
    w2 = w_ff2[0].astype(BF16)

    meta_tables = _rope_tables(0, N_META)
    tok_tables = _rope_tables(N_META, s)

    m_out = _in_projection(meta_tokens.astype(F32), meta_tables, wts, N_META)
    _, m_gk, m_gv, _, m_g, _, m_sk, m_sv = m_out
    front = ((GLA_CHUNK - N_META, 0), (0, 0))
    km, vm, gm = jnp.pad(m_gk, front), jnp.pad(m_gv, front), jnp.pad(m_g, front)

    x2 = x.reshape(b * s, d)
    gq, gk, gv, gr, g, sq, sk, sv = _in_projection(x2, tok_tables, wts, ROWS_IN)

    def seq(a):
        return a.reshape(b, s, a.shape[-1])

    o_gla = _gla_mixer(seq(gq), seq(gk), seq(gv), seq(gr), seq(g), km, vm, gm,
                       gla_norm_w[0].reshape(1, GLA_DV), ROWS_GLA)
    o_swa = _swa_mixer(sinks[0].astype(F32), seq(sq), seq(sk), seq(sv), m_sk, m_sv)

    y = _out_ffn(x2, o_gla.reshape(b * s, GV), o_swa.reshape(b * s, SQ), wo,
                 norm_ff_w[0].reshape(1, D_MODEL), w1, w2, final_norm_w.reshape(1, D_MODEL), ROWS_OUT)
    return y.reshape(b, s, d)
```

```python
import functools

import jax
import jax.numpy as jnp
from jax import lax
from jax.experimental import pallas as pl
from jax.experimental.pallas import tpu as pltpu

F32 = jnp.float32
BF16 = jnp.bfloat16

D_MODEL = 1024
N_META = 16
GLA_HEADS = 4
GLA_DK = 64
GLA_DV = 128
GLA_RANK = 16
GLA_TAU = 16.0
GLA_CHUNK = 64
GLA_LEVELS = 6
SWA_HEADS = 8
SWA_KV_HEADS = 2
SWA_HD = 64
SWA_BLOCK = 128
SWA_HALF = SWA_BLOCK // 2
ROPE_DIM = 16
ROPE_THETA = 500000.0
D_FF = 4096
EPS = 1e-5
LOG2_E = 1.4426950408889634

LANES = 128
GQ = GLA_HEADS * GLA_DK
GV = GLA_HEADS * GLA_DV
SQ = SWA_HEADS * SWA_HD
SKV = SWA_KV_HEADS * SWA_HD
FF_CHUNK = 1024

ROWS_IN = 1024
ROWS_IN_SUB = 256
W_SQ = 2 * GQ + 2 * GV
W_SK = W_SQ + SQ
W_LR = W_SK + 2 * SKV
ROWS_GLA = 512
GLA_UNROLL = 4
ROWS_OUT = 512
VMEM_LIMIT = 56 * 1024 * 1024


def _nt_dot(a, b):
    return lax.dot_general(a, b, (((1,), (1,)), ((), ())), preferred_element_type=F32)


def _tn_dot(a, b):
    return lax.dot_general(a, b, (((0,), (0,)), ((), ())), preferred_element_type=F32)


def _rms(x, w):
    return x * lax.rsqrt(jnp.mean(x * x, axis=-1, keepdims=True) + EPS) * w


def _inproj_kernel(x_ref, nw_ref, w_ref, wg_ref, bg_ref, cq_ref, cos_ref, sup_ref, sdn_ref,
                   gq_ref, gk_ref, gv_ref, gr_ref, g_ref, sq_ref, sk_ref, sv_ref):
    half = ROPE_DIM // 2
    n_sub = x_ref.shape[0] // ROWS_IN_SUB if x_ref.shape[0] > ROWS_IN_SUB else 1
    sub = x_ref.shape[0] // n_sub
    for i in range(n_sub):
        rows = slice(i * sub, (i + 1) * sub)
        u = _rms(x_ref[rows, :], nw_ref[...]).astype(BF16)

        def proj(lo, width):
            return jnp.dot(u, w_ref[:, lo:lo + width], preferred_element_type=F32)

        qk = proj(0, 2 * GQ)
        gq_ref[rows, :] = (qk[:, :GQ] * (GLA_DK ** -0.5)).astype(BF16)
        gk_ref[rows, :] = qk[:, GQ:].astype(BF16)
        gv_ref[rows, :] = proj(2 * GQ, GV).astype(BF16)
        gr_ref[rows, :] = proj(2 * GQ + GV, GV).astype(BF16)

        lr = proj(W_LR, LANES).astype(BF16)
        z = jnp.dot(lr, wg_ref[...], preferred_element_type=F32) + bg_ref[...]
        g_ref[rows, :] = (jnp.minimum(z, 0.0) - jnp.log1p(jnp.exp(-jnp.abs(z)))) * (LOG2_E / GLA_TAU)

        cos, s_up, s_dn = cos_ref[rows, :], sup_ref[rows, :], sdn_ref[rows, :]

        def rope(t):
            return t * cos + pltpu.roll(t, LANES - half, axis=1) * s_up + pltpu.roll(t, half, axis=1) * s_dn

        sq = proj(W_SQ, SQ)
        for s in range(SQ // LANES):
            sl = slice(s * LANES, (s + 1) * LANES)
            sq_ref[rows, sl] = (rope(sq[:, sl]) * cq_ref[...]).astype(BF16)
        kv = proj(W_SK, 2 * SKV)
        sk_ref[rows, :] = rope(kv[:, :SKV]).astype(BF16)
        sv_ref[rows, :] = kv[:, SKV:].astype(BF16)


def _in_projection(x2, pos_tables, wts, rows):
    n = x2.shape[0]
    grid = (n // rows,)
    nblk_pos = pos_tables[0].shape[0] // rows

    def row_spec(width):
        return pl.BlockSpec((rows, width), lambda i: (i, 0))

    def pos_spec():
        return pl.BlockSpec((rows, LANES), lambda i: (i % nblk_pos, 0))

    def full(a):
        return pl.BlockSpec(a.shape, lambda i: (0,) * a.ndim)

    cos_t, sup_t, sdn_t = pos_tables
    in_specs = ([row_spec(D_MODEL)] + [full(a) for a in wts]
                + [pos_spec(), pos_spec(), pos_spec()])
    widths = (GQ, GQ, GV, GV, GQ, SQ, SKV, SKV)
    dtypes = (BF16, BF16, BF16, BF16, F32, BF16, BF16, BF16)
    return pl.pallas_call(
        _inproj_kernel,
        grid=grid,
        in_specs=in_specs,
        out_specs=[row_spec(w) for w in widths],
        out_shape=[jax.ShapeDtypeStruct((n, w), d) for w, d in zip(widths, dtypes)],
        compiler_params=pltpu.CompilerParams(dimension_semantics=("arbitrary",),
                                             vmem_limit_bytes=VMEM_LIMIT),
        name="in_projection",
    )(x2, *wts, cos_t, sup_t, sdn_t)


def _block_cumsums(g):
    rows = g.shape[0]
    row = lax.broadcasted_iota(jnp.int32, g.shape, 0)
    w, t = [g], [g]
    for k in range(GLA_LEVELS):
        s = 1 << k
        upper = (row & s) != 0
        below = pltpu.roll(t[k], s, axis=0)
        above = pltpu.roll(t[k], rows - s, axis=0)
        w.append(w[k] + jnp.where(upper, below, 0.0))
        t.append(t[k] + jnp.where(upper, below, above))
    return w, t


def _gla_levels():
    ri = lax.broadcasted_iota(jnp.int32, (GLA_CHUNK, LANES), 0)
    ci = lax.broadcasted_iota(jnp.int32, (GLA_CHUNK, LANES), 1) & (GLA_CHUNK - 1)
    x = ri ^ ci
    level = jnp.zeros_like(x)
    for b in range(GLA_LEVELS):
        level = level + (x >= (1 << b)).astype(jnp.int32)
    return jnp.where(ci > ri, -1, level)


def _gla_chunk(q, k, v, g, st, level):
    w, t = _block_cumsums(g)
    n_slab = GQ // LANES

    def low_lanes(rows):
        return lax.broadcasted_iota(jnp.int32, (rows, LANES), 1) < GLA_DK

    def block_diag(m):
        lo = low_lanes(m.shape[0])
        zero = jnp.zeros_like(m)
        return jnp.concatenate([jnp.where(lo, m, zero), jnp.where(lo, zero, m)], axis=0)

    o = None
    if level is not None:
        a = [jnp.zeros((GLA_CHUNK, LANES), F32) for _ in range(n_slab)]
        for lv in range(GLA_LEVELS + 1):
            if lv == 0:
                qt, kt = q.astype(BF16), k.astype(BF16)
            else:
                qt = (q * jnp.exp2(w[lv - 1])).astype(BF16)
                kt = (k * jnp.exp2(t[lv - 1] - w[lv - 1])).astype(BF16)
            for s in range(n_slab):
                sl = slice(s * LANES, (s + 1) * LANES)
                a[s] = jnp.where(level == lv, _nt_dot(qt[:, sl], block_diag(kt[:, sl])), a[s])

        qb = (q * jnp.exp2(w[GLA_LEVELS])).astype(BF16)
        st_b = st.astype(BF16)
        outs = []
        for s in range(n_slab):
            sl = slice(s * LANES, (s + 1) * LANES)
            v_ab = v[:, (2 * s) * GLA_DV:(2 * s + 2) * GLA_DV]
            first = lax.broadcasted_iota(jnp.int32, v_ab.shape, 1) < GLA_DV
            zero = jnp.zeros_like(v_ab)
            v_bd = jnp.concatenate([jnp.where(first, v_ab, zero), jnp.where(first, zero, v_ab)], axis=0)
            outs.append(jnp.dot(a[s].astype(BF16), v_bd, preferred_element_type=F32)
                        + _nt_dot(qb[:, sl], block_diag(st_b[:, sl])))
        o = jnp.concatenate(outs, axis=1)

    tot = t[GLA_LEVELS]
    ku = (k * jnp.exp2(tot - w[GLA_LEVELS])).astype(BF16)
    decayed = st * jnp.exp2(tot[0:1, :])
    lo = low_lanes(GLA_DV)
    new_st = []
    for s in range(n_slab):
        sl = slice(s * LANES, (s + 1) * LANES)
        upd = _tn_dot(v[:, (2 * s) * GLA_DV:(2 * s + 2) * GLA_DV], ku[:, sl])
        new_st.append(decayed[:, sl] + jnp.where(lo, upd[:GLA_DV], upd[GLA_DV:]))
    return o, jnp.concatenate(new_st, axis=1)


def _gla_gate(o, r, nw):
    out = []
    for h in range(GLA_HEADS):
        sl = slice(h * GLA_DV, (h + 1) * GLA_DV)
        rh = r[:, sl].astype(F32)
        out.append((_rms(o[:, sl], nw) * (rh * jax.nn.sigmoid(rh))).astype(BF16))
    return jnp.concatenate(out, axis=1)


def _gla_kernel(q_ref, k_ref, v_ref, r_ref, g_ref, km_ref, vm_ref, gm_ref, nw_ref, o_ref, st_ref):
    @pl.when(pl.program_id(1) == 0)
    def _():
        st_ref[...] = jnp.zeros(st_ref.shape, F32)
        _, st0 = _gla_chunk(None, km_ref[...].astype(F32), vm_ref[...], gm_ref[...], st_ref[...], None)
        st_ref[...] = st0

    nw = nw_ref[...]
    level = _gla_levels()

    def body(c, carry):
        st = st_ref[...]
        for u in range(GLA_UNROLL):
            rows = pl.ds(pl.multiple_of((c * GLA_UNROLL + u) * GLA_CHUNK, GLA_CHUNK), GLA_CHUNK)
            o, st = _gla_chunk(q_ref[0, rows, :].astype(F32), k_ref[0, rows, :].astype(F32),
                               v_ref[0, rows, :], g_ref[0, rows, :], st, level)
            o_ref[0, rows, :] = _gla_gate(o, r_ref[0, rows, :], nw)
        st_ref[...] = st
        return carry

    lax.fori_loop(0, q_ref.shape[1] // (GLA_CHUNK * GLA_UNROLL), body, 0)


def _gla_mixer(gq, gk, gv, gr, g, km, vm, gm, gla_norm_w, rows):
    b, s, _ = gq.shape

    def seq_spec(width):
        return pl.BlockSpec((1, rows, width), lambda i, j: (i, j, 0))

    def full(a):
        return pl.BlockSpec(a.shape, lambda i, j: (0,) * a.ndim)

    return pl.pallas_call(
        _gla_kernel,
        grid=(b, s // rows),
        in_specs=[seq_spec(GQ), seq_spec(GQ), seq_spec(GV), seq_spec(GV), seq_spec(GQ),
                  full(km), full(vm), full(gm), full(gla_norm_w)],
        out_specs=seq_spec(GV),
        out_shape=jax.ShapeDtypeStruct((b, s, GV), BF16),
        scratch_shapes=[pltpu.VMEM((GLA_DV, GQ), F32)],
        compiler_params=pltpu.CompilerParams(dimension_semantics=("arbitrary", "arbitrary"),
                                             vmem_limit_bytes=VMEM_LIMIT),
        name="gla_mixer",
    )(gq, gk, gv, gr, g, km, vm, gm, gla_norm_w)


def _swa_block(sink_ref, q, kc, kp, vc, vp, km, vm, has_prev):
    n_win = SWA_BLOCK + SWA_HALF
    n_keys = n_win + N_META
    r = lax.broadcasted_iota(jnp.int32, (SWA_HALF, n_keys), 0)
    c = lax.broadcasted_iota(jnp.int32, (SWA_HALF, n_keys), 1)
    in_band = (c > r) & (c <= r + SWA_BLOCK)
    lane = lax.broadcasted_iota(jnp.int32, (SWA_HALF, LANES), 1)
    low = lane < SWA_HD

    halves = []
    for half in range(2):
        rows = slice(half * SWA_HALF, (half + 1) * SWA_HALF)
        if half == 0:
            k_all = jnp.concatenate([kp, kc[:SWA_HALF], km], axis=0)
            v_all = jnp.concatenate([vp, vc[:SWA_HALF], vm], axis=0)
            n_prev = SWA_BLOCK
        else:
            k_all = jnp.concatenate([kp[SWA_HALF:], kc, km], axis=0)
            v_all = jnp.concatenate([vp[SWA_HALF:], vc, vm], axis=0)
            n_prev = SWA_HALF
        valid = (c >= n_win) | in_band
        if has_prev is not None:
            valid = (c >= n_win) | (in_band & (has_prev | (c >= n_prev)))
        bias = jnp.where(valid, 0.0, -jnp.inf)
        pieces = []
        for t in range(SWA_HEADS // 2):
            qt = q[rows, t * LANES:(t + 1) * LANES]
            pieces += [jnp.where(low, qt, jnp.zeros_like(qt)), jnp.where(low, jnp.zeros_like(qt), qt)]
        q_all = jnp.concatenate(pieces, axis=0)
        s_all = _nt_dot(q_all, k_all)
        p, denom = [], []
        for i in range(SWA_HEADS):
            s = s_all[i * SWA_HALF:(i + 1) * SWA_HALF] + bias
            sink = sink_ref[(i // 2) + (i % 2) * (SWA_HEADS // 2)] * LOG2_E
            m = jnp.maximum(jnp.max(s, axis=-1, keepdims=True), sink)
            p.append(jnp.exp2(s - m))
            denom.append(jnp.sum(p[i], axis=-1, keepdims=True) + jnp.exp2(sink - m))
        o = jnp.dot(jnp.concatenate(p, axis=0).astype(BF16), v_all, preferred_element_type=F32)
        slabs = []
        for t in range(SWA_HEADS // 2):
            lo = o[(2 * t) * SWA_HALF:(2 * t + 1) * SWA_HALF] / denom[2 * t]
            hi = o[(2 * t + 1) * SWA_HALF:(2 * t + 2) * SWA_HALF] / denom[2 * t + 1]
            slabs.append(jnp.where(low, lo, hi).astype(BF16))
        halves.append(jnp.concatenate(slabs, axis=1))
    return jnp.concatenate(halves, axis=0)


def _swa_ffn_kernel(tiles_per_seq, sink_ref, sq_ref, sk_ref, sv_ref, skp_ref, svp_ref, km_ref, vm_ref,
                    x_ref, og_ref, wo_ref, nf_ref, w1_ref, w2_ref, nl_ref, y_ref, os_ref):
    s = pl.program_id(0)
    tile = jnp.minimum(s, pl.num_programs(0) - 2)
    seq_start = lax.rem(tile, tiles_per_seq) == 0
    cur = lax.rem(s, 2)

    @pl.when(s == 0)
    def _():
        os_ref[...] = jnp.zeros(os_ref.shape, BF16)

    def swa_block(j):
        rows = slice(j * SWA_BLOCK, (j + 1) * SWA_BLOCK)
        if j == 0:
            kp, vp, has_prev = skp_ref[...], svp_ref[...], jnp.logical_not(seq_start)
        else:
            before = slice((j - 1) * SWA_BLOCK, j * SWA_BLOCK)
            kp, vp, has_prev = sk_ref[before, :], sv_ref[before, :], None
        os_ref[cur, rows, :] = _swa_block(sink_ref, sq_ref[rows, :], sk_ref[rows, :], kp, sv_ref[rows, :], vp,
                                          km_ref[...], vm_ref[...], has_prev)

    n_ff = D_FF // FF_CHUNK
    swa_per_ff = x_ref.shape[0] // SWA_BLOCK // n_ff

    h = (x_ref[...]
         + jnp.dot(og_ref[...], wo_ref[0:GV, :], preferred_element_type=F32)
         + jnp.dot(os_ref[1 - cur], wo_ref[GV:GV + SQ, :], preferred_element_type=F32))
    f = _rms(h, nf_ref[...]).astype(BF16)
    ff = None
    for c in range(n_ff):
        for j in range(swa_per_ff):
            swa_block(c * swa_per_ff + j)
        sl = slice(c * FF_CHUNK, (c + 1) * FF_CHUNK)
        a = jnp.maximum(jnp.dot(f, w1_ref[:, sl], preferred_element_type=F32), 0.0)
        d = jnp.dot((a * a).astype(BF16), w2_ref[sl, :], preferred_element_type=F32)
        ff = d if ff is None else ff + d
    y_ref[...] = _rms(h + ff, nl_ref[...])


def _swa_ffn(sinks, sq, sk, sv, km, vm, x2, og, wo, nf, w1, w2, nl, rows, tiles_per_seq):
    n = x2.shape[0]
    n_tiles = n // rows
    prev_per_tile = rows // SWA_BLOCK

    def mix_spec(width):
        return pl.BlockSpec((rows, width), lambda i: (jnp.minimum(i, n_tiles - 1), 0))

    def prev_spec(width):
        return pl.BlockSpec((SWA_BLOCK, width),
                            lambda i: (jnp.maximum(jnp.minimum(i, n_tiles - 1) * prev_per_tile - 1, 0), 0))

    def ffn_spec(width):
        return pl.BlockSpec((rows, width), lambda i: (jnp.maximum(i - 1, 0), 0))

    def full(a):
        return pl.BlockSpec(a.shape, lambda i: (0,) * a.ndim, pipeline_mode=pl.Buffered(1))

    weights = (wo, nf, w1, w2, nl)
    return pl.pallas_call(
        functools.partial(_swa_ffn_kernel, tiles_per_seq),
        grid=(n_tiles + 1,),
        in_specs=([pl.BlockSpec(memory_space=pltpu.SMEM),
                   mix_spec(SQ), mix_spec(SKV), mix_spec(SKV), prev_spec(SKV), prev_spec(SKV), full(km), full(vm),
                   ffn_spec(D_MODEL), ffn_spec(GV)] + [full(a) for a in weights]),
        out_specs=ffn_spec(D_MODEL),
        out_shape=jax.ShapeDtypeStruct((n, D_MODEL), F32),
        scratch_shapes=[pltpu.VMEM((2, rows, SQ), BF16)],
        compiler_params=pltpu.CompilerParams(dimension_semantics=("arbitrary",),
                                             vmem_limit_bytes=VMEM_LIMIT),
        name="swa_ffn",
    )(sinks, sq, sk, sv, sk, sv, km, vm, x2, og, *weights)


def _rope_tables(first_pos, n_pos):
    half = ROPE_DIM // 2
    inv_freq = 1.0 / (ROPE_THETA ** (jnp.arange(0, ROPE_DIM, 2, dtype=F32) / ROPE_DIM))
    ang = jnp.arange(first_pos, first_pos + n_pos, dtype=jnp.int32).astype(F32)[:, None] * inv_freq[None, :]
    cos, sin = jnp.cos(ang), jnp.sin(ang)
    pad = jnp.zeros((n_pos, SWA_HD - ROPE_DIM), F32)
    zero = jnp.zeros((n_pos, half), F32)
    cos_t = jnp.concatenate([cos, cos, pad + 1.0], axis=1)
    sup_t = jnp.concatenate([-sin, zero, pad], axis=1)
    sdn_t = jnp.concatenate([zero, sin, pad], axis=1)
    rep = LANES // SWA_HD
    return tuple(jnp.tile(a, (1, rep)) for a in (cos_t, sup_t, sdn_t))


def _swa_head_order():
    order = []
    for t in range(SWA_HEADS // 2):
        order += [t, t + SWA_HEADS // 2]
    return order


def kernel(x, meta_tokens, norm_mix_w, w_in, w_gate_up, b_gate, gla_norm_w, sinks, w_out, norm_ff_w,
           w_ff1, w_ff2, final_norm_w):
    b, s, d = x.shape
    assert d == D_MODEL and s % ROWS_IN == 0 and s % ROWS_OUT == 0 and w_in.shape[0] == 1
    assert s % ROWS_GLA == 0 and ROWS_GLA % (GLA_CHUNK * GLA_UNROLL) == 0
    assert ROWS_OUT % (SWA_BLOCK * (D_FF // FF_CHUNK)) == 0

    w = w_in[0]
    o_lr = 2 * GQ + 2 * GV
    o_sq = o_lr + GLA_RANK
    o_sk = o_sq + SQ
    order = _swa_head_order()
    w_all = jnp.concatenate(
        [w[:, :o_lr]] + [w[:, o_sq + hh * SWA_HD:o_sq + (hh + 1) * SWA_HD] for hh in order]
        + [w[:, o_sk:o_sk + 2 * SKV], w[:, o_lr:o_sq], jnp.zeros((D_MODEL, LANES - GLA_RANK), F32)],
        axis=1).astype(BF16)
    wg = jnp.pad(w_gate_up[0], ((0, LANES - GLA_RANK), (0, 0))).astype(BF16)
    bg = b_gate[0].reshape(1, GQ)
    nw = norm_mix_w[0].reshape(1, D_MODEL)
    cq = jnp.full((1, LANES), SWA_HD ** -0.5 * LOG2_E, F32)
    wts = (nw, w_all, wg, bg, cq)

    wo = w_out[0]
    wo = jnp.concatenate([wo[:GV]] + [wo[GV + hh * SWA_HD:GV + (hh + 1) * SWA_HD] for hh in order],
                         axis=0).astype(BF16)
    w1 = w_ff1[0].astype(BF16)
    w2 = w_ff2[0].astype(BF16)

    meta_tables = _rope_tables(0, N_META)
    tok_tables = _rope_tables(N_META, s)

    m_out = _in_projection(meta_tokens.astype(F32), meta_tables, wts, N_META)
    _, m_gk, m_gv, _, m_g, _, m_sk, m_sv = m_out
    front = ((GLA_CHUNK - N_META, 0), (0, 0))
    km, vm, gm = jnp.pad(m_gk, front), jnp.pad(m_gv, front), jnp.pad(m_g, front)

    x2 = x.reshape(b * s, d)
    gq, gk, gv, gr, g, sq, sk, sv = _in_projection(x2, tok_tables, wts, ROWS_IN)

    def seq(a):
        return a.reshape(b, s, a.shape[-1])

    o_gla = _gla_mixer(seq(gq), seq(gk), seq(gv), seq(gr), seq(g), km, vm, gm,
                       gla_norm_w[0].reshape(1, GLA_DV), ROWS_GLA)
    y = _swa_ffn(sinks[0].astype(F32), sq, sk, sv, m_sk, m_sv, x2, o_gla.reshape(b * s, GV), wo,
                 norm_ff_w[0].reshape(1, D_MODEL), w1, w2, final_norm_w.reshape(1, D_MODEL),
                 ROWS_OUT, s // ROWS_OUT)
    return y.reshape(b, s, d)
```

```python
import functools

import jax
import jax.numpy as jnp
from jax import lax
from jax.experimental import pallas as pl
from jax.experimental.pallas import tpu as pltpu

F32 = jnp.float32
BF16 = jnp.bfloat16

D_MODEL = 1024
N_META = 16
GLA_HEADS = 4
GLA_DK = 64
GLA_DV = 128
GLA_RANK = 16
GLA_TAU = 16.0
GLA_CHUNK = 64
GLA_LEVELS = 6
SWA_HEADS = 8
SWA_KV_HEADS = 2
SWA_HD = 64
SWA_BLOCK = 128
SWA_HALF = SWA_BLOCK // 2
ROPE_DIM = 16
ROPE_THETA = 500000.0
D_FF = 4096
EPS = 1e-5
LOG2_E = 1.4426950408889634

LANES = 128
GQ = GLA_HEADS * GLA_DK
GV = GLA_HEADS * GLA_DV
SQ = SWA_HEADS * SWA_HD
SKV = SWA_KV_HEADS * SWA_HD
FF_CHUNK = 1024

ROWS_IN = 1024
ROWS_IN_SUB = 256
W_SQ = 2 * GQ + 2 * GV
W_SK = W_SQ + SQ
W_LR = W_SK + 2 * SKV
ROWS_GLA = 512
GLA_UNROLL = 4
ROWS_OUT = 512
VMEM_LIMIT = 56 * 1024 * 1024


def _nt_dot(a, b):
    return lax.dot_general(a, b, (((1,), (1,)), ((), ())), preferred_element_type=F32)


def _tn_dot(a, b):
    return lax.dot_general(a, b, (((0,), (0,)), ((), ())), preferred_element_type=F32)


def _rms(x, w):
    return x * lax.rsqrt(jnp.mean(x * x, axis=-1, keepdims=True) + EPS) * w


def _inproj_kernel(x_ref, nw_ref, w_ref, wg_ref, bg_ref, cq_ref, cos_ref, sup_ref, sdn_ref,
                   gq_ref, gk_ref, gv_ref, gr_ref, g_ref, sq_ref, sk_ref, sv_ref):
    half = ROPE_DIM // 2
    n_sub = x_ref.shape[0] // ROWS_IN_SUB if x_ref.shape[0] > ROWS_IN_SUB else 1
    sub = x_ref.shape[0] // n_sub
    for i in range(n_sub):
        rows = slice(i * sub, (i + 1) * sub)
        u = _rms(x_ref[rows, :], nw_ref[...]).astype(BF16)

        def proj(lo, width):
            return jnp.dot(u, w_ref[:, lo:lo + width], preferred_element_type=F32)

        qk = proj(0, 2 * GQ)
        gq_ref[rows, :] = (qk[:, :GQ] * (GLA_DK ** -0.5)).astype(BF16)
        gk_ref[rows, :] = qk[:, GQ:].astype(BF16)
        gv_ref[rows, :] = proj(2 * GQ, GV).astype(BF16)
        gr_ref[rows, :] = proj(2 * GQ + GV, GV).astype(BF16)

        lr = proj(W_LR, LANES).astype(BF16)
        z = jnp.dot(lr, wg_ref[...], preferred_element_type=F32) + bg_ref[...]
        g_ref[rows, :] = (jnp.minimum(z, 0.0) - jnp.log1p(jnp.exp(-jnp.abs(z)))) * (LOG2_E / GLA_TAU)

        cos, s_up, s_dn = cos_ref[rows, :], sup_ref[rows, :], sdn_ref[rows, :]

        def rope(t):
            return t * cos + pltpu.roll(t, LANES - half, axis=1) * s_up + pltpu.roll(t, half, axis=1) * s_dn

        sq = proj(W_SQ, SQ)
        for s in range(SQ // LANES):
            sl = slice(s * LANES, (s + 1) * LANES)
            sq_ref[rows, sl] = (rope(sq[:, sl]) * cq_ref[...]).astype(BF16)
        kv = proj(W_SK, 2 * SKV)
        sk_ref[rows, :] = rope(kv[:, :SKV]).astype(BF16)
        sv_ref[rows, :] = kv[:, SKV:].astype(BF16)


def _in_projection(x2, pos_tables, wts, rows):
    n = x2.shape[0]
    grid = (n // rows,)
    nblk_pos = pos_tables[0].shape[0] // rows

    def row_spec(width):
        return pl.BlockSpec((rows, width), lambda i: (i, 0))

    def pos_spec():
        return pl.BlockSpec((rows, LANES), lambda i: (i % nblk_pos, 0))

    def full(a):
        return pl.BlockSpec(a.shape, lambda i: (0,) * a.ndim)

    cos_t, sup_t, sdn_t = pos_tables
    in_specs = ([row_spec(D_MODEL)] + [full(a) for a in wts]
                + [pos_spec(), pos_spec(), pos_spec()])
    widths = (GQ, GQ, GV, GV, GQ, SQ, SKV, SKV)
    dtypes = (BF16, BF16, BF16, BF16, F32, BF16, BF16, BF16)
    return pl.pallas_call(
        _inproj_kernel,
        grid=grid,
        in_specs=in_specs,
        out_specs=[row_spec(w) for w in widths],
        out_shape=[jax.ShapeDtypeStruct((n, w), d) for w, d in zip(widths, dtypes)],
        compiler_params=pltpu.CompilerParams(dimension_semantics=("arbitrary",),
                                             vmem_limit_bytes=VMEM_LIMIT),
        name="in_projection",
    )(x2, *wts, cos_t, sup_t, sdn_t)


def _block_cumsums(g):
    rows = g.shape[0]
    row = lax.broadcasted_iota(jnp.int32, g.shape, 0)
    w, t = [g], [g]
    for k in range(GLA_LEVELS):
        s = 1 << k
        upper = (row & s) != 0
        below = pltpu.roll(t[k], s, axis=0)
        above = pltpu.roll(t[k], rows - s, axis=0)
        w.append(w[k] + jnp.where(upper, below, 0.0))
        t.append(t[k] + jnp.where(upper, below, above))
    return w, t


def _gla_levels():
    ri = lax.broadcasted_iota(jnp.int32, (GLA_CHUNK, LANES), 0)
    ci = lax.broadcasted_iota(jnp.int32, (GLA_CHUNK, LANES), 1) & (GLA_CHUNK - 1)
    x = ri ^ ci
    level = jnp.zeros_like(x)
    for b in range(GLA_LEVELS):
        level = level + (x >= (1 << b)).astype(jnp.int32)
    return jnp.where(ci > ri, -1, level)


def _gla_chunk(q, k, v, g, st, level):
    w, t = _block_cumsums(g)
    n_slab = GQ // LANES

    def block_diag_t(m):
        r = m.shape[0]
        m2 = jnp.concatenate([m, m], axis=0).T
        same_head = ((lax.broadcasted_iota(jnp.int32, m2.shape, 0) < GLA_DK)
                     == (lax.broadcasted_iota(jnp.int32, m2.shape, 1) < r))
        return jnp.where(same_head, m2, jnp.zeros_like(m2))

    o = None
    if level is not None:
        a = [jnp.zeros((GLA_CHUNK, LANES), F32) for _ in range(n_slab)]
        for lv in range(GLA_LEVELS + 1):
            if lv == 0:
                qt, kt = q.astype(BF16), k.astype(BF16)
            else:
                qt = (q * jnp.exp2(w[lv - 1])).astype(BF16)
                kt = (k * jnp.exp2(t[lv - 1] - w[lv - 1])).astype(BF16)
            for s in range(n_slab):
                sl = slice(s * LANES, (s + 1) * LANES)
                scores = jnp.dot(qt[:, sl], block_diag_t(kt[:, sl]), preferred_element_type=F32)
                a[s] = jnp.where(level == lv, scores, a[s])

        qb = (q * jnp.exp2(w[GLA_LEVELS])).astype(BF16)
        st_b = st.astype(BF16)
        outs = []
        for s in range(n_slab):
            sl = slice(s * LANES, (s + 1) * LANES)
            v_ab = v[:, (2 * s) * GLA_DV:(2 * s + 2) * GLA_DV]
            first = lax.broadcasted_iota(jnp.int32, v_ab.shape, 1) < GLA_DV
            zero = jnp.zeros_like(v_ab)
            v_bd = jnp.concatenate([jnp.where(first, v_ab, zero), jnp.where(first, zero, v_ab)], axis=0)
            outs.append(jnp.dot(a[s].astype(BF16), v_bd, preferred_element_type=F32)
                        + jnp.dot(qb[:, sl], block_diag_t(st_b[:, sl]), preferred_element_type=F32))
        o = jnp.concatenate(outs, axis=1)

    tot = t[GLA_LEVELS]
    ku = (k * jnp.exp2(tot - w[GLA_LEVELS])).astype(BF16)
    decayed = st * jnp.exp2(tot[0:1, :])
    lo = lax.broadcasted_iota(jnp.int32, (GLA_DV, LANES), 1) < GLA_DK
    new_st = []
    for s in range(n_slab):
        sl = slice(s * LANES, (s + 1) * LANES)
        upd = _tn_dot(v[:, (2 * s) * GLA_DV:(2 * s + 2) * GLA_DV], ku[:, sl])
        new_st.append(decayed[:, sl] + jnp.where(lo, upd[:GLA_DV], upd[GLA_DV:]))
    return o, jnp.concatenate(new_st, axis=1)


def _gla_gate(o, r, nw):
    out = []
    for h in range(GLA_HEADS):
        sl = slice(h * GLA_DV, (h + 1) * GLA_DV)
        rh = r[:, sl].astype(F32)
        out.append((_rms(o[:, sl], nw) * (rh * jax.nn.sigmoid(rh))).astype(BF16))
    return jnp.concatenate(out, axis=1)


def _gla_kernel(q_ref, k_ref, v_ref, r_ref, g_ref, km_ref, vm_ref, gm_ref, nw_ref, o_ref, st_ref):
    @pl.when(pl.program_id(1) == 0)
    def _():
        st_ref[...] = jnp.zeros(st_ref.shape, F32)
        _, st0 = _gla_chunk(None, km_ref[...].astype(F32), vm_ref[...], gm_ref[...], st_ref[...], None)
        st_ref[...] = st0

    nw = nw_ref[...]
    level = _gla_levels()

    def body(c, carry):
        st = st_ref[...]
        for u in range(GLA_UNROLL):
            rows = pl.ds(pl.multiple_of((c * GLA_UNROLL + u) * GLA_CHUNK, GLA_CHUNK), GLA_CHUNK)
            o, st = _gla_chunk(q_ref[0, rows, :].astype(F32), k_ref[0, rows, :].astype(F32),
                               v_ref[0, rows, :], g_ref[0, rows, :], st, level)
            o_ref[0, rows, :] = _gla_gate(o, r_ref[0, rows, :], nw)
        st_ref[...] = st
        return carry

    lax.fori_loop(0, q_ref.shape[1] // (GLA_CHUNK * GLA_UNROLL), body, 0)


def _gla_mixer(gq, gk, gv, gr, g, km, vm, gm, gla_norm_w, rows):
    b, s, _ = gq.shape

    def seq_spec(width):
        return pl.BlockSpec((1, rows, width), lambda i, j: (i, j, 0))

    def full(a):
        return pl.BlockSpec(a.shape, lambda i, j: (0,) * a.ndim)

    return pl.pallas_call(
        _gla_kernel,
        grid=(b, s // rows),
        in_specs=[seq_spec(GQ), seq_spec(GQ), seq_spec(GV), seq_spec(GV), seq_spec(GQ),
                  full(km), full(vm), full(gm), full(gla_norm_w)],
        out_specs=seq_spec(GV),
        out_shape=jax.ShapeDtypeStruct((b, s, GV), BF16),
        scratch_shapes=[pltpu.VMEM((GLA_DV, GQ), F32)],
        compiler_params=pltpu.CompilerParams(dimension_semantics=("arbitrary", "arbitrary"),
                                             vmem_limit_bytes=VMEM_LIMIT),
        name="gla_mixer",
    )(gq, gk, gv, gr, g, km, vm, gm, gla_norm_w)


def _swa_block(sink_ref, q, kc, kp, vc, vp, km, vm, has_prev):
    n_win = SWA_BLOCK + SWA_HALF
    n_keys = n_win + N_META
    r = lax.broadcasted_iota(jnp.int32, (SWA_HALF, n_keys), 0)
    c = lax.broadcasted_iota(jnp.int32, (SWA_HALF, n_keys), 1)
    in_band = (c > r) & (c <= r + SWA_BLOCK)
    lane = lax.broadcasted_iota(jnp.int32, (SWA_HALF, LANES), 1)
    low = lane < SWA_HD

    halves = []
    for half in range(2):
        rows = slice(half * SWA_HALF, (half + 1) * SWA_HALF)
        if half == 0:
            k_all = jnp.concatenate([kp, kc[:SWA_HALF], km], axis=0)
            v_all = jnp.concatenate([vp, vc[:SWA_HALF], vm], axis=0)
            n_prev = SWA_BLOCK
        else:
            k_all = jnp.concatenate([kp[SWA_HALF:], kc, km], axis=0)
            v_all = jnp.concatenate([vp[SWA_HALF:], vc, vm], axis=0)
            n_prev = SWA_HALF
        valid = (c >= n_win) | in_band
        if has_prev is not None:
            valid = (c >= n_win) | (in_band & (has_prev | (c >= n_prev)))
        bias = jnp.where(valid, 0.0, -jnp.inf)
        pieces = []
        for t in range(SWA_HEADS // 2):
            qt = q[rows, t * LANES:(t + 1) * LANES]
            pieces += [jnp.where(low, qt, jnp.zeros_like(qt)), jnp.where(low, jnp.zeros_like(qt), qt)]
        q_all = jnp.concatenate(pieces, axis=0)
        s_all = _nt_dot(q_all, k_all)
        p, denom = [], []
        for i in range(SWA_HEADS):
            s = s_all[i * SWA_HALF:(i + 1) * SWA_HALF] + bias
            sink = sink_ref[(i // 2) + (i % 2) * (SWA_HEADS // 2)] * LOG2_E
            m = jnp.maximum(jnp.max(s, axis=-1, keepdims=True), sink)
            p.append(jnp.exp2(s - m))
            denom.append(jnp.sum(p[i], axis=-1, keepdims=True) + jnp.exp2(sink - m))
        o = jnp.dot(jnp.concatenate(p, axis=0).astype(BF16), v_all, preferred_element_type=F32)
        slabs = []
        for t in range(SWA_HEADS // 2):
            lo = o[(2 * t) * SWA_HALF:(2 * t + 1) * SWA_HALF] / denom[2 * t]
            hi = o[(2 * t + 1) * SWA_HALF:(2 * t + 2) * SWA_HALF] / denom[2 * t + 1]
            slabs.append(jnp.where(low, lo, hi).astype(BF16))
        halves.append(jnp.concatenate(slabs, axis=1))
    return jnp.concatenate(halves, axis=0)


def _swa_ffn_kernel(tiles_per_seq, sink_ref, sq_ref, sk_ref, sv_ref, skp_ref, svp_ref, km_ref, vm_ref,
                    x_ref, og_ref, wo_ref, nf_ref, w1_ref, w2_ref, nl_ref, y_ref, os_ref):
    s = pl.program_id(0)
    tile = jnp.minimum(s, pl.num_programs(0) - 2)
    seq_start = lax.rem(tile, tiles_per_seq) == 0
    cur = lax.rem(s, 2)

    @pl.when(s == 0)
    def _():
        os_ref[...] = jnp.zeros(os_ref.shape, BF16)

    def swa_block(j):
        rows = slice(j * SWA_BLOCK, (j + 1) * SWA_BLOCK)
        if j == 0:
            kp, vp, has_prev = skp_ref[...], svp_ref[...], jnp.logical_not(seq_start)
        else:
            before = slice((j - 1) * SWA_BLOCK, j * SWA_BLOCK)
            kp, vp, has_prev = sk_ref[before, :], sv_ref[before, :], None
        os_ref[cur, rows, :] = _swa_block(sink_ref, sq_ref[rows, :], sk_ref[rows, :], kp, sv_ref[rows, :], vp,
                                          km_ref[...], vm_ref[...], has_prev)

    n_ff = D_FF // FF_CHUNK
    swa_per_ff = x_ref.shape[0] // SWA_BLOCK // n_ff

    h = (x_ref[...]
         + jnp.dot(og_ref[...], wo_ref[0:GV, :], preferred_element_type=F32)
         + jnp.dot(os_ref[1 - cur], wo_ref[GV:GV + SQ, :], preferred_element_type=F32))
    f = _rms(h, nf_ref[...]).astype(BF16)
    ff = None
    for c in range(n_ff):
        for j in range(swa_per_ff):
            swa_block(c * swa_per_ff + j)
        sl = slice(c * FF_CHUNK, (c + 1) * FF_CHUNK)
        a = jnp.maximum(jnp.dot(f, w1_ref[:, sl], preferred_element_type=F32), 0.0)
        d = jnp.dot((a * a).astype(BF16), w2_ref[sl, :], preferred_element_type=F32)
        ff = d if ff is None else ff + d
    y_ref[...] = _rms(h + ff, nl_ref[...])


def _swa_ffn(sinks, sq, sk, sv, km, vm, x2, og, wo, nf, w1, w2, nl, rows, tiles_per_seq):
    n = x2.shape[0]
    n_tiles = n // rows
    prev_per_tile = rows // SWA_BLOCK

    def mix_spec(width):
        return pl.BlockSpec((rows, width), lambda i: (jnp.minimum(i, n_tiles - 1), 0))

    def prev_spec(width):
        return pl.BlockSpec((SWA_BLOCK, width),
                            lambda i: (jnp.maximum(jnp.minimum(i, n_tiles - 1) * prev_per_tile - 1, 0), 0))

    def ffn_spec(width):
        return pl.BlockSpec((rows, width), lambda i: (jnp.maximum(i - 1, 0), 0))

    def full(a):
        return pl.BlockSpec(a.shape, lambda i: (0,) * a.ndim, pipeline_mode=pl.Buffered(1))

    weights = (wo, nf, w1, w2, nl)
    return pl.pallas_call(
        functools.partial(_swa_ffn_kernel, tiles_per_seq),
        grid=(n_tiles + 1,),
        in_specs=([pl.BlockSpec(memory_space=pltpu.SMEM),
                   mix_spec(SQ), mix_spec(SKV), mix_spec(SKV), prev_spec(SKV), prev_spec(SKV), full(km), full(vm),
                   ffn_spec(D_MODEL), ffn_spec(GV)] + [full(a) for a in weights]),
        out_specs=ffn_spec(D_MODEL),
        out_shape=jax.ShapeDtypeStruct((n, D_MODEL), F32),
        scratch_shapes=[pltpu.VMEM((2, rows, SQ), BF16)],
        compiler_params=pltpu.CompilerParams(dimension_semantics=("arbitrary",),
                                             vmem_limit_bytes=VMEM_LIMIT),
        name="swa_ffn",
    )(sinks, sq, sk, sv, sk, sv, km, vm, x2, og, *weights)


def _rope_tables(first_pos, n_pos):
    half = ROPE_DIM // 2
    inv_freq = 1.0 / (ROPE_THETA ** (jnp.arange(0, ROPE_DIM, 2, dtype=F32) / ROPE_DIM))
    ang = jnp.arange(first_pos, first_pos + n_pos, dtype=jnp.int32).astype(F32)[:, None] * inv_freq[None, :]
    cos, sin = jnp.cos(ang), jnp.sin(ang)
    pad = jnp.zeros((n_pos, SWA_HD - ROPE_DIM), F32)
    zero = jnp.zeros((n_pos, half), F32)
    cos_t = jnp.concatenate([cos, cos, pad + 1.0], axis=1)
    sup_t = jnp.concatenate([-sin, zero, pad], axis=1)
    sdn_t = jnp.concatenate([zero, sin, pad], axis=1)
    rep = LANES // SWA_HD
    return tuple(jnp.tile(a, (1, rep)) for a in (cos_t, sup_t, sdn_t))


def _swa_head_order():
    order = []
    for t in range(SWA_HEADS // 2):
        order += [t, t + SWA_HEADS // 2]
    return order


def kernel(x, meta_tokens, norm_mix_w, w_in, w_gate_up, b_gate, gla_norm_w, sinks, w_out, norm_ff_w,
           w_ff1, w_ff2, final_norm_w):
    b, s, d = x.shape
    assert d == D_MODEL and s % ROWS_IN == 0 and s % ROWS_OUT == 0 and w_in.shape[0] == 1
    assert s % ROWS_GLA == 0 and ROWS_GLA % (GLA_CHUNK * GLA_UNROLL) == 0
    assert ROWS_OUT % (SWA_BLOCK * (D_FF // FF_CHUNK)) == 0

    w = w_in[0]
    o_lr = 2 * GQ + 2 * GV
    o_sq = o_lr + GLA_RANK
    o_sk = o_sq + SQ
    order = _swa_head_order()
    w_all = jnp.concatenate(
        [w[:, :o_lr]] + [w[:, o_sq + hh * SWA_HD:o_sq + (hh + 1) * SWA_HD] for hh in order]
        + [w[:, o_sk:o_sk + 2 * SKV], w[:, o_lr:o_sq], jnp.zeros((D_MODEL, LANES - GLA_RANK), F32)],
        axis=1).astype(BF16)
    wg = jnp.pad(w_gate_up[0], ((0, LANES - GLA_RANK), (0, 0))).astype(BF16)
    bg = b_gate[0].reshape(1, GQ)
    nw = norm_mix_w[0].reshape(1, D_MODEL)
    cq = jnp.full((1, LANES), SWA_HD ** -0.5 * LOG2_E, F32)
    wts = (nw, w_all, wg, bg, cq)

    wo = w_out[0]
    wo = jnp.concatenate([wo[:GV]] + [wo[GV + hh * SWA_HD:GV + (hh + 1) * SWA_HD] for hh in order],
                         axis=0).astype(BF16)
    w1 = w_ff1[0].astype(BF16)
    w2 = w_ff2[0].astype(BF16)

    meta_tables = _rope_tables(0, N_META)
    tok_tables = _rope_tables(N_META, s)

    m_out = _in_projection(meta_tokens.astype(F32), meta_tables, wts, N_META)
    _, m_gk, m_gv, _, m_g, _, m_sk, m_sv = m_out
    front = ((GLA_CHUNK - N_META, 0), (0, 0))
    km, vm, gm = jnp.pad(m_gk, front), jnp.pad(m_gv, front), jnp.pad(m_g, front)

    x2 = x.reshape(b * s, d)
    gq, gk, gv, gr, g, sq, sk, sv = _in_projection(x2, tok_tables, wts, ROWS_IN)

    def seq(a):
        return a.reshape(b, s, a.shape[-1])

    o_gla = _gla_mixer(seq(gq), seq(gk), seq(gv), seq(gr), seq(g), km, vm, gm,
                       gla_norm_w[0].reshape(1, GLA_DV), ROWS_GLA)
    y = _swa_ffn(sinks[0].astype(F32), sq, sk, sv, m_sk, m_sv, x2, o_gla.reshape(b * s, GV), wo,
                 norm_ff_w[0].reshape(1, D_MODEL), w1, w2, final_norm_w.reshape(1, D_MODEL),
                 ROWS_OUT, s // ROWS_OUT)
    return y.reshape(b, s, d)
```

```python
import functools

import jax
import jax.numpy as jnp
from jax import lax
from jax.experimental import pallas as pl
from jax.experimental.pallas import tpu as pltpu

F32 = jnp.float32
BF16 = jnp.bfloat16

D_MODEL = 1024
N_META = 16
GLA_HEADS = 4
GLA_DK = 64
GLA_DV = 128
GLA_RANK = 16
GLA_TAU = 16.0
GLA_CHUNK = 64
GLA_LEVELS = 6
SWA_HEADS = 8
SWA_KV_HEADS = 2
SWA_HD = 64
SWA_BLOCK = 128
SWA_HALF = SWA_BLOCK // 2
ROPE_DIM = 16
ROPE_THETA = 500000.0
D_FF = 4096
EPS = 1e-5
LOG2_E = 1.4426950408889634

LANES = 128
GQ = GLA_HEADS * GLA_DK
GV = GLA_HEADS * GLA_DV
SQ = SWA_HEADS * SWA_HD
SKV = SWA_KV_HEADS * SWA_HD
FF_CHUNK = 1024

ROWS_IN = 1024
ROWS_IN_SUB = 256
W_SQ = 2 * GQ + 2 * GV
W_SK = W_SQ + SQ
W_LR = W_SK + 2 * SKV
ROWS_GLA = 512
GLA_UNROLL = 4
ROWS_OUT = 512
VMEM_LIMIT = 56 * 1024 * 1024


def _nt_dot(a, b):
    return lax.dot_general(a, b, (((1,), (1,)), ((), ())), preferred_element_type=F32)


def _tn_dot(a, b):
    return lax.dot_general(a, b, (((0,), (0,)), ((), ())), preferred_element_type=F32)


def _rms(x, w):
    return x * lax.rsqrt(jnp.mean(x * x, axis=-1, keepdims=True) + EPS) * w


def _inproj_kernel(x_ref, nw_ref, w_ref, wg_ref, bg_ref, cq_ref, cos_ref, sup_ref, sdn_ref,
                   gq_ref, gk_ref, gv_ref, gr_ref, g_ref, sq_ref, sk_ref, sv_ref):
    half = ROPE_DIM // 2
    n_sub = x_ref.shape[0] // ROWS_IN_SUB if x_ref.shape[0] > ROWS_IN_SUB else 1
    sub = x_ref.shape[0] // n_sub
    for i in range(n_sub):
        rows = slice(i * sub, (i + 1) * sub)
        u = _rms(x_ref[rows, :], nw_ref[...]).astype(BF16)

        def proj(lo, width):
            return jnp.dot(u, w_ref[:, lo:lo + width], preferred_element_type=F32)

        qk = proj(0, 2 * GQ)
        gq_ref[rows, :] = (qk[:, :GQ] * (GLA_DK ** -0.5)).astype(BF16)
        gk_ref[rows, :] = qk[:, GQ:].astype(BF16)
        gv_ref[rows, :] = proj(2 * GQ, GV).astype(BF16)
        gr_ref[rows, :] = proj(2 * GQ + GV, GV).astype(BF16)

        lr = proj(W_LR, LANES).astype(BF16)
        z = jnp.dot(lr, wg_ref[...], preferred_element_type=F32) + bg_ref[...]
        g_ref[rows, :] = (jnp.minimum(z, 0.0) - jnp.log1p(jnp.exp(-jnp.abs(z)))) * (LOG2_E / GLA_TAU)

        cos, s_up, s_dn = cos_ref[rows, :], sup_ref[rows, :], sdn_ref[rows, :]

        def rope(t):
            return t * cos + pltpu.roll(t, LANES - half, axis=1) * s_up + pltpu.roll(t, half, axis=1) * s_dn

        sq = proj(W_SQ, SQ)
        for s in range(SQ // LANES):
            sl = slice(s * LANES, (s + 1) * LANES)
            sq_ref[rows, sl] = (rope(sq[:, sl]) * cq_ref[...]).astype(BF16)
        kv = proj(W_SK, 2 * SKV)
        sk_ref[rows, :] = rope(kv[:, :SKV]).astype(BF16)
        sv_ref[rows, :] = kv[:, SKV:].astype(BF16)


def _in_projection(x2, pos_tables, wts, rows):
    n = x2.shape[0]
    grid = (n // rows,)
    nblk_pos = pos_tables[0].shape[0] // rows

    def row_spec(width):
        return pl.BlockSpec((rows, width), lambda i: (i, 0))

    def pos_spec():
        return pl.BlockSpec((rows, LANES), lambda i: (i % nblk_pos, 0))

    def full(a):
        return pl.BlockSpec(a.shape, lambda i: (0,) * a.ndim)

    cos_t, sup_t, sdn_t = pos_tables
    in_specs = ([row_spec(D_MODEL)] + [full(a) for a in wts]
                + [pos_spec(), pos_spec(), pos_spec()])
    widths = (GQ, GQ, GV, GV, GQ, SQ, SKV, SKV)
    dtypes = (BF16, BF16, BF16, BF16, F32, BF16, BF16, BF16)
    return pl.pallas_call(
        _inproj_kernel,
        grid=grid,
        in_specs=in_specs,
        out_specs=[row_spec(w) for w in widths],
        out_shape=[jax.ShapeDtypeStruct((n, w), d) for w, d in zip(widths, dtypes)],
        compiler_params=pltpu.CompilerParams(dimension_semantics=("arbitrary",),
                                             vmem_limit_bytes=VMEM_LIMIT),
        name="in_projection",
    )(x2, *wts, cos_t, sup_t, sdn_t)


def _block_cumsums(g):
    rows = g.shape[0]
    row = lax.broadcasted_iota(jnp.int32, g.shape, 0)
    w, t = [g], [g]
    for k in range(GLA_LEVELS):
        s = 1 << k
        upper = (row & s) != 0
        below = pltpu.roll(t[k], s, axis=0)
        above = pltpu.roll(t[k], rows - s, axis=0)
        w.append(w[k] + jnp.where(upper, below, 0.0))
        t.append(t[k] + jnp.where(upper, below, above))
    return w, t


def _gla_levels():
    ri = lax.broadcasted_iota(jnp.int32, (GLA_CHUNK, LANES), 0)
    ci = lax.broadcasted_iota(jnp.int32, (GLA_CHUNK, LANES), 1) & (GLA_CHUNK - 1)
    x = ri ^ ci
    level = jnp.zeros_like(x)
    for b in range(GLA_LEVELS):
        level = level + (x >= (1 << b)).astype(jnp.int32)
    return jnp.where(ci > ri, -1, level)


def _gla_chunk(q, k, v, g, st, level):
    w, t = _block_cumsums(g)
    n_slab = GQ // LANES

    def block_diag_t(m):
        r = m.shape[0]
        m2 = jnp.concatenate([m, m], axis=0).T
        same_head = ((lax.broadcasted_iota(jnp.int32, m2.shape, 0) < GLA_DK)
                     == (lax.broadcasted_iota(jnp.int32, m2.shape, 1) < r))
        return jnp.where(same_head, m2, jnp.zeros_like(m2))

    o = None
    if level is not None:
        a = [jnp.zeros((GLA_CHUNK, LANES), F32) for _ in range(n_slab)]
        for lv in range(GLA_LEVELS + 1):
            if lv == 0:
                qt, kt = q.astype(BF16), k.astype(BF16)
            else:
                qt = (q * jnp.exp2(w[lv - 1])).astype(BF16)
                kt = (k * jnp.exp2(t[lv - 1] - w[lv - 1])).astype(BF16)
            for s in range(n_slab):
                sl = slice(s * LANES, (s + 1) * LANES)
                scores = jnp.dot(qt[:, sl], block_diag_t(kt[:, sl]), preferred_element_type=F32)
                a[s] = jnp.where(level == lv, scores, a[s])

        qb = (q * jnp.exp2(w[GLA_LEVELS])).astype(BF16)
        st_b = st.astype(BF16)
        outs = []
        for s in range(n_slab):
            sl = slice(s * LANES, (s + 1) * LANES)
            v_ab = v[:, (2 * s) * GLA_DV:(2 * s + 2) * GLA_DV]
            first = lax.broadcasted_iota(jnp.int32, v_ab.shape, 1) < GLA_DV
            zero = jnp.zeros_like(v_ab)
            v_bd = jnp.concatenate([jnp.where(first, v_ab, zero), jnp.where(first, zero, v_ab)], axis=0)
            outs.append(jnp.dot(a[s].astype(BF16), v_bd, preferred_element_type=F32)
                        + jnp.dot(qb[:, sl], block_diag_t(st_b[:, sl]), preferred_element_type=F32))
        o = jnp.concatenate(outs, axis=1)

    tot = t[GLA_LEVELS]
    ku = (k * jnp.exp2(tot - w[GLA_LEVELS])).astype(BF16)
    decayed = st * jnp.exp2(tot[0:1, :])
    lo = lax.broadcasted_iota(jnp.int32, (GLA_DV, LANES), 1) < GLA_DK
    new_st = []
    for s in range(n_slab):
        sl = slice(s * LANES, (s + 1) * LANES)
        upd = _tn_dot(v[:, (2 * s) * GLA_DV:(2 * s + 2) * GLA_DV], ku[:, sl])
        new_st.append(decayed[:, sl] + jnp.where(lo, upd[:GLA_DV], upd[GLA_DV:]))
    return o, jnp.concatenate(new_st, axis=1)


def _gla_gate(o, r, nw):
    out = []
    for h in range(GLA_HEADS):
        sl = slice(h * GLA_DV, (h + 1) * GLA_DV)
        rh = r[:, sl].astype(F32)
        out.append((_rms(o[:, sl], nw) * (rh * jax.nn.sigmoid(rh))).astype(BF16))
    return jnp.concatenate(out, axis=1)


def _gla_kernel(q_ref, k_ref, v_ref, r_ref, g_ref, km_ref, vm_ref, gm_ref, nw_ref, o_ref, st_ref):
    @pl.when(pl.program_id(1) == 0)
    def _():
        st_ref[...] = jnp.zeros(st_ref.shape, F32)
        _, st0 = _gla_chunk(None, km_ref[...].astype(F32), vm_ref[...], gm_ref[...], st_ref[...], None)
        st_ref[...] = st0

    nw = nw_ref[...]
    level = _gla_levels()

    def body(c, carry):
        st = st_ref[...]
        for u in range(GLA_UNROLL):
            rows = pl.ds(pl.multiple_of((c * GLA_UNROLL + u) * GLA_CHUNK, GLA_CHUNK), GLA_CHUNK)
            o, st = _gla_chunk(q_ref[0, rows, :].astype(F32), k_ref[0, rows, :].astype(F32),
                               v_ref[0, rows, :], g_ref[0, rows, :], st, level)
            o_ref[0, rows, :] = _gla_gate(o, r_ref[0, rows, :], nw)
        st_ref[...] = st
        return carry

    lax.fori_loop(0, q_ref.shape[1] // (GLA_CHUNK * GLA_UNROLL), body, 0)


def _gla_mixer(gq, gk, gv, gr, g, km, vm, gm, gla_norm_w, rows):
    b, s, _ = gq.shape

    def seq_spec(width):
        return pl.BlockSpec((1, rows, width), lambda i, j: (i, j, 0))

    def full(a):
        return pl.BlockSpec(a.shape, lambda i, j: (0,) * a.ndim)

    return pl.pallas_call(
        _gla_kernel,
        grid=(b, s // rows),
        in_specs=[seq_spec(GQ), seq_spec(GQ), seq_spec(GV), seq_spec(GV), seq_spec(GQ),
                  full(km), full(vm), full(gm), full(gla_norm_w)],
        out_specs=seq_spec(GV),
        out_shape=jax.ShapeDtypeStruct((b, s, GV), BF16),
        scratch_shapes=[pltpu.VMEM((GLA_DV, GQ), F32)],
        compiler_params=pltpu.CompilerParams(dimension_semantics=("arbitrary", "arbitrary"),
                                             vmem_limit_bytes=VMEM_LIMIT),
        name="gla_mixer",
    )(gq, gk, gv, gr, g, km, vm, gm, gla_norm_w)


def _swa_block(sink_ref, q, kc, kp, vc, vp, km, vm, has_prev):
    n_win = SWA_BLOCK + SWA_HALF
    n_keys = n_win + N_META
    r = lax.broadcasted_iota(jnp.int32, (SWA_HALF, n_keys), 0)
    c = lax.broadcasted_iota(jnp.int32, (SWA_HALF, n_keys), 1)
    in_band = (c > r) & (c <= r + SWA_BLOCK)
    lane = lax.broadcasted_iota(jnp.int32, (SWA_HALF, LANES), 1)
    low = lane < SWA_HD

    halves = []
    for half in range(2):
        rows = slice(half * SWA_HALF, (half + 1) * SWA_HALF)
        if half == 0:
            k_all = jnp.concatenate([kp, kc[:SWA_HALF], km], axis=0)
            v_all = jnp.concatenate([vp, vc[:SWA_HALF], vm], axis=0)
            n_prev = SWA_BLOCK
        else:
            k_all = jnp.concatenate([kp[SWA_HALF:], kc, km], axis=0)
            v_all = jnp.concatenate([vp[SWA_HALF:], vc, vm], axis=0)
            n_prev = SWA_HALF
        valid = (c >= n_win) | in_band
        if has_prev is not None:
            valid = (c >= n_win) | (in_band & (has_prev | (c >= n_prev)))
        bias = jnp.where(valid, 0.0, -jnp.inf)
        pieces = []
        for t in range(SWA_HEADS // 2):
            qt = q[rows, t * LANES:(t + 1) * LANES]
            pieces += [jnp.where(low, qt, jnp.zeros_like(qt)), jnp.where(low, jnp.zeros_like(qt), qt)]
        q_all = jnp.concatenate(pieces, axis=0)
        s_all = _nt_dot(q_all, k_all)
        p, denom = [], []
        for i in range(SWA_HEADS):
            s = s_all[i * SWA_HALF:(i + 1) * SWA_HALF] + bias
            sink = sink_ref[(i // 2) + (i % 2) * (SWA_HEADS // 2)] * LOG2_E
            m = jnp.maximum(jnp.max(s, axis=-1, keepdims=True), sink)
            p.append(jnp.exp2(s - m))
            denom.append(jnp.sum(p[i], axis=-1, keepdims=True) + jnp.exp2(sink - m))
        o = jnp.dot(jnp.concatenate(p, axis=0).astype(BF16), v_all, preferred_element_type=F32)
        slabs = []
        for t in range(SWA_HEADS // 2):
            lo = o[(2 * t) * SWA_HALF:(2 * t + 1) * SWA_HALF] / denom[2 * t]
            hi = o[(2 * t + 1) * SWA_HALF:(2 * t + 2) * SWA_HALF] / denom[2 * t + 1]
            slabs.append(jnp.where(low, lo, hi).astype(BF16))
        halves.append(jnp.concatenate(slabs, axis=1))
    return jnp.concatenate(halves, axis=0)


def _swa_ffn_kernel(tiles_per_seq, sink_ref, sq0_ref, sk0_ref, sv0_ref, sq_ref, sk_ref, sv_ref, skp_ref, svp_ref,
                    km_ref, vm_ref, x_ref, og_ref, wo_ref, nf_ref, w1_ref, w2_ref, nl_ref, y_ref, os_ref):
    s = pl.program_id(0)
    ahead = jnp.minimum(s + 1, pl.num_programs(0) - 1)
    seq_start = lax.rem(ahead, tiles_per_seq) == 0
    cur = lax.rem(s, 2)

    def swa_tile_block(j, slot, q_ref, k_ref, v_ref, first_prev):
        rows = slice(j * SWA_BLOCK, (j + 1) * SWA_BLOCK)
        if j == 0:
            kp, vp, has_prev = first_prev
        else:
            before = slice((j - 1) * SWA_BLOCK, j * SWA_BLOCK)
            kp, vp, has_prev = k_ref[before, :], v_ref[before, :], None
        os_ref[slot, rows, :] = _swa_block(sink_ref, q_ref[rows, :], k_ref[rows, :], kp, v_ref[rows, :], vp,
                                           km_ref[...], vm_ref[...], has_prev)

    n_blocks = x_ref.shape[0] // SWA_BLOCK

    @pl.when(s == 0)
    def _():
        for j in range(n_blocks):
            swa_tile_block(j, 0, sq0_ref, sk0_ref, sv0_ref, (sk0_ref[0:SWA_BLOCK, :], sv0_ref[0:SWA_BLOCK, :], s > 0))

    n_ff = D_FF // FF_CHUNK
    swa_per_ff = n_blocks // n_ff

    h = (x_ref[...]
         + jnp.dot(og_ref[...], wo_ref[0:GV, :], preferred_element_type=F32)
         + jnp.dot(os_ref[cur], wo_ref[GV:GV + SQ, :], preferred_element_type=F32))
    f = _rms(h, nf_ref[...]).astype(BF16)
    ff = None
    for c in range(n_ff):
        for j in range(c * swa_per_ff, (c + 1) * swa_per_ff):
            swa_tile_block(j, 1 - cur, sq_ref, sk_ref, sv_ref,
                           (skp_ref[...], svp_ref[...], jnp.logical_not(seq_start)))
        sl = slice(c * FF_CHUNK, (c + 1) * FF_CHUNK)
        a = jnp.maximum(jnp.dot(f, w1_ref[:, sl], preferred_element_type=F32), 0.0)
        d = jnp.dot((a * a).astype(BF16), w2_ref[sl, :], preferred_element_type=F32)
        ff = d if ff is None else ff + d
    y_ref[...] = _rms(h + ff, nl_ref[...])


def _swa_ffn(sinks, sq, sk, sv, km, vm, x2, og, wo, nf, w1, w2, nl, rows, tiles_per_seq):
    n = x2.shape[0]
    n_tiles = n // rows
    prev_per_tile = rows // SWA_BLOCK

    def first_spec(width):
        return pl.BlockSpec((rows, width), lambda i: (0, 0))

    def ahead_spec(width):
        return pl.BlockSpec((rows, width), lambda i: (jnp.minimum(i + 1, n_tiles - 1), 0))

    def prev_spec(width):
        return pl.BlockSpec((SWA_BLOCK, width),
                            lambda i: (jnp.minimum(i + 1, n_tiles - 1) * prev_per_tile - 1, 0))

    def ffn_spec(width):
        return pl.BlockSpec((rows, width), lambda i: (i, 0))

    def full(a):
        return pl.BlockSpec(a.shape, lambda i: (0,) * a.ndim, pipeline_mode=pl.Buffered(1))

    weights = (wo, nf, w1, w2, nl)
    return pl.pallas_call(
        functools.partial(_swa_ffn_kernel, tiles_per_seq),
        grid=(n_tiles,),
        in_specs=([pl.BlockSpec(memory_space=pltpu.SMEM), first_spec(SQ), first_spec(SKV), first_spec(SKV),
                   ahead_spec(SQ), ahead_spec(SKV), ahead_spec(SKV), prev_spec(SKV), prev_spec(SKV),
                   full(km), full(vm), ffn_spec(D_MODEL), ffn_spec(GV)] + [full(a) for a in weights]),
        out_specs=ffn_spec(D_MODEL),
        out_shape=jax.ShapeDtypeStruct((n, D_MODEL), F32),
        scratch_shapes=[pltpu.VMEM((2, rows, SQ), BF16)],
        compiler_params=pltpu.CompilerParams(dimension_semantics=("arbitrary",),
                                             vmem_limit_bytes=VMEM_LIMIT),
        name="swa_ffn",
    )(sinks, sq, sk, sv, sq, sk, sv, sk, sv, km, vm, x2, og, *weights)


def _rope_tables(first_pos, n_pos):
    half = ROPE_DIM // 2
    inv_freq = 1.0 / (ROPE_THETA ** (jnp.arange(0, ROPE_DIM, 2, dtype=F32) / ROPE_DIM))
    ang = jnp.arange(first_pos, first_pos + n_pos, dtype=jnp.int32).astype(F32)[:, None] * inv_freq[None, :]
    cos, sin = jnp.cos(ang), jnp.sin(ang)
    pad = jnp.zeros((n_pos, SWA_HD - ROPE_DIM), F32)
    zero = jnp.zeros((n_pos, half), F32)
    cos_t = jnp.concatenate([cos, cos, pad + 1.0], axis=1)
    sup_t = jnp.concatenate([-sin, zero, pad], axis=1)
    sdn_t = jnp.concatenate([zero, sin, pad], axis=1)
    rep = LANES // SWA_HD
    return tuple(jnp.tile(a, (1, rep)) for a in (cos_t, sup_t, sdn_t))


def _swa_head_order():
    order = []
    for t in range(SWA_HEADS // 2):
        order += [t, t + SWA_HEADS // 2]
    return order


def kernel(x, meta_tokens, norm_mix_w, w_in, w_gate_up, b_gate, gla_norm_w, sinks, w_out, norm_ff_w,
           w_ff1, w_ff2, final_norm_w):
    b, s, d = x.shape
    assert d == D_MODEL and s % ROWS_IN == 0 and s % ROWS_OUT == 0 and w_in.shape[0] == 1
    assert s % ROWS_GLA == 0 and ROWS_GLA % (GLA_CHUNK * GLA_UNROLL) == 0
    assert ROWS_OUT % (SWA_BLOCK * (D_FF // FF_CHUNK)) == 0

    w = w_in[0]
    o_lr = 2 * GQ + 2 * GV
    o_sq = o_lr + GLA_RANK
    o_sk = o_sq + SQ
    order = _swa_head_order()
    w_all = jnp.concatenate(
        [w[:, :o_lr]] + [w[:, o_sq + hh * SWA_HD:o_sq + (hh + 1) * SWA_HD] for hh in order]
        + [w[:, o_sk:o_sk + 2 * SKV], w[:, o_lr:o_sq], jnp.zeros((D_MODEL, LANES - GLA_RANK), F32)],
        axis=1).astype(BF16)
    wg = jnp.pad(w_gate_up[0], ((0, LANES - GLA_RANK), (0, 0))).astype(BF16)
    bg = b_gate[0].reshape(1, GQ)
    nw = norm_mix_w[0].reshape(1, D_MODEL)
    cq = jnp.full((1, LANES), SWA_HD ** -0.5 * LOG2_E, F32)
    wts = (nw, w_all, wg, bg, cq)

    wo = w_out[0]
    wo = jnp.concatenate([wo[:GV]] + [wo[GV + hh * SWA_HD:GV + (hh + 1) * SWA_HD] for hh in order],
                         axis=0).astype(BF16)
    w1 = w_ff1[0].astype(BF16)
    w2 = w_ff2[0].astype(BF16)

    meta_tables = _rope_tables(0, N_META)
    tok_tables = _rope_tables(N_META, s)

    m_out = _in_projection(meta_tokens.astype(F32), meta_tables, wts, N_META)
    _, m_gk, m_gv, _, m_g, _, m_sk, m_sv = m_out
    front = ((GLA_CHUNK - N_META, 0), (0, 0))
    km, vm, gm = jnp.pad(m_gk, front), jnp.pad(m_gv, front), jnp.pad(m_g, front)

    x2 = x.reshape(b * s, d)
    gq, gk, gv, gr, g, sq, sk, sv = _in_projection(x2, tok_tables, wts, ROWS_IN)

    def seq(a):
        return a.reshape(b, s, a.shape[-1])

    o_gla = _gla_mixer(seq(gq), seq(gk), seq(gv), seq(gr), seq(g), km, vm, gm,
                       gla_norm_w[0].reshape(1, GLA_DV), ROWS_GLA)
    y = _swa_ffn(sinks[0].astype(F32), sq, sk, sv, m_sk, m_sv, x2, o_gla.reshape(b * s, GV), wo,
                 norm_ff_w[0].reshape(1, D_MODEL), w1, w2, final_norm_w.reshape(1, D_MODEL),
                 ROWS_OUT, s // ROWS_OUT)
    return y.reshape(b, s, d)
```

```python
import functools

import jax
import jax.numpy as jnp
from jax import lax
from jax.experimental import pallas as pl
from jax.experimental.pallas import tpu as pltpu

F32 = jnp.float32
BF16 = jnp.bfloat16

D_MODEL = 1024
N_META = 16
GLA_HEADS = 4
GLA_DK = 64
GLA_DV = 128
GLA_RANK = 16
GLA_TAU = 16.0
GLA_CHUNK = 64
GLA_LEVELS = 6
SWA_HEADS = 8
SWA_KV_HEADS = 2
SWA_HD = 64
SWA_BLOCK = 128
SWA_HALF = SWA_BLOCK // 2
ROPE_DIM = 16
ROPE_THETA = 500000.0
D_FF = 4096
EPS = 1e-5
LOG2_E = 1.4426950408889634

LANES = 128
GQ = GLA_HEADS * GLA_DK
GV = GLA_HEADS * GLA_DV
SQ = SWA_HEADS * SWA_HD
SKV = SWA_KV_HEADS * SWA_HD
FF_CHUNK = 1024

ROWS_IN = 1024
ROWS_IN_SUB = 256
W_SQ = 2 * GQ + 2 * GV
W_SK = W_SQ + SQ
W_LR = W_SK + 2 * SKV
ROWS_GLA = 512
GLA_UNROLL = 4
ROWS_OUT = 512
VMEM_LIMIT = 56 * 1024 * 1024


def _nt_dot(a, b):
    return lax.dot_general(a, b, (((1,), (1,)), ((), ())), preferred_element_type=F32)


def _tn_dot(a, b):
    return lax.dot_general(a, b, (((0,), (0,)), ((), ())), preferred_element_type=F32)


def _rms(x, w):
    return x * lax.rsqrt(jnp.mean(x * x, axis=-1, keepdims=True) + EPS) * w


def _project(x_ref, nw_ref, w_ref, wg_ref, bg_ref, cq_ref, cos_ref, sup_ref, sdn_ref,
             gq_ref, gk_ref, gv_ref, gr_ref, g_ref, sq_ref, sk_ref, sv_ref):
    half = ROPE_DIM // 2
    n_sub = x_ref.shape[0] // ROWS_IN_SUB if x_ref.shape[0] > ROWS_IN_SUB else 1
    sub = x_ref.shape[0] // n_sub
    for i in range(n_sub):
        rows = slice(i * sub, (i + 1) * sub)
        u = _rms(x_ref[rows, :], nw_ref[...]).astype(BF16)

        def proj(lo, width):
            return jnp.dot(u, w_ref[:, lo:lo + width], preferred_element_type=F32)

        qk = proj(0, 2 * GQ)
        gq_ref[rows, :] = (qk[:, :GQ] * (GLA_DK ** -0.5)).astype(BF16)
        gk_ref[rows, :] = qk[:, GQ:].astype(BF16)
        gv_ref[rows, :] = proj(2 * GQ, GV).astype(BF16)
        gr_ref[rows, :] = proj(2 * GQ + GV, GV).astype(BF16)

        lr = proj(W_LR, LANES).astype(BF16)
        z = jnp.dot(lr, wg_ref[...], preferred_element_type=F32) + bg_ref[...]
        g_ref[rows, :] = (jnp.minimum(z, 0.0) - jnp.log1p(jnp.exp(-jnp.abs(z)))) * (LOG2_E / GLA_TAU)

        cos, s_up, s_dn = cos_ref[rows, :], sup_ref[rows, :], sdn_ref[rows, :]

        def rope(t):
            return t * cos + pltpu.roll(t, LANES - half, axis=1) * s_up + pltpu.roll(t, half, axis=1) * s_dn

        sq = proj(W_SQ, SQ)
        for s in range(SQ // LANES):
            sl = slice(s * LANES, (s + 1) * LANES)
            sq_ref[rows, sl] = (rope(sq[:, sl]) * cq_ref[...]).astype(BF16)
        kv = proj(W_SK, 2 * SKV)
        sk_ref[rows, :] = rope(kv[:, :SKV]).astype(BF16)
        sv_ref[rows, :] = kv[:, SKV:].astype(BF16)


PROJ_WIDTHS = (GQ, GQ, GV, GV, GQ, SQ, SKV, SKV)
PROJ_DTYPES = (BF16, BF16, BF16, BF16, F32, BF16, BF16, BF16)


def _meta_prep_kernel(x_ref, win_ref, nw_ref, wg_ref, bg_ref, cq_ref, cos_ref, sup_ref, sdn_ref, w_ref, *out_refs):
    o_lr = 2 * GQ + 2 * GV
    o_sq = o_lr + GLA_RANK
    o_sk = o_sq + SQ
    w_ref[:, 0:o_lr] = win_ref[:, 0:o_lr].astype(BF16)
    for i, hh in enumerate(_swa_head_order()):
        w_ref[:, W_SQ + i * SWA_HD:W_SQ + (i + 1) * SWA_HD] = (
            win_ref[:, o_sq + hh * SWA_HD:o_sq + (hh + 1) * SWA_HD].astype(BF16))
    w_ref[:, W_SK:W_LR] = win_ref[:, o_sk:o_sk + 2 * SKV].astype(BF16)
    w_ref[:, W_LR:W_LR + GLA_RANK] = win_ref[:, o_lr:o_sq].astype(BF16)
    w_ref[:, W_LR + GLA_RANK:] = jnp.zeros((D_MODEL, LANES - GLA_RANK), BF16)
    _project(x_ref, nw_ref, w_ref, wg_ref, bg_ref, cq_ref, cos_ref, sup_ref, sdn_ref, *out_refs)


def _meta_prep(meta_tokens, w_in, small, pos_tables):
    n = meta_tokens.shape[0]
    return pl.pallas_call(
        _meta_prep_kernel,
        out_shape=([jax.ShapeDtypeStruct((D_MODEL, W_LR + LANES), BF16)]
                   + [jax.ShapeDtypeStruct((n, w), d) for w, d in zip(PROJ_WIDTHS, PROJ_DTYPES)]),
        compiler_params=pltpu.CompilerParams(vmem_limit_bytes=VMEM_LIMIT),
        name="meta_prep",
    )(meta_tokens, w_in, *small, *pos_tables)


def _inproj_kernel(x_ref, nw_ref, w_ref, wg_ref, bg_ref, cq_ref, cos_ref, sup_ref, sdn_ref,
                   w1_ref, w2_ref, wo_ref, *out_refs):
    *proj_refs, w1b_ref, w2b_ref, wob_ref = out_refs
    _project(x_ref, nw_ref, w_ref, wg_ref, bg_ref, cq_ref, cos_ref, sup_ref, sdn_ref, *proj_refs)
    w1b_ref[...] = w1_ref[...].astype(BF16)
    w2b_ref[...] = w2_ref[...].astype(BF16)
    wob_ref[...] = wo_ref[...].astype(BF16)


def _in_projection(x2, pos_tables, w_all, small, w1, w2, wo, rows):
    n = x2.shape[0]
    steps = n // rows
    nblk_pos = pos_tables[0].shape[0] // rows
    assert w1.shape[0] % steps == 0 and w2.shape[0] % steps == 0 and wo.shape[0] == steps * SWA_HD

    def row_spec(width):
        return pl.BlockSpec((rows, width), lambda i: (i, 0))

    def pos_spec():
        return pl.BlockSpec((rows, LANES), lambda i: (i % nblk_pos, 0))

    def full(a):
        return pl.BlockSpec(a.shape, lambda i: (0,) * a.ndim)

    def slice_spec(a):
        return pl.BlockSpec((a.shape[0] // steps, a.shape[1]), lambda i: (i, 0))

    n_gla = GV // SWA_HD
    half = SWA_HEADS // 2

    def wo_src(i):
        j = i - n_gla
        return (jnp.where(j < 0, i, n_gla + j // 2 + half * (j % 2)), 0)

    in_specs = ([row_spec(D_MODEL), full(small[0]), full(w_all)] + [full(a) for a in small[1:]]
                + [pos_spec(), pos_spec(), pos_spec(), slice_spec(w1), slice_spec(w2),
                   pl.BlockSpec((SWA_HD, wo.shape[1]), wo_src)])
    weights_out = [jax.ShapeDtypeStruct(a.shape, BF16) for a in (w1, w2, wo)]
    return pl.pallas_call(
        _inproj_kernel,
        grid=(steps,),
        in_specs=in_specs,
        out_specs=[row_spec(w) for w in PROJ_WIDTHS] + [slice_spec(w1), slice_spec(w2), slice_spec(wo)],
        out_shape=[jax.ShapeDtypeStruct((n, w), d) for w, d in zip(PROJ_WIDTHS, PROJ_DTYPES)] + weights_out,
        compiler_params=pltpu.CompilerParams(dimension_semantics=("arbitrary",),
                                             vmem_limit_bytes=VMEM_LIMIT),
        name="in_projection",
    )(x2, small[0], w_all, *small[1:], *pos_tables, w1, w2, wo)


def _block_cumsums(g):
    rows = g.shape[0]
    row = lax.broadcasted_iota(jnp.int32, g.shape, 0)
    w, t = [g], [g]
    for k in range(GLA_LEVELS):
        s = 1 << k
        upper = (row & s) != 0
        below = pltpu.roll(t[k], s, axis=0)
        above = pltpu.roll(t[k], rows - s, axis=0)
        w.append(w[k] + jnp.where(upper, below, 0.0))
        t.append(t[k] + jnp.where(upper, below, above))
    return w, t


def _gla_levels():
    ri = lax.broadcasted_iota(jnp.int32, (GLA_CHUNK, LANES), 0)
    ci = lax.broadcasted_iota(jnp.int32, (GLA_CHUNK, LANES), 1) & (GLA_CHUNK - 1)
    x = ri ^ ci
    level = jnp.zeros_like(x)
    for b in range(GLA_LEVELS):
        level = level + (x >= (1 << b)).astype(jnp.int32)
    return jnp.where(ci > ri, -1, level)


def _gla_chunk(q, k, v, g, st, level):
    w, t = _block_cumsums(g)
    n_slab = GQ // LANES

    def block_diag_t(m):
        r = m.shape[0]
        m2 = jnp.concatenate([m, m], axis=0).T
        same_head = ((lax.broadcasted_iota(jnp.int32, m2.shape, 0) < GLA_DK)
                     == (lax.broadcasted_iota(jnp.int32, m2.shape, 1) < r))
        return jnp.where(same_head, m2, jnp.zeros_like(m2))

    o = None
    if level is not None:
        a = [jnp.zeros((GLA_CHUNK, LANES), F32) for _ in range(n_slab)]
        for lv in range(GLA_LEVELS + 1):
            if lv == 0:
                qt, kt = q.astype(BF16), k.astype(BF16)
            else:
                qt = (q * jnp.exp2(w[lv - 1])).astype(BF16)
                kt = (k * jnp.exp2(t[lv - 1] - w[lv - 1])).astype(BF16)
            for s in range(n_slab):
                sl = slice(s * LANES, (s + 1) * LANES)
                scores = jnp.dot(qt[:, sl], block_diag_t(kt[:, sl]), preferred_element_type=F32)
                a[s] = jnp.where(level == lv, scores, a[s])

        qb = (q * jnp.exp2(w[GLA_LEVELS])).astype(BF16)
        st_b = st.astype(BF16)
        outs = []
        for s in range(n_slab):
            sl = slice(s * LANES, (s + 1) * LANES)
            v_ab = v[:, (2 * s) * GLA_DV:(2 * s + 2) * GLA_DV]
            first = lax.broadcasted_iota(jnp.int32, v_ab.shape, 1) < GLA_DV
            zero = jnp.zeros_like(v_ab)
            v_bd = jnp.concatenate([jnp.where(first, v_ab, zero), jnp.where(first, zero, v_ab)], axis=0)
            outs.append(jnp.dot(a[s].astype(BF16), v_bd, preferred_element_type=F32)
                        + jnp.dot(qb[:, sl], block_diag_t(st_b[:, sl]), preferred_element_type=F32))
        o = jnp.concatenate(outs, axis=1)

    tot = t[GLA_LEVELS]
    ku = (k * jnp.exp2(tot - w[GLA_LEVELS])).astype(BF16)
    decayed = st * jnp.exp2(tot[0:1, :])
    lo = lax.broadcasted_iota(jnp.int32, (GLA_DV, LANES), 1) < GLA_DK
    new_st = []
    for s in range(n_slab):
        sl = slice(s * LANES, (s + 1) * LANES)
        upd = _tn_dot(v[:, (2 * s) * GLA_DV:(2 * s + 2) * GLA_DV], ku[:, sl])
        new_st.append(decayed[:, sl] + jnp.where(lo, upd[:GLA_DV], upd[GLA_DV:]))
    return o, jnp.concatenate(new_st, axis=1)


def _gla_gate(o, r, nw):
    out = []
    for h in range(GLA_HEADS):
        sl = slice(h * GLA_DV, (h + 1) * GLA_DV)
        rh = r[:, sl].astype(F32)
        out.append((_rms(o[:, sl], nw) * (rh * jax.nn.sigmoid(rh))).astype(BF16))
    return jnp.concatenate(out, axis=1)


def _gla_kernel(q_ref, k_ref, v_ref, r_ref, g_ref, km_ref, vm_ref, gm_ref, nw_ref, o_ref, st_ref):
    @pl.when(pl.program_id(1) == 0)
    def _():
        st_ref[...] = jnp.zeros(st_ref.shape, F32)
        _, st0 = _gla_chunk(None, km_ref[...].astype(F32), vm_ref[...], gm_ref[...], st_ref[...], None)
        st_ref[...] = st0

    nw = nw_ref[...]
    level = _gla_levels()

    def body(c, carry):
        st = st_ref[...]
        for u in range(GLA_UNROLL):
            rows = pl.ds(pl.multiple_of((c * GLA_UNROLL + u) * GLA_CHUNK, GLA_CHUNK), GLA_CHUNK)
            o, st = _gla_chunk(q_ref[0, rows, :].astype(F32), k_ref[0, rows, :].astype(F32),
                               v_ref[0, rows, :], g_ref[0, rows, :], st, level)
            o_ref[0, rows, :] = _gla_gate(o, r_ref[0, rows, :], nw)
        st_ref[...] = st
        return carry

    lax.fori_loop(0, q_ref.shape[1] // (GLA_CHUNK * GLA_UNROLL), body, 0)


def _gla_mixer(gq, gk, gv, gr, g, km, vm, gm, gla_norm_w, rows):
    b, s, _ = gq.shape

    def seq_spec(width):
        return pl.BlockSpec((1, rows, width), lambda i, j: (i, j, 0))

    def full(a):
        return pl.BlockSpec(a.shape, lambda i, j: (0,) * a.ndim)

    return pl.pallas_call(
        _gla_kernel,
        grid=(b, s // rows),
        in_specs=[seq_spec(GQ), seq_spec(GQ), seq_spec(GV), seq_spec(GV), seq_spec(GQ),
                  full(km), full(vm), full(gm), full(gla_norm_w)],
        out_specs=seq_spec(GV),
        out_shape=jax.ShapeDtypeStruct((b, s, GV), BF16),
        scratch_shapes=[pltpu.VMEM((GLA_DV, GQ), F32)],
        compiler_params=pltpu.CompilerParams(dimension_semantics=("arbitrary", "arbitrary"),
                                             vmem_limit_bytes=VMEM_LIMIT),
        name="gla_mixer",
    )(gq, gk, gv, gr, g, km, vm, gm, gla_norm_w)


def _swa_block(sink_ref, q, kc, kp, vc, vp, km, vm, has_prev):
    n_win = SWA_BLOCK + SWA_HALF
    n_keys = n_win + N_META
    r = lax.broadcasted_iota(jnp.int32, (SWA_HALF, n_keys), 0)
    c = lax.broadcasted_iota(jnp.int32, (SWA_HALF, n_keys), 1)
    in_band = (c > r) & (c <= r + SWA_BLOCK)
    lane = lax.broadcasted_iota(jnp.int32, (SWA_HALF, LANES), 1)
    low = lane < SWA_HD

    halves = []
    for half in range(2):
        rows = slice(half * SWA_HALF, (half + 1) * SWA_HALF)
        if half == 0:
            k_all = jnp.concatenate([kp, kc[:SWA_HALF], km], axis=0)
            v_all = jnp.concatenate([vp, vc[:SWA_HALF], vm], axis=0)
            n_prev = SWA_BLOCK
        else:
            k_all = jnp.concatenate([kp[SWA_HALF:], kc, km], axis=0)
            v_all = jnp.concatenate([vp[SWA_HALF:], vc, vm], axis=0)
            n_prev = SWA_HALF
        valid = (c >= n_win) | in_band
        if has_prev is not None:
            valid = (c >= n_win) | (in_band & (has_prev | (c >= n_prev)))
        bias = jnp.where(valid, 0.0, -jnp.inf)
        pieces = []
        for t in range(SWA_HEADS // 2):
            qt = q[rows, t * LANES:(t + 1) * LANES]
            pieces += [jnp.where(low, qt, jnp.zeros_like(qt)), jnp.where(low, jnp.zeros_like(qt), qt)]
        q_all = jnp.concatenate(pieces, axis=0)
        s_all = _nt_dot(q_all, k_all)
        p, denom = [], []
        for i in range(SWA_HEADS):
            s = s_all[i * SWA_HALF:(i + 1) * SWA_HALF] + bias
            sink = sink_ref[(i // 2) + (i % 2) * (SWA_HEADS // 2)] * LOG2_E
            m = jnp.maximum(jnp.max(s, axis=-1, keepdims=True), sink)
            p.append(jnp.exp2(s - m))
            denom.append(jnp.sum(p[i], axis=-1, keepdims=True) + jnp.exp2(sink - m))
        o = jnp.dot(jnp.concatenate(p, axis=0).astype(BF16), v_all, preferred_element_type=F32)
        slabs = []
        for t in range(SWA_HEADS // 2):
            lo = o[(2 * t) * SWA_HALF:(2 * t + 1) * SWA_HALF] / denom[2 * t]
            hi = o[(2 * t + 1) * SWA_HALF:(2 * t + 2) * SWA_HALF] / denom[2 * t + 1]
            slabs.append(jnp.where(low, lo, hi).astype(BF16))
        halves.append(jnp.concatenate(slabs, axis=1))
    return jnp.concatenate(halves, axis=0)


def _swa_ffn_kernel(tiles_per_seq, sink_ref, sq0_ref, sk0_ref, sv0_ref, sq_ref, sk_ref, sv_ref, skp_ref, svp_ref,
                    km_ref, vm_ref, x_ref, og_ref, wo_ref, nf_ref, w1_ref, w2_ref, nl_ref, y_ref, os_ref):
    s = pl.program_id(0)
    ahead = jnp.minimum(s + 1, pl.num_programs(0) - 1)
    seq_start = lax.rem(ahead, tiles_per_seq) == 0
    cur = lax.rem(s, 2)

    def swa_tile_block(j, slot, q_ref, k_ref, v_ref, first_prev):
        rows = slice(j * SWA_BLOCK, (j + 1) * SWA_BLOCK)
        if j == 0:
            kp, vp, has_prev = first_prev
        else:
            before = slice((j - 1) * SWA_BLOCK, j * SWA_BLOCK)
            kp, vp, has_prev = k_ref[before, :], v_ref[before, :], None
        os_ref[slot, rows, :] = _swa_block(sink_ref, q_ref[rows, :], k_ref[rows, :], kp, v_ref[rows, :], vp,
                                           km_ref[...], vm_ref[...], has_prev)

    n_blocks = x_ref.shape[0] // SWA_BLOCK

    @pl.when(s == 0)
    def _():
        for j in range(n_blocks):
            swa_tile_block(j, 0, sq0_ref, sk0_ref, sv0_ref, (sk0_ref[0:SWA_BLOCK, :], sv0_ref[0:SWA_BLOCK, :], s > 0))

    n_ff = D_FF // FF_CHUNK
    swa_per_ff = n_blocks // n_ff

    h = (x_ref[...]
         + jnp.dot(og_ref[...], wo_ref[0:GV, :], preferred_element_type=F32)
         + jnp.dot(os_ref[cur], wo_ref[GV:GV + SQ, :], preferred_element_type=F32))
    f = _rms(h, nf_ref[...]).astype(BF16)
    ff = None
    for c in range(n_ff):
        for j in range(c * swa_per_ff, (c + 1) * swa_per_ff):
            swa_tile_block(j, 1 - cur, sq_ref, sk_ref, sv_ref,
                           (skp_ref[...], svp_ref[...], jnp.logical_not(seq_start)))
        sl = slice(c * FF_CHUNK, (c + 1) * FF_CHUNK)
        a = jnp.maximum(jnp.dot(f, w1_ref[:, sl], preferred_element_type=F32), 0.0)
        d = jnp.dot((a * a).astype(BF16), w2_ref[sl, :], preferred_element_type=F32)
        ff = d if ff is None else ff + d
    y_ref[...] = _rms(h + ff, nl_ref[...])


def _swa_ffn(sinks, sq, sk, sv, km, vm, x2, og, wo, nf, w1, w2, nl, rows, tiles_per_seq):
    n = x2.shape[0]
    n_tiles = n // rows
    prev_per_tile = rows // SWA_BLOCK

    def first_spec(width):
        return pl.BlockSpec((rows, width), lambda i: (0, 0))

    def ahead_spec(width):
        return pl.BlockSpec((rows, width), lambda i: (jnp.minimum(i + 1, n_tiles - 1), 0))

    def prev_spec(width):
        return pl.BlockSpec((SWA_BLOCK, width),
                            lambda i: (jnp.minimum(i + 1, n_tiles - 1) * prev_per_tile - 1, 0))

    def ffn_spec(width):
        return pl.BlockSpec((rows, width), lambda i: (i, 0))

    def full(a):
        return pl.BlockSpec(a.shape, lambda i: (0,) * a.ndim, pipeline_mode=pl.Buffered(1))

    weights = (wo, nf, w1, w2, nl)
    return pl.pallas_call(
        functools.partial(_swa_ffn_kernel, tiles_per_seq),
        grid=(n_tiles,),
        in_specs=([pl.BlockSpec(memory_space=pltpu.SMEM), first_spec(SQ), first_spec(SKV), first_spec(SKV),
                   ahead_spec(SQ), ahead_spec(SKV), ahead_spec(SKV), prev_spec(SKV), prev_spec(SKV),
                   full(km), full(vm), ffn_spec(D_MODEL), ffn_spec(GV)] + [full(a) for a in weights]),
        out_specs=ffn_spec(D_MODEL),
        out_shape=jax.ShapeDtypeStruct((n, D_MODEL), F32),
        scratch_shapes=[pltpu.VMEM((2, rows, SQ), BF16)],
        compiler_params=pltpu.CompilerParams(dimension_semantics=("arbitrary",),
                                             vmem_limit_bytes=VMEM_LIMIT),
        name="swa_ffn",
    )(sinks, sq, sk, sv, sq, sk, sv, sk, sv, km, vm, x2, og, *weights)


def _rope_tables(first_pos, n_pos):
    half = ROPE_DIM // 2
    inv_freq = 1.0 / (ROPE_THETA ** (jnp.arange(0, ROPE_DIM, 2, dtype=F32) / ROPE_DIM))
    ang = jnp.arange(first_pos, first_pos + n_pos, dtype=jnp.int32).astype(F32)[:, None] * inv_freq[None, :]
    cos, sin = jnp.cos(ang), jnp.sin(ang)
    pad = jnp.zeros((n_pos, SWA_HD - ROPE_DIM), F32)
    zero = jnp.zeros((n_pos, half), F32)
    rep = LANES // SWA_HD
    cos_t = jnp.concatenate([cos, cos, pad + 1.0] * rep, axis=1)
    sup_t = jnp.concatenate([-sin, zero, pad] * rep, axis=1)
    sdn_t = jnp.concatenate([zero, sin, pad] * rep, axis=1)
    return cos_t, sup_t, sdn_t


def _swa_head_order():
    order = []
    for t in range(SWA_HEADS // 2):
        order += [t, t + SWA_HEADS // 2]
    return order


def kernel(x, meta_tokens, norm_mix_w, w_in, w_gate_up, b_gate, gla_norm_w, sinks, w_out, norm_ff_w,
           w_ff1, w_ff2, final_norm_w):
    b, s, d = x.shape
    assert d == D_MODEL and s % ROWS_IN == 0 and s % ROWS_OUT == 0 and w_in.shape[0] == 1
    assert s % ROWS_GLA == 0 and ROWS_GLA % (GLA_CHUNK * GLA_UNROLL) == 0
    assert ROWS_OUT % (SWA_BLOCK * (D_FF // FF_CHUNK)) == 0

    wg = jnp.pad(w_gate_up[0], ((0, LANES - GLA_RANK), (0, 0))).astype(BF16)
    bg = b_gate[0].reshape(1, GQ)
    nw = norm_mix_w[0].reshape(1, D_MODEL)
    cq = jnp.full((1, LANES), SWA_HD ** -0.5 * LOG2_E, F32)
    small = (nw, wg, bg, cq)

    meta_tables = _rope_tables(0, N_META)
    tok_tables = _rope_tables(N_META, s)

    w_all, _, m_gk, m_gv, _, m_g, _, m_sk, m_sv = _meta_prep(meta_tokens.astype(F32), w_in[0], small, meta_tables)
    front = ((GLA_CHUNK - N_META, 0), (0, 0))
    km, vm, gm = jnp.pad(m_gk, front), jnp.pad(m_gv, front), jnp.pad(m_g, front)

    x2 = x.reshape(b * s, d)
    gq, gk, gv, gr, g, sq, sk, sv, w1, w2, wo = _in_projection(x2, tok_tables, w_all, small,
                                                              w_ff1[0], w_ff2[0], w_out[0], ROWS_IN)

    def seq(a):
        return a.reshape(b, s, a.shape[-1])

    o_gla = _gla_mixer(seq(gq), seq(gk), seq(gv), seq(gr), seq(g), km, vm, gm,
                       gla_norm_w[0].reshape(1, GLA_DV), ROWS_GLA)
    y = _swa_ffn(sinks[0].astype(F32), sq, sk, sv, m_sk, m_sv, x2, o_gla.reshape(b * s, GV), wo,
                 norm_ff_w[0].reshape(1, D_MODEL), w1, w2, final_norm_w.reshape(1, D_MODEL),
                 ROWS_OUT, s // ROWS_OUT)
    return y.reshape(b, s, d)
```

```python
import functools

import jax
import jax.numpy as jnp
from jax import lax
from jax.experimental import pallas as pl
from jax.experimental.pallas import tpu as pltpu

F32 = jnp.float32
BF16 = jnp.bfloat16

D_MODEL = 1024
N_META = 16
GLA_HEADS = 4
GLA_DK = 64
GLA_DV = 128
GLA_RANK = 16
GLA_TAU = 16.0
GLA_CHUNK = 64
GLA_LEVELS = 6
SWA_HEADS = 8
SWA_KV_HEADS = 2
SWA_HD = 64
SWA_BLOCK = 128
SWA_HALF = SWA_BLOCK // 2
ROPE_DIM = 16
ROPE_THETA = 500000.0
D_FF = 4096
EPS = 1e-5
LOG2_E = 1.4426950408889634

LANES = 128
GQ = GLA_HEADS * GLA_DK
GV = GLA_HEADS * GLA_DV
SQ = SWA_HEADS * SWA_HD
SKV = SWA_KV_HEADS * SWA_HD
FF_CHUNK = 1024

ROWS_IN = 1024
ROWS_IN_SUB = 256
W_SQ = 2 * GQ + 2 * GV
W_SK = W_SQ + SQ
W_LR = W_SK + 2 * SKV
ROWS_GLA = 512
GLA_UNROLL = 4
ROWS_OUT = 512
VMEM_LIMIT = 56 * 1024 * 1024


def _nt_dot(a, b):
    return lax.dot_general(a, b, (((1,), (1,)), ((), ())), preferred_element_type=F32)


def _tn_dot(a, b):
    return lax.dot_general(a, b, (((0,), (0,)), ((), ())), preferred_element_type=F32)


def _rms(x, w):
    return x * lax.rsqrt(jnp.mean(x * x, axis=-1, keepdims=True) + EPS) * w


def _project(x_ref, nw_ref, w_ref, wg_ref, bg_ref, cq_ref, cos_ref, sup_ref, sdn_ref,
             gq_ref, gk_ref, gv_ref, gr_ref, g_ref, sq_ref, sk_ref, sv_ref):
    half = ROPE_DIM // 2
    n_sub = x_ref.shape[0] // ROWS_IN_SUB if x_ref.shape[0] > ROWS_IN_SUB else 1
    sub = x_ref.shape[0] // n_sub
    for i in range(n_sub):
        rows = slice(i * sub, (i + 1) * sub)
        u = _rms(x_ref[rows, :], nw_ref[...]).astype(BF16)

        def proj(lo, width):
            return jnp.dot(u, w_ref[:, lo:lo + width], preferred_element_type=F32)

        qk = proj(0, 2 * GQ)
        gq_ref[rows, :] = (qk[:, :GQ] * (GLA_DK ** -0.5)).astype(BF16)
        gk_ref[rows, :] = qk[:, GQ:].astype(BF16)
        gv_ref[rows, :] = proj(2 * GQ, GV).astype(BF16)
        gr_ref[rows, :] = proj(2 * GQ + GV, GV).astype(BF16)

        lr = proj(W_LR, LANES).astype(BF16)
        z = jnp.dot(lr, wg_ref[...], preferred_element_type=F32) + bg_ref[...]
        g_ref[rows, :] = (jnp.minimum(z, 0.0) - jnp.log1p(jnp.exp(-jnp.abs(z)))) * (LOG2_E / GLA_TAU)

        cos, s_up, s_dn = cos_ref[rows, :], sup_ref[rows, :], sdn_ref[rows, :]

        def rope(t):
            return t * cos + pltpu.roll(t, LANES - half, axis=1) * s_up + pltpu.roll(t, half, axis=1) * s_dn

        sq = proj(W_SQ, SQ)
        for s in range(SQ // LANES):
            sl = slice(s * LANES, (s + 1) * LANES)
            sq_ref[rows, sl] = (rope(sq[:, sl]) * cq_ref[...]).astype(BF16)
        kv = proj(W_SK, 2 * SKV)
        sk_ref[rows, :] = rope(kv[:, :SKV]).astype(BF16)
        sv_ref[rows, :] = kv[:, SKV:].astype(BF16)


PROJ_WIDTHS = (GQ, GQ, GV, GV, GQ, SQ, SKV, SKV)
PROJ_DTYPES = (BF16, BF16, BF16, BF16, F32, BF16, BF16, BF16)


def _meta_prep_kernel(x_ref, wt_ref, nw_ref, wg_ref, bg_ref, cq_ref, cos_ref, sup_ref, sdn_ref, w_ref, *out_refs):
    o_lr = 2 * GQ + 2 * GV
    o_sq = o_lr + GLA_RANK
    o_sk = o_sq + SQ
    half = SWA_HEADS // 2

    def rows(lo, n):
        return wt_ref[lo:lo + n, :].astype(BF16)

    slabs = [rows(c * LANES, LANES) for c in range(o_lr // LANES)]
    slabs += [jnp.concatenate([rows(o_sq + t * SWA_HD, SWA_HD), rows(o_sq + (t + half) * SWA_HD, SWA_HD)], axis=0)
              for t in range(half)]
    slabs += [rows(o_sk + c * LANES, LANES) for c in range(2 * SKV // LANES)]
    slabs += [jnp.concatenate([rows(o_lr, GLA_RANK), jnp.zeros((LANES - GLA_RANK, D_MODEL), BF16)], axis=0)]
    for c, slab in enumerate(slabs):
        w_ref[:, c * LANES:(c + 1) * LANES] = slab.T
    _project(x_ref, nw_ref, w_ref, wg_ref, bg_ref, cq_ref, cos_ref, sup_ref, sdn_ref, *out_refs)


def _meta_prep(meta_tokens, w_in_t, small, pos_table):
    n = meta_tokens.shape[0]
    out_shape = ([jax.ShapeDtypeStruct((D_MODEL, W_LR + LANES), BF16)]
                 + [jax.ShapeDtypeStruct((n, w), d) for w, d in zip(PROJ_WIDTHS, PROJ_DTYPES)])

    def full(a):
        return pl.BlockSpec(a.shape, lambda i: (0,) * len(a.shape))

    return pl.pallas_call(
        _meta_prep_kernel,
        grid=(1,),
        in_specs=([full(meta_tokens), full(w_in_t)] + [full(a) for a in small]
                  + [pl.BlockSpec((n, LANES), lambda i, which=which: (0, which)) for which in range(3)]),
        out_specs=[full(o) for o in out_shape],
        out_shape=out_shape,
        compiler_params=pltpu.CompilerParams(dimension_semantics=("arbitrary",),
                                             vmem_limit_bytes=VMEM_LIMIT),
        name="meta_prep",
    )(meta_tokens, w_in_t, *small, pos_table, pos_table, pos_table)


def _inproj_kernel(x_ref, nw_ref, w_ref, wg_ref, bg_ref, cq_ref, cos_ref, sup_ref, sdn_ref,
                   w1_ref, w2_ref, wo_ref, *out_refs):
    *proj_refs, w1b_ref, w2b_ref, wob_ref = out_refs
    _project(x_ref, nw_ref, w_ref, wg_ref, bg_ref, cq_ref, cos_ref, sup_ref, sdn_ref, *proj_refs)
    w1b_ref[...] = w1_ref[...].astype(BF16)
    w2b_ref[...] = w2_ref[...].astype(BF16)
    wob_ref[...] = wo_ref[...].astype(BF16)


def _in_projection(x2, pos_table, w_all, small, w1, w2, wo, rows):
    n = x2.shape[0]
    steps = n // rows
    nblk_pos = pos_table.shape[0] // rows
    assert w1.shape[0] % steps == 0 and w2.shape[0] % steps == 0 and wo.shape[0] == steps * SWA_HD

    def row_spec(width):
        return pl.BlockSpec((rows, width), lambda i: (i, 0))

    def pos_spec(which):
        return pl.BlockSpec((rows, LANES), lambda i: (i % nblk_pos, which))

    def full(a):
        return pl.BlockSpec(a.shape, lambda i: (0,) * a.ndim)

    def slice_spec(a):
        return pl.BlockSpec((a.shape[0] // steps, a.shape[1]), lambda i: (i, 0))

    n_gla = GV // SWA_HD
    half = SWA_HEADS // 2

    def wo_src(i):
        j = i - n_gla
        return (jnp.where(j < 0, i, n_gla + j // 2 + half * (j % 2)), 0)

    in_specs = ([row_spec(D_MODEL), full(small[0]), full(w_all)] + [full(a) for a in small[1:]]
                + [pos_spec(0), pos_spec(1), pos_spec(2), slice_spec(w1), slice_spec(w2),
                   pl.BlockSpec((SWA_HD, wo.shape[1]), wo_src)])
    weights_out = [jax.ShapeDtypeStruct(a.shape, BF16) for a in (w1, w2, wo)]
    return pl.pallas_call(
        _inproj_kernel,
        grid=(steps,),
        in_specs=in_specs,
        out_specs=[row_spec(w) for w in PROJ_WIDTHS] + [slice_spec(w1), slice_spec(w2), slice_spec(wo)],
        out_shape=[jax.ShapeDtypeStruct((n, w), d) for w, d in zip(PROJ_WIDTHS, PROJ_DTYPES)] + weights_out,
        compiler_params=pltpu.CompilerParams(dimension_semantics=("arbitrary",),
                                             vmem_limit_bytes=VMEM_LIMIT),
        name="in_projection",
    )(x2, small[0], w_all, *small[1:], pos_table, pos_table, pos_table, w1, w2, wo)


def _block_cumsums(g):
    rows = g.shape[0]
    row = lax.broadcasted_iota(jnp.int32, g.shape, 0)
    w, t = [g], [g]
    for k in range(GLA_LEVELS):
        s = 1 << k
        upper = (row & s) != 0
        below = pltpu.roll(t[k], s, axis=0)
        above = pltpu.roll(t[k], rows - s, axis=0)
        w.append(w[k] + jnp.where(upper, below, 0.0))
        t.append(t[k] + jnp.where(upper, below, above))
    return w, t


def _gla_levels():
    ri = lax.broadcasted_iota(jnp.int32, (GLA_CHUNK, LANES), 0)
    ci = lax.broadcasted_iota(jnp.int32, (GLA_CHUNK, LANES), 1) & (GLA_CHUNK - 1)
    x = ri ^ ci
    level = jnp.zeros_like(x)
    for b in range(GLA_LEVELS):
        level = level + (x >= (1 << b)).astype(jnp.int32)
    return jnp.where(ci > ri, -1, level)


def _gla_chunk(q, k, v, g, st, level):
    w, t = _block_cumsums(g)
    n_slab = GQ // LANES

    def block_diag_t(m):
        r = m.shape[0]
        m2 = jnp.concatenate([m, m], axis=0).T
        same_head = ((lax.broadcasted_iota(jnp.int32, m2.shape, 0) < GLA_DK)
                     == (lax.broadcasted_iota(jnp.int32, m2.shape, 1) < r))
        return jnp.where(same_head, m2, jnp.zeros_like(m2))

    o = None
    if level is not None:
        a = [jnp.zeros((GLA_CHUNK, LANES), F32) for _ in range(n_slab)]
        for lv in range(GLA_LEVELS + 1):
            if lv == 0:
                qt, kt = q.astype(BF16), k.astype(BF16)
            else:
                qt = (q * jnp.exp2(w[lv - 1])).astype(BF16)
                kt = (k * jnp.exp2(t[lv - 1] - w[lv - 1])).astype(BF16)
            for s in range(n_slab):
                sl = slice(s * LANES, (s + 1) * LANES)
                scores = jnp.dot(qt[:, sl], block_diag_t(kt[:, sl]), preferred_element_type=F32)
                a[s] = jnp.where(level == lv, scores, a[s])

        qb = (q * jnp.exp2(w[GLA_LEVELS])).astype(BF16)
        st_b = st.astype(BF16)
        outs = []
        for s in range(n_slab):
            sl = slice(s * LANES, (s + 1) * LANES)
            v_ab = v[:, (2 * s) * GLA_DV:(2 * s + 2) * GLA_DV]
            first = lax.broadcasted_iota(jnp.int32, v_ab.shape, 1) < GLA_DV
            zero = jnp.zeros_like(v_ab)
            v_bd = jnp.concatenate([jnp.where(first, v_ab, zero), jnp.where(first, zero, v_ab)], axis=0)
            outs.append(jnp.dot(a[s].astype(BF16), v_bd, preferred_element_type=F32)
                        + jnp.dot(qb[:, sl], block_diag_t(st_b[:, sl]), preferred_element_type=F32))
        o = jnp.concatenate(outs, axis=1)

    tot = t[GLA_LEVELS]
    ku = (k * jnp.exp2(tot - w[GLA_LEVELS])).astype(BF16)
    decayed = st * jnp.exp2(tot[0:1, :])
    lo = lax.broadcasted_iota(jnp.int32, (GLA_DV, LANES), 1) < GLA_DK
    new_st = []
    for s in range(n_slab):
        sl = slice(s * LANES, (s + 1) * LANES)
        upd = _tn_dot(v[:, (2 * s) * GLA_DV:(2 * s + 2) * GLA_DV], ku[:, sl])
        new_st.append(decayed[:, sl] + jnp.where(lo, upd[:GLA_DV], upd[GLA_DV:]))
    return o, jnp.concatenate(new_st, axis=1)


def _gla_gate(o, r, nw):
    out = []
    for h in range(GLA_HEADS):
        sl = slice(h * GLA_DV, (h + 1) * GLA_DV)
        rh = r[:, sl].astype(F32)
        out.append((_rms(o[:, sl], nw) * (rh * jax.nn.sigmoid(rh))).astype(BF16))
    return jnp.concatenate(out, axis=1)


def _gla_kernel(q_ref, k_ref, v_ref, r_ref, g_ref, km_ref, vm_ref, gm_ref, nw_ref, o_ref, st_ref):
    @pl.when(pl.program_id(1) == 0)
    def _():
        st_ref[...] = jnp.zeros(st_ref.shape, F32)
        _, st0 = _gla_chunk(None, km_ref[...].astype(F32), vm_ref[...], gm_ref[...], st_ref[...], None)
        st_ref[...] = st0

    nw = nw_ref[...]
    level = _gla_levels()

    def body(c, carry):
        st = st_ref[...]
        for u in range(GLA_UNROLL):
            rows = pl.ds(pl.multiple_of((c * GLA_UNROLL + u) * GLA_CHUNK, GLA_CHUNK), GLA_CHUNK)
            o, st = _gla_chunk(q_ref[0, rows, :].astype(F32), k_ref[0, rows, :].astype(F32),
                               v_ref[0, rows, :], g_ref[0, rows, :], st, level)
            o_ref[0, rows, :] = _gla_gate(o, r_ref[0, rows, :], nw)
        st_ref[...] = st
        return carry

    lax.fori_loop(0, q_ref.shape[1] // (GLA_CHUNK * GLA_UNROLL), body, 0)


def _gla_mixer(gq, gk, gv, gr, g, km, vm, gm, gla_norm_w, rows):
    b, s, _ = gq.shape

    def seq_spec(width):
        return pl.BlockSpec((1, rows, width), lambda i, j: (i, j, 0))

    def full(a):
        return pl.BlockSpec(a.shape, lambda i, j: (0,) * a.ndim)

    return pl.pallas_call(
        _gla_kernel,
        grid=(b, s // rows),
        in_specs=[seq_spec(GQ), seq_spec(GQ), seq_spec(GV), seq_spec(GV), seq_spec(GQ),
                  full(km), full(vm), full(gm), full(gla_norm_w)],
        out_specs=seq_spec(GV),
        out_shape=jax.ShapeDtypeStruct((b, s, GV), BF16),
        scratch_shapes=[pltpu.VMEM((GLA_DV, GQ), F32)],
        compiler_params=pltpu.CompilerParams(dimension_semantics=("arbitrary", "arbitrary"),
                                             vmem_limit_bytes=VMEM_LIMIT),
        name="gla_mixer",
    )(gq, gk, gv, gr, g, km, vm, gm, gla_norm_w)


def _swa_block(sink_ref, q, kc, kp, vc, vp, km, vm, has_prev):
    n_win = SWA_BLOCK + SWA_HALF
    n_keys = n_win + N_META
    r = lax.broadcasted_iota(jnp.int32, (SWA_HALF, n_keys), 0)
    c = lax.broadcasted_iota(jnp.int32, (SWA_HALF, n_keys), 1)
    in_band = (c > r) & (c <= r + SWA_BLOCK)
    lane = lax.broadcasted_iota(jnp.int32, (SWA_HALF, LANES), 1)
    low = lane < SWA_HD

    halves = []
    for half in range(2):
        rows = slice(half * SWA_HALF, (half + 1) * SWA_HALF)
        if half == 0:
            k_all = jnp.concatenate([kp, kc[:SWA_HALF], km], axis=0)
            v_all = jnp.concatenate([vp, vc[:SWA_HALF], vm], axis=0)
            n_prev = SWA_BLOCK
        else:
            k_all = jnp.concatenate([kp[SWA_HALF:], kc, km], axis=0)
            v_all = jnp.concatenate([vp[SWA_HALF:], vc, vm], axis=0)
            n_prev = SWA_HALF
        valid = (c >= n_win) | in_band
        if has_prev is not None:
            valid = (c >= n_win) | (in_band & (has_prev | (c >= n_prev)))
        bias = jnp.where(valid, 0.0, -jnp.inf)
        pieces = []
        for t in range(SWA_HEADS // 2):
            qt = q[rows, t * LANES:(t + 1) * LANES]
            pieces += [jnp.where(low, qt, jnp.zeros_like(qt)), jnp.where(low, jnp.zeros_like(qt), qt)]
        q_all = jnp.concatenate(pieces, axis=0)
        s_all = _nt_dot(q_all, k_all)
        p, denom = [], []
        for i in range(SWA_HEADS):
            s = s_all[i * SWA_HALF:(i + 1) * SWA_HALF] + bias
            sink = sink_ref[(i // 2) + (i % 2) * (SWA_HEADS // 2)] * LOG2_E
            m = jnp.maximum(jnp.max(s, axis=-1, keepdims=True), sink)
            p.append(jnp.exp2(s - m))
            denom.append(jnp.sum(p[i], axis=-1, keepdims=True) + jnp.exp2(sink - m))
        o = jnp.dot(jnp.concatenate(p, axis=0).astype(BF16), v_all, preferred_element_type=F32)
        slabs = []
        for t in range(SWA_HEADS // 2):
            lo = o[(2 * t) * SWA_HALF:(2 * t + 1) * SWA_HALF] / denom[2 * t]
            hi = o[(2 * t + 1) * SWA_HALF:(2 * t + 2) * SWA_HALF] / denom[2 * t + 1]
            slabs.append(jnp.where(low, lo, hi).astype(BF16))
        halves.append(jnp.concatenate(slabs, axis=1))
    return jnp.concatenate(halves, axis=0)


def _swa_ffn_kernel(tiles_per_seq, sink_ref, sq0_ref, sk0_ref, sv0_ref, sq_ref, sk_ref, sv_ref, skp_ref, svp_ref,
                    km_ref, vm_ref, x_ref, og_ref, wo_ref, nf_ref, w1_ref, w2_ref, nl_ref, y_ref, os_ref):
    s = pl.program_id(0)
    ahead = jnp.minimum(s + 1, pl.num_programs(0) - 1)
    seq_start = lax.rem(ahead, tiles_per_seq) == 0
    cur = lax.rem(s, 2)

    def swa_tile_block(j, slot, q_ref, k_ref, v_ref, first_prev):
        rows = slice(j * SWA_BLOCK, (j + 1) * SWA_BLOCK)
        if j == 0:
            kp, vp, has_prev = first_prev
        else:
            before = slice((j - 1) * SWA_BLOCK, j * SWA_BLOCK)
            kp, vp, has_prev = k_ref[before, :], v_ref[before, :], None
        os_ref[slot, rows, :] = _swa_block(sink_ref, q_ref[rows, :], k_ref[rows, :], kp, v_ref[rows, :], vp,
                                           km_ref[...], vm_ref[...], has_prev)

    n_blocks = x_ref.shape[0] // SWA_BLOCK

    @pl.when(s == 0)
    def _():
        for j in range(n_blocks):
            swa_tile_block(j, 0, sq0_ref, sk0_ref, sv0_ref, (sk0_ref[0:SWA_BLOCK, :], sv0_ref[0:SWA_BLOCK, :], s > 0))

    n_ff = D_FF // FF_CHUNK
    swa_per_ff = n_blocks // n_ff

    h = (x_ref[...]
         + jnp.dot(og_ref[...], wo_ref[0:GV, :], preferred_element_type=F32)
         + jnp.dot(os_ref[cur], wo_ref[GV:GV + SQ, :], preferred_element_type=F32))
    f = _rms(h, nf_ref[...]).astype(BF16)
    ff = None
    for c in range(n_ff):
        for j in range(c * swa_per_ff, (c + 1) * swa_per_ff):
            swa_tile_block(j, 1 - cur, sq_ref, sk_ref, sv_ref,
                           (skp_ref[...], svp_ref[...], jnp.logical_not(seq_start)))
        sl = slice(c * FF_CHUNK, (c + 1) * FF_CHUNK)
        a = jnp.maximum(jnp.dot(f, w1_ref[:, sl], preferred_element_type=F32), 0.0)
        d = jnp.dot((a * a).astype(BF16), w2_ref[sl, :], preferred_element_type=F32)
        ff = d if ff is None else ff + d
    y_ref[...] = _rms(h + ff, nl_ref[...])


def _swa_ffn(sinks, sq, sk, sv, km, vm, x2, og, wo, nf, w1, w2, nl, rows, tiles_per_seq):
    n = x2.shape[0]
    n_tiles = n // rows
    prev_per_tile = rows // SWA_BLOCK

    def first_spec(width):
        return pl.BlockSpec((rows, width), lambda i: (0, 0))

    def ahead_spec(width):
        return pl.BlockSpec((rows, width), lambda i: (jnp.minimum(i + 1, n_tiles - 1), 0))

    def prev_spec(width):
        return pl.BlockSpec((SWA_BLOCK, width),
                            lambda i: (jnp.minimum(i + 1, n_tiles - 1) * prev_per_tile - 1, 0))

    def ffn_spec(width):
        return pl.BlockSpec((rows, width), lambda i: (i, 0))

    def full(a):
        return pl.BlockSpec(a.shape, lambda i: (0,) * a.ndim, pipeline_mode=pl.Buffered(1))

    weights = (wo, nf, w1, w2, nl)
    return pl.pallas_call(
        functools.partial(_swa_ffn_kernel, tiles_per_seq),
        grid=(n_tiles,),
        in_specs=([pl.BlockSpec(memory_space=pltpu.SMEM), first_spec(SQ), first_spec(SKV), first_spec(SKV),
                   ahead_spec(SQ), ahead_spec(SKV), ahead_spec(SKV), prev_spec(SKV), prev_spec(SKV),
                   full(km), full(vm), ffn_spec(D_MODEL), ffn_spec(GV)] + [full(a) for a in weights]),
        out_specs=ffn_spec(D_MODEL),
        out_shape=jax.ShapeDtypeStruct((n, D_MODEL), F32),
        scratch_shapes=[pltpu.VMEM((2, rows, SQ), BF16)],
        compiler_params=pltpu.CompilerParams(dimension_semantics=("arbitrary",),
                                             vmem_limit_bytes=VMEM_LIMIT),
        name="swa_ffn",
    )(sinks, sq, sk, sv, sq, sk, sv, sk, sv, km, vm, x2, og, *weights)


def _rope_table(first_pos, n_pos):
    half = ROPE_DIM // 2
    inv_freq = 1.0 / (ROPE_THETA ** (jnp.arange(0, ROPE_DIM, 2, dtype=F32) / ROPE_DIM))
    ang = jnp.arange(first_pos, first_pos + n_pos, dtype=jnp.int32).astype(F32)[:, None] * inv_freq[None, :]
    cs = jnp.concatenate([jnp.cos(ang), jnp.sin(ang)], axis=1)
    lane = jnp.arange(3 * LANES)
    d = lane % SWA_HD
    which = lane // LANES
    src = jnp.where(which == 0, d % half, half + d % half)
    live = jnp.where(which == 0, d < ROPE_DIM, jnp.where(which == 1, d < half, (d >= half) & (d < ROPE_DIM)))
    sign = jnp.where(which == 1, -1.0, 1.0)
    spread = jnp.where(live[None, :] & (jnp.arange(2 * half)[:, None] == src[None, :]), sign[None, :], 0.0)
    unrotated = jnp.where((which == 0) & (d >= ROPE_DIM), 1.0, 0.0)
    return jnp.dot(cs, spread.astype(F32), precision=lax.Precision.HIGHEST) + unrotated[None, :].astype(F32)


def _swa_head_order():
    order = []
    for t in range(SWA_HEADS // 2):
        order += [t, t + SWA_HEADS // 2]
    return order


def kernel(x, meta_tokens, norm_mix_w, w_in, w_gate_up, b_gate, gla_norm_w, sinks, w_out, norm_ff_w,
           w_ff1, w_ff2, final_norm_w):
    b, s, d = x.shape
    assert d == D_MODEL and s % ROWS_IN == 0 and s % ROWS_OUT == 0 and w_in.shape[0] == 1
    assert s % ROWS_GLA == 0 and ROWS_GLA % (GLA_CHUNK * GLA_UNROLL) == 0
    assert ROWS_OUT % (SWA_BLOCK * (D_FF // FF_CHUNK)) == 0

    wg = jnp.pad(w_gate_up[0], ((0, LANES - GLA_RANK), (0, 0))).astype(BF16)
    bg = b_gate[0].reshape(1, GQ)
    nw = norm_mix_w[0].reshape(1, D_MODEL)
    cq = jnp.full((1, LANES), SWA_HD ** -0.5 * LOG2_E, F32)
    small = (nw, wg, bg, cq)

    meta_tables = _rope_table(0, N_META)
    tok_tables = _rope_table(N_META, s)

    w_all, _, m_gk, m_gv, _, m_g, _, m_sk, m_sv = _meta_prep(meta_tokens.astype(F32), w_in[0].T, small, meta_tables)
    front = ((GLA_CHUNK - N_META, 0), (0, 0))
    km, vm, gm = jnp.pad(m_gk, front), jnp.pad(m_gv, front), jnp.pad(m_g, front)

    x2 = x.reshape(b * s, d)
    gq, gk, gv, gr, g, sq, sk, sv, w1, w2, wo = _in_projection(x2, tok_tables, w_all, small,
                                                              w_ff1[0], w_ff2[0], w_out[0], ROWS_IN)

    def seq(a):
        return a.reshape(b, s, a.shape[-1])

    o_gla = _gla_mixer(seq(gq), seq(gk), seq(gv), seq(gr), seq(g), km, vm, gm,
                       gla_norm_w[0].reshape(1, GLA_DV), ROWS_GLA)
    y = _swa_ffn(sinks[0].astype(F32), sq, sk, sv, m_sk, m_sv, x2, o_gla.reshape(b * s, GV), wo,
                 norm_ff_w[0].reshape(1, D_MODEL), w1, w2, final_norm_w.reshape(1, D_MODEL),
                 ROWS_OUT, s // ROWS_OUT)
    return y.reshape(b, s, d)
```

```python
import functools

import jax
import jax.numpy as jnp
from jax import lax
from jax.experimental import pallas as pl
from jax.experimental.pallas import tpu as pltpu

F32 = jnp.float32
BF16 = jnp.bfloat16

D_MODEL = 1024
N_META = 16
GLA_HEADS = 4
GLA_DK = 64
GLA_DV = 128
GLA_RANK = 16
GLA_TAU = 16.0
GLA_CHUNK = 64
GLA_LEVELS = 6
SWA_HEADS = 8
SWA_KV_HEADS = 2
SWA_HD = 64
SWA_BLOCK = 128
SWA_HALF = SWA_BLOCK // 2
ROPE_DIM = 16
ROPE_THETA = 500000.0
D_FF = 4096
EPS = 1e-5
LOG2_E = 1.4426950408889634

LANES = 128
GQ = GLA_HEADS * GLA_DK
GV = GLA_HEADS * GLA_DV
SQ = SWA_HEADS * SWA_HD
SKV = SWA_KV_HEADS * SWA_HD
FF_CHUNK = 1024

ROWS_IN = 1024
ROWS_IN_SUB = 256
W_SQ = 2 * GQ + 2 * GV
W_SK = W_SQ + SQ
W_LR = W_SK + 2 * SKV
ROWS_GLA = 512
GLA_UNROLL = 4
ROWS_OUT = 512
VMEM_LIMIT = 56 * 1024 * 1024


def _nt_dot(a, b):
    return lax.dot_general(a, b, (((1,), (1,)), ((), ())), preferred_element_type=F32)


def _tn_dot(a, b):
    return lax.dot_general(a, b, (((0,), (0,)), ((), ())), preferred_element_type=F32)


def _rms(x, w):
    return x * lax.rsqrt(jnp.mean(x * x, axis=-1, keepdims=True) + EPS) * w


def _project(x_ref, nw_ref, w_ref, wg_ref, bg_ref, cq_ref, cos_ref, sup_ref, sdn_ref, gla_ref, g_ref, swa_ref):
    half = ROPE_DIM // 2
    n_sub = x_ref.shape[0] // ROWS_IN_SUB if x_ref.shape[0] > ROWS_IN_SUB else 1
    sub = x_ref.shape[0] // n_sub
    for i in range(n_sub):
        rows = slice(i * sub, (i + 1) * sub)
        u = _rms(x_ref[rows, :], nw_ref[...]).astype(BF16)

        def proj(lo, width):
            return jnp.dot(u, w_ref[:, lo:lo + width], preferred_element_type=F32)

        qk = proj(0, 2 * GQ)
        gla_ref[rows, 0:GQ] = (qk[:, :GQ] * (GLA_DK ** -0.5)).astype(BF16)
        gla_ref[rows, GQ:2 * GQ] = qk[:, GQ:].astype(BF16)
        gla_ref[rows, 2 * GQ:2 * GQ + GV] = proj(2 * GQ, GV).astype(BF16)
        gla_ref[rows, 2 * GQ + GV:2 * GQ + 2 * GV] = proj(2 * GQ + GV, GV).astype(BF16)

        lr = proj(W_LR, LANES).astype(BF16)
        z = jnp.dot(lr, wg_ref[...], preferred_element_type=F32) + bg_ref[...]
        g_ref[rows, :] = (jnp.minimum(z, 0.0) - jnp.log1p(jnp.exp(-jnp.abs(z)))) * (LOG2_E / GLA_TAU)

        cos, s_up, s_dn = cos_ref[rows, :], sup_ref[rows, :], sdn_ref[rows, :]

        def rope(t):
            return t * cos + pltpu.roll(t, LANES - half, axis=1) * s_up + pltpu.roll(t, half, axis=1) * s_dn

        sq = proj(W_SQ, SQ)
        for s in range(SQ // LANES):
            sl = slice(s * LANES, (s + 1) * LANES)
            swa_ref[rows, sl] = (rope(sq[:, sl]) * cq_ref[...]).astype(BF16)
        kv = proj(W_SK, 2 * SKV)
        swa_ref[rows, SQ:SQ + SKV] = rope(kv[:, :SKV]).astype(BF16)
        swa_ref[rows, SQ + SKV:SQ + 2 * SKV] = kv[:, SKV:].astype(BF16)


PROJ_WIDTHS = (2 * GQ + 2 * GV, GQ, SQ + 2 * SKV)
PROJ_DTYPES = (BF16, F32, BF16)


def _meta_prep_kernel(x_ref, wt_ref, nw_ref, wg_ref, bg_ref, cq_ref, cos_ref, sup_ref, sdn_ref, w_ref, *out_refs):
    o_lr = 2 * GQ + 2 * GV
    o_sq = o_lr + GLA_RANK
    o_sk = o_sq + SQ
    half = SWA_HEADS // 2

    def rows(lo, n):
        return wt_ref[lo:lo + n, :].astype(BF16)

    slabs = [rows(c * LANES, LANES) for c in range(o_lr // LANES)]
    slabs += [jnp.concatenate([rows(o_sq + t * SWA_HD, SWA_HD), rows(o_sq + (t + half) * SWA_HD, SWA_HD)], axis=0)
              for t in range(half)]
    slabs += [rows(o_sk + c * LANES, LANES) for c in range(2 * SKV // LANES)]
    slabs += [jnp.concatenate([rows(o_lr, GLA_RANK), jnp.zeros((LANES - GLA_RANK, D_MODEL), BF16)], axis=0)]
    for c, slab in enumerate(slabs):
        w_ref[:, c * LANES:(c + 1) * LANES] = slab.T
    _project(x_ref, nw_ref, w_ref, wg_ref, bg_ref, cq_ref, cos_ref, sup_ref, sdn_ref, *out_refs)


def _meta_prep(meta_tokens, w_in_t, small, pos_table):
    n = meta_tokens.shape[0]
    out_shape = ([jax.ShapeDtypeStruct((D_MODEL, W_LR + LANES), BF16)]
                 + [jax.ShapeDtypeStruct((n, w), d) for w, d in zip(PROJ_WIDTHS, PROJ_DTYPES)])

    def full(a):
        return pl.BlockSpec(a.shape, lambda i: (0,) * len(a.shape))

    return pl.pallas_call(
        _meta_prep_kernel,
        grid=(1,),
        in_specs=([full(meta_tokens), full(w_in_t)] + [full(a) for a in small]
                  + [pl.BlockSpec((n, LANES), lambda i, which=which: (0, which)) for which in range(3)]),
        out_specs=[full(o) for o in out_shape],
        out_shape=out_shape,
        compiler_params=pltpu.CompilerParams(dimension_semantics=("arbitrary",),
                                             vmem_limit_bytes=VMEM_LIMIT),
        name="meta_prep",
    )(meta_tokens, w_in_t, *small, pos_table, pos_table, pos_table)


def _inproj_kernel(x_ref, nw_ref, w_ref, wg_ref, bg_ref, cq_ref, cos_ref, sup_ref, sdn_ref,
                   w1_ref, w2_ref, wo_ref, *out_refs):
    *proj_refs, w1b_ref, w2b_ref, wob_ref = out_refs
    _project(x_ref, nw_ref, w_ref, wg_ref, bg_ref, cq_ref, cos_ref, sup_ref, sdn_ref, *proj_refs)
    w1b_ref[...] = w1_ref[...].astype(BF16)
    w2b_ref[...] = w2_ref[...].astype(BF16)
    wob_ref[...] = wo_ref[...].astype(BF16)


def _in_projection(x2, pos_table, w_all, small, w1, w2, wo, rows):
    n = x2.shape[0]
    steps = n // rows
    nblk_pos = pos_table.shape[0] // rows
    assert w1.shape[0] % steps == 0 and w2.shape[0] % steps == 0 and wo.shape[0] == steps * SWA_HD

    def row_spec(width):
        return pl.BlockSpec((rows, width), lambda i: (i, 0))

    def pos_spec(which):
        return pl.BlockSpec((rows, LANES), lambda i: (i % nblk_pos, which))

    def full(a):
        return pl.BlockSpec(a.shape, lambda i: (0,) * a.ndim)

    def slice_spec(a):
        return pl.BlockSpec((a.shape[0] // steps, a.shape[1]), lambda i: (i, 0))

    n_gla = GV // SWA_HD
    half = SWA_HEADS // 2

    def wo_src(i):
        j = i - n_gla
        return (jnp.where(j < 0, i, n_gla + j // 2 + half * (j % 2)), 0)

    in_specs = ([row_spec(D_MODEL), full(small[0]), full(w_all)] + [full(a) for a in small[1:]]
                + [pos_spec(0), pos_spec(1), pos_spec(2), slice_spec(w1), slice_spec(w2),
                   pl.BlockSpec((SWA_HD, wo.shape[1]), wo_src)])
    weights_out = [jax.ShapeDtypeStruct(a.shape, BF16) for a in (w1, w2, wo)]
    return pl.pallas_call(
        _inproj_kernel,
        grid=(steps,),
        in_specs=in_specs,
        out_specs=[row_spec(w) for w in PROJ_WIDTHS] + [slice_spec(w1), slice_spec(w2), slice_spec(wo)],
        out_shape=[jax.ShapeDtypeStruct((n, w), d) for w, d in zip(PROJ_WIDTHS, PROJ_DTYPES)] + weights_out,
        compiler_params=pltpu.CompilerParams(dimension_semantics=("arbitrary",),
                                             vmem_limit_bytes=VMEM_LIMIT),
        name="in_projection",
    )(x2, small[0], w_all, *small[1:], pos_table, pos_table, pos_table, w1, w2, wo)


def _block_cumsums(g):
    rows = g.shape[0]
    row = lax.broadcasted_iota(jnp.int32, g.shape, 0)
    w, t = [g], [g]
    for k in range(GLA_LEVELS):
        s = 1 << k
        upper = (row & s) != 0
        below = pltpu.roll(t[k], s, axis=0)
        above = pltpu.roll(t[k], rows - s, axis=0)
        w.append(w[k] + jnp.where(upper, below, 0.0))
        t.append(t[k] + jnp.where(upper, below, above))
    return w, t


def _gla_levels():
    ri = lax.broadcasted_iota(jnp.int32, (GLA_CHUNK, LANES), 0)
    ci = lax.broadcasted_iota(jnp.int32, (GLA_CHUNK, LANES), 1) & (GLA_CHUNK - 1)
    x = ri ^ ci
    level = jnp.zeros_like(x)
    for b in range(GLA_LEVELS):
        level = level + (x >= (1 << b)).astype(jnp.int32)
    return jnp.where(ci > ri, -1, level)


def _gla_chunk(q, k, v, g, st, level):
    w, t = _block_cumsums(g)
    n_slab = GQ // LANES

    def block_diag_t(m):
        r = m.shape[0]
        m2 = jnp.concatenate([m, m], axis=0).T
        same_head = ((lax.broadcasted_iota(jnp.int32, m2.shape, 0) < GLA_DK)
                     == (lax.broadcasted_iota(jnp.int32, m2.shape, 1) < r))
        return jnp.where(same_head, m2, jnp.zeros_like(m2))

    o = None
    if level is not None:
        a = [jnp.zeros((GLA_CHUNK, LANES), F32) for _ in range(n_slab)]
        for lv in range(GLA_LEVELS + 1):
            if lv == 0:
                qt, kt = q.astype(BF16), k.astype(BF16)
            else:
                qt = (q * jnp.exp2(w[lv - 1])).astype(BF16)
                kt = (k * jnp.exp2(t[lv - 1] - w[lv - 1])).astype(BF16)
            for s in range(n_slab):
                sl = slice(s * LANES, (s + 1) * LANES)
                scores = jnp.dot(qt[:, sl], block_diag_t(kt[:, sl]), preferred_element_type=F32)
                a[s] = jnp.where(level == lv, scores, a[s])

        qb = (q * jnp.exp2(w[GLA_LEVELS])).astype(BF16)
        st_b = st.astype(BF16)
        outs = []
        for s in range(n_slab):
            sl = slice(s * LANES, (s + 1) * LANES)
            v_ab = v[:, (2 * s) * GLA_DV:(2 * s + 2) * GLA_DV]
            first = lax.broadcasted_iota(jnp.int32, v_ab.shape, 1) < GLA_DV
            zero = jnp.zeros_like(v_ab)
            v_bd = jnp.concatenate([jnp.where(first, v_ab, zero), jnp.where(first, zero, v_ab)], axis=0)
            outs.append(jnp.dot(a[s].astype(BF16), v_bd, preferred_element_type=F32)
                        + jnp.dot(qb[:, sl], block_diag_t(st_b[:, sl]), preferred_element_type=F32))
        o = jnp.concatenate(outs, axis=1)

    tot = t[GLA_LEVELS]
    ku = (k * jnp.exp2(tot - w[GLA_LEVELS])).astype(BF16)
    decayed = st * jnp.exp2(tot[0:1, :])
    lo = lax.broadcasted_iota(jnp.int32, (GLA_DV, LANES), 1) < GLA_DK
    new_st = []
    for s in range(n_slab):
        sl = slice(s * LANES, (s + 1) * LANES)
        upd = _tn_dot(v[:, (2 * s) * GLA_DV:(2 * s + 2) * GLA_DV], ku[:, sl])
        new_st.append(decayed[:, sl] + jnp.where(lo, upd[:GLA_DV], upd[GLA_DV:]))
    return o, jnp.concatenate(new_st, axis=1)


def _gla_gate(o, r, nw):
    out = []
    for h in range(GLA_HEADS):
        sl = slice(h * GLA_DV, (h + 1) * GLA_DV)
        rh = r[:, sl].astype(F32)
        out.append((_rms(o[:, sl], nw) * (rh * jax.nn.sigmoid(rh))).astype(BF16))
    return jnp.concatenate(out, axis=1)


def _gla_kernel(q_ref, k_ref, v_ref, r_ref, g_ref, km_ref, vm_ref, gm_ref, nw_ref, o_ref, st_ref):
    @pl.when(pl.program_id(1) == 0)
    def _():
        st_ref[...] = jnp.zeros(st_ref.shape, F32)
        _, st0 = _gla_chunk(None, km_ref[...].astype(F32), vm_ref[...], gm_ref[...], st_ref[...], None)
        st_ref[...] = st0

    nw = nw_ref[...]
    level = _gla_levels()

    def body(c, carry):
        st = st_ref[...]
        for u in range(GLA_UNROLL):
            rows = pl.ds(pl.multiple_of((c * GLA_UNROLL + u) * GLA_CHUNK, GLA_CHUNK), GLA_CHUNK)
            o, st = _gla_chunk(q_ref[0, rows, :].astype(F32), k_ref[0, rows, :].astype(F32),
                               v_ref[0, rows, :], g_ref[0, rows, :], st, level)
            o_ref[0, rows, :] = _gla_gate(o, r_ref[0, rows, :], nw)
        st_ref[...] = st
        return carry

    lax.fori_loop(0, q_ref.shape[1] // (GLA_CHUNK * GLA_UNROLL), body, 0)


def _gla_mixer(gla_in, g, km, vm, gm, gla_norm_w, rows):
    b, s, _ = gla_in.shape

    def seq_spec(width, lane_block=0):
        return pl.BlockSpec((1, rows, width), lambda i, j: (i, j, lane_block))

    def full(a):
        return pl.BlockSpec(a.shape, lambda i, j: (0,) * a.ndim)

    return pl.pallas_call(
        _gla_kernel,
        grid=(b, s // rows),
        in_specs=[seq_spec(GQ, 0), seq_spec(GQ, 1), seq_spec(GV, 2 * GQ // GV), seq_spec(GV, 2 * GQ // GV + 1),
                  seq_spec(GQ), full(km), full(vm), full(gm), full(gla_norm_w)],
        out_specs=seq_spec(GV),
        out_shape=jax.ShapeDtypeStruct((b, s, GV), BF16),
        scratch_shapes=[pltpu.VMEM((GLA_DV, GQ), F32)],
        compiler_params=pltpu.CompilerParams(dimension_semantics=("arbitrary", "arbitrary"),
                                             vmem_limit_bytes=VMEM_LIMIT),
        name="gla_mixer",
    )(gla_in, gla_in, gla_in, gla_in, g, km, vm, gm, gla_norm_w)


def _swa_block(sink_ref, q, kc, kp, vc, vp, km, vm, has_prev):
    n_win = SWA_BLOCK + SWA_HALF
    n_keys = n_win + N_META
    r = lax.broadcasted_iota(jnp.int32, (SWA_HALF, n_keys), 0)
    c = lax.broadcasted_iota(jnp.int32, (SWA_HALF, n_keys), 1)
    in_band = (c > r) & (c <= r + SWA_BLOCK)
    lane = lax.broadcasted_iota(jnp.int32, (SWA_HALF, LANES), 1)
    low = lane < SWA_HD

    halves = []
    for half in range(2):
        rows = slice(half * SWA_HALF, (half + 1) * SWA_HALF)
        if half == 0:
            k_all = jnp.concatenate([kp, kc[:SWA_HALF], km], axis=0)
            v_all = jnp.concatenate([vp, vc[:SWA_HALF], vm], axis=0)
            n_prev = SWA_BLOCK
        else:
            k_all = jnp.concatenate([kp[SWA_HALF:], kc, km], axis=0)
            v_all = jnp.concatenate([vp[SWA_HALF:], vc, vm], axis=0)
            n_prev = SWA_HALF
        valid = (c >= n_win) | in_band
        if has_prev is not None:
            valid = (c >= n_win) | (in_band & (has_prev | (c >= n_prev)))
        bias = jnp.where(valid, 0.0, -jnp.inf)
        pieces = []
        for t in range(SWA_HEADS // 2):
            qt = q[rows, t * LANES:(t + 1) * LANES]
            pieces += [jnp.where(low, qt, jnp.zeros_like(qt)), jnp.where(low, jnp.zeros_like(qt), qt)]
        q_all = jnp.concatenate(pieces, axis=0)
        s_all = _nt_dot(q_all, k_all)
        p, denom = [], []
        for i in range(SWA_HEADS):
            s = s_all[i * SWA_HALF:(i + 1) * SWA_HALF] + bias
            sink = sink_ref[(i // 2) + (i % 2) * (SWA_HEADS // 2)] * LOG2_E
            m = jnp.maximum(jnp.max(s, axis=-1, keepdims=True), sink)
            p.append(jnp.exp2(s - m))
            denom.append(jnp.sum(p[i], axis=-1, keepdims=True) + jnp.exp2(sink - m))
        o = jnp.dot(jnp.concatenate(p, axis=0).astype(BF16), v_all, preferred_element_type=F32)
        slabs = []
        for t in range(SWA_HEADS // 2):
            lo = o[(2 * t) * SWA_HALF:(2 * t + 1) * SWA_HALF] / denom[2 * t]
            hi = o[(2 * t + 1) * SWA_HALF:(2 * t + 2) * SWA_HALF] / denom[2 * t + 1]
            slabs.append(jnp.where(low, lo, hi).astype(BF16))
        halves.append(jnp.concatenate(slabs, axis=1))
    return jnp.concatenate(halves, axis=0)


def _swa_ffn_kernel(tiles_per_seq, sink_ref, sq0_ref, sk0_ref, sv0_ref, sq_ref, sk_ref, sv_ref, skp_ref, svp_ref,
                    km_ref, vm_ref, x_ref, og_ref, wo_ref, nf_ref, w1_ref, w2_ref, nl_ref, y_ref, os_ref):
    s = pl.program_id(0)
    ahead = jnp.minimum(s + 1, pl.num_programs(0) - 1)
    seq_start = lax.rem(ahead, tiles_per_seq) == 0
    cur = lax.rem(s, 2)

    def swa_tile_block(j, slot, q_ref, k_ref, v_ref, first_prev):
        rows = slice(j * SWA_BLOCK, (j + 1) * SWA_BLOCK)
        if j == 0:
            kp, vp, has_prev = first_prev
        else:
            before = slice((j - 1) * SWA_BLOCK, j * SWA_BLOCK)
            kp, vp, has_prev = k_ref[before, :], v_ref[before, :], None
        os_ref[slot, rows, :] = _swa_block(sink_ref, q_ref[rows, :], k_ref[rows, :], kp, v_ref[rows, :], vp,
                                           km_ref[...], vm_ref[...], has_prev)

    n_blocks = x_ref.shape[0] // SWA_BLOCK

    @pl.when(s == 0)
    def _():
        for j in range(n_blocks):
            swa_tile_block(j, 0, sq0_ref, sk0_ref, sv0_ref, (sk0_ref[0:SWA_BLOCK, :], sv0_ref[0:SWA_BLOCK, :], s > 0))

    n_ff = D_FF // FF_CHUNK
    swa_per_ff = n_blocks // n_ff

    h = (x_ref[...]
         + jnp.dot(og_ref[...], wo_ref[0:GV, :], preferred_element_type=F32)
         + jnp.dot(os_ref[cur], wo_ref[GV:GV + SQ, :], preferred_element_type=F32))
    f = _rms(h, nf_ref[...]).astype(BF16)
    ff = None
    for c in range(n_ff):
        for j in range(c * swa_per_ff, (c + 1) * swa_per_ff):
            swa_tile_block(j, 1 - cur, sq_ref, sk_ref, sv_ref,
                           (skp_ref[...], svp_ref[...], jnp.logical_not(seq_start)))
        sl = slice(c * FF_CHUNK, (c + 1) * FF_CHUNK)
        a = jnp.maximum(jnp.dot(f, w1_ref[:, sl], preferred_element_type=F32), 0.0)
        d = jnp.dot((a * a).astype(BF16), w2_ref[sl, :], preferred_element_type=F32)
        ff = d if ff is None else ff + d
    y_ref[...] = _rms(h + ff, nl_ref[...])


def _swa_ffn(sinks, swa_in, km, vm, x2, og, wo, nf, w1, w2, nl, rows, tiles_per_seq):
    n = x2.shape[0]
    n_tiles = n // rows
    prev_per_tile = rows // SWA_BLOCK
    lane_block = {SQ: 0, SKV: SQ // SKV}

    def first_spec(width, k=0):
        return pl.BlockSpec((rows, width), lambda i: (0, lane_block[width] + k), pipeline_mode=pl.Buffered(1))

    def ahead_spec(width, k=0):
        return pl.BlockSpec((rows, width), lambda i: (jnp.minimum(i + 1, n_tiles - 1), lane_block[width] + k))

    def prev_spec(width, k=0):
        return pl.BlockSpec((SWA_BLOCK, width),
                            lambda i: (jnp.minimum(i + 1, n_tiles - 1) * prev_per_tile - 1, lane_block[width] + k))

    def ffn_spec(width):
        return pl.BlockSpec((rows, width), lambda i: (i, 0))

    def full(a):
        return pl.BlockSpec(a.shape, lambda i: (0,) * a.ndim, pipeline_mode=pl.Buffered(1))

    weights = (wo, nf, w1, w2, nl)
    return pl.pallas_call(
        functools.partial(_swa_ffn_kernel, tiles_per_seq),
        grid=(n_tiles,),
        in_specs=([pl.BlockSpec(memory_space=pltpu.SMEM), first_spec(SQ), first_spec(SKV), first_spec(SKV, 1),
                   ahead_spec(SQ), ahead_spec(SKV), ahead_spec(SKV, 1), prev_spec(SKV), prev_spec(SKV, 1),
                   full(km), full(vm), ffn_spec(D_MODEL), ffn_spec(GV)] + [full(a) for a in weights]),
        out_specs=ffn_spec(D_MODEL),
        out_shape=jax.ShapeDtypeStruct((n, D_MODEL), F32),
        scratch_shapes=[pltpu.VMEM((2, rows, SQ), BF16)],
        compiler_params=pltpu.CompilerParams(dimension_semantics=("arbitrary",),
                                             vmem_limit_bytes=VMEM_LIMIT),
        name="swa_ffn",
    )(sinks, *([swa_in] * 8), km, vm, x2, og, *weights)


def _rope_table(first_pos, n_pos):
    half = ROPE_DIM // 2
    inv_freq = 1.0 / (ROPE_THETA ** (jnp.arange(0, ROPE_DIM, 2, dtype=F32) / ROPE_DIM))
    ang = jnp.arange(first_pos, first_pos + n_pos, dtype=jnp.int32).astype(F32)[:, None] * inv_freq[None, :]
    cs = jnp.concatenate([jnp.cos(ang), jnp.sin(ang)], axis=1)
    lane = jnp.arange(3 * LANES)
    d = lane % SWA_HD
    which = lane // LANES
    src = jnp.where(which == 0, d % half, half + d % half)
    live = jnp.where(which == 0, d < ROPE_DIM, jnp.where(which == 1, d < half, (d >= half) & (d < ROPE_DIM)))
    sign = jnp.where(which == 1, -1.0, 1.0)
    spread = jnp.where(live[None, :] & (jnp.arange(2 * half)[:, None] == src[None, :]), sign[None, :], 0.0)
    unrotated = jnp.where((which == 0) & (d >= ROPE_DIM), 1.0, 0.0)
    return jnp.dot(cs, spread.astype(F32), precision=lax.Precision.HIGHEST) + unrotated[None, :].astype(F32)


def _swa_head_order():
    order = []
    for t in range(SWA_HEADS // 2):
        order += [t, t + SWA_HEADS // 2]
    return order


def kernel(x, meta_tokens, norm_mix_w, w_in, w_gate_up, b_gate, gla_norm_w, sinks, w_out, norm_ff_w,
           w_ff1, w_ff2, final_norm_w):
    b, s, d = x.shape
    assert d == D_MODEL and s % ROWS_IN == 0 and s % ROWS_OUT == 0 and w_in.shape[0] == 1
    assert s % ROWS_GLA == 0 and ROWS_GLA % (GLA_CHUNK * GLA_UNROLL) == 0
    assert ROWS_OUT % (SWA_BLOCK * (D_FF // FF_CHUNK)) == 0

    wg = jnp.pad(w_gate_up[0], ((0, LANES - GLA_RANK), (0, 0))).astype(BF16)
    bg = b_gate[0].reshape(1, GQ)
    nw = norm_mix_w[0].reshape(1, D_MODEL)
    cq = jnp.full((1, LANES), SWA_HD ** -0.5 * LOG2_E, F32)
    small = (nw, wg, bg, cq)

    meta_tables = _rope_table(0, N_META)
    tok_tables = _rope_table(N_META, s)

    w_all, m_gla, m_g, m_swa = _meta_prep(meta_tokens.astype(F32), w_in[0].T, small, meta_tables)
    front = ((GLA_CHUNK - N_META, 0), (0, 0))
    km = jnp.pad(m_gla[:, GQ:2 * GQ], front)
    vm = jnp.pad(m_gla[:, 2 * GQ:2 * GQ + GV], front)
    gm = jnp.pad(m_g, front)
    m_sk, m_sv = m_swa[:, SQ:SQ + SKV], m_swa[:, SQ + SKV:]

    x2 = x.reshape(b * s, d)
    gla_in, g, swa_in, w1, w2, wo = _in_projection(x2, tok_tables, w_all, small,
                                                   w_ff1[0], w_ff2[0], w_out[0], ROWS_IN)

    def seq(a):
        return a.reshape(b, s, a.shape[-1])

    o_gla = _gla_mixer(seq(gla_in), seq(g), km, vm, gm, gla_norm_w[0].reshape(1, GLA_DV), ROWS_GLA)
    y = _swa_ffn(sinks[0].astype(F32), swa_in, m_sk, m_sv, x2, o_gla.reshape(b * s, GV), wo,
                 norm_ff_w[0].reshape(1, D_MODEL), w1, w2, final_norm_w.reshape(1, D_MODEL),
                 ROWS_OUT, s // ROWS_OUT)
    return y.reshape(b, s, d)
```

```python
import functools

import jax
import jax.numpy as jnp
from jax import lax
from jax.experimental import pallas as pl
from jax.experimental.pallas import tpu as pltpu

F32 = jnp.float32
BF16 = jnp.bfloat16

D_MODEL = 1024
N_META = 16
GLA_HEADS = 4
GLA_DK = 64
GLA_DV = 128
GLA_RANK = 16
GLA_TAU = 16.0
GLA_CHUNK = 64
GLA_LEVELS = 6
SWA_HEADS = 8
SWA_KV_HEADS = 2
SWA_HD = 64
SWA_BLOCK = 128
SWA_HALF = SWA_BLOCK // 2
ROPE_DIM = 16
ROPE_THETA = 500000.0
D_FF = 4096
EPS = 1e-5
LOG2_E = 1.4426950408889634

LANES = 128
GQ = GLA_HEADS * GLA_DK
GV = GLA_HEADS * GLA_DV
SQ = SWA_HEADS * SWA_HD
SKV = SWA_KV_HEADS * SWA_HD
FF_CHUNK = 1024

ROWS_IN = 1024
ROWS_IN_SUB = 256
W_SQ = 2 * GQ + 2 * GV
W_SK = W_SQ + SQ
W_LR = W_SK + 2 * SKV
ROWS_GLA = 512
ROWS_OUT = 512
VMEM_LIMIT = 56 * 1024 * 1024


def _nt_dot(a, b):
    return lax.dot_general(a, b, (((1,), (1,)), ((), ())), preferred_element_type=F32)


def _tn_dot(a, b):
    return lax.dot_general(a, b, (((0,), (0,)), ((), ())), preferred_element_type=F32)


def _rms(x, w):
    return x * lax.rsqrt(jnp.mean(x * x, axis=-1, keepdims=True) + EPS) * w


def _project(normed, n_rows, w_ref, wg_ref, bg_ref, cq_ref, cs_ref, spread_ref, unrot_ref, gla_ref, g_ref, swa_ref,
             after=None):
    half = ROPE_DIM // 2
    n_sub = n_rows // ROWS_IN_SUB if n_rows > ROWS_IN_SUB else 1
    sub = n_rows // n_sub
    for i in range(n_sub):
        rows = slice(i * sub, (i + 1) * sub)
        u = normed(rows)

        def proj(lo, width):
            return jnp.dot(u, w_ref[:, lo:lo + width], preferred_element_type=F32)


        cs = cs_ref[rows, :]
        hi = cs.astype(BF16)
        lo = (cs - hi.astype(F32)).astype(BF16)
        pat = jnp.dot(jnp.concatenate([hi, lo], axis=1), spread_ref[...], preferred_element_type=F32) + unrot_ref[...]
        cos, s_up, s_dn = pat[:, 0:LANES], pat[:, LANES:2 * LANES], pat[:, 2 * LANES:]

        def rope(t):
            return t * cos + pltpu.roll(t, LANES - half, axis=1) * s_up + pltpu.roll(t, half, axis=1) * s_dn

        sq = proj(W_SQ, SQ)
        for s in range(SQ // LANES):
            sl = slice(s * LANES, (s + 1) * LANES)
            swa_ref[rows, sl] = (rope(sq[:, sl]) * cq_ref[...]).astype(BF16)
        kv = proj(W_SK, 2 * SKV)
        swa_ref[rows, SQ:SQ + SKV] = rope(kv[:, :SKV]).astype(BF16)
        swa_ref[rows, SQ + SKV:SQ + 2 * SKV] = kv[:, SKV:].astype(BF16)

        lr = proj(W_LR, LANES).astype(BF16)
        z = jnp.dot(lr, wg_ref[...], preferred_element_type=F32) + bg_ref[...]
        g_ref[rows, :] = (jnp.minimum(z, 0.0) - jnp.log1p(jnp.exp(-jnp.abs(z)))) * (LOG2_E / GLA_TAU)

        qk = proj(0, 2 * GQ)
        gla_ref[rows, 0:GQ] = (qk[:, :GQ] * (GLA_DK ** -0.5)).astype(BF16)
        gla_ref[rows, GQ:2 * GQ] = qk[:, GQ:].astype(BF16)
        gla_ref[rows, 2 * GQ:2 * GQ + GV] = proj(2 * GQ, GV).astype(BF16)
        gla_ref[rows, 2 * GQ + GV:2 * GQ + 2 * GV] = proj(2 * GQ + GV, GV).astype(BF16)
        if after is not None:
            after(rows)


PROJ_WIDTHS = (2 * GQ + 2 * GV, GQ, SQ + 2 * SKV)
PROJ_DTYPES = (BF16, F32, BF16)


def _meta_prep_kernel(x_ref, wt_ref, nw_ref, wg_ref, bg_ref, cq_ref, cs_ref, spread_ref, unrot_ref, w_ref, *out_refs):
    o_lr = 2 * GQ + 2 * GV
    o_sq = o_lr + GLA_RANK
    o_sk = o_sq + SQ
    half = SWA_HEADS // 2

    def rows(lo, n):
        return wt_ref[lo:lo + n, :].astype(BF16)

    slabs = [rows(c * LANES, LANES) for c in range(o_lr // LANES)]
    slabs += [jnp.concatenate([rows(o_sq + t * SWA_HD, SWA_HD), rows(o_sq + (t + half) * SWA_HD, SWA_HD)], axis=0)
              for t in range(half)]
    slabs += [rows(o_sk + c * LANES, LANES) for c in range(2 * SKV // LANES)]
    slabs += [jnp.concatenate([rows(o_lr, GLA_RANK), jnp.zeros((LANES - GLA_RANK, D_MODEL), BF16)], axis=0)]
    for c, slab in enumerate(slabs):
        w_ref[:, c * LANES:(c + 1) * LANES] = slab.T
    _project(lambda rows: _rms(x_ref[rows, :], nw_ref[...]).astype(BF16), x_ref.shape[0],
             w_ref, wg_ref, bg_ref, cq_ref, cs_ref, spread_ref, unrot_ref, *out_refs)


def _meta_prep(meta_tokens, w_in_t, small, rope):
    n = meta_tokens.shape[0]
    return pl.pallas_call(
        _meta_prep_kernel,
        out_shape=([jax.ShapeDtypeStruct((D_MODEL, W_LR + LANES), BF16)]
                   + [jax.ShapeDtypeStruct((n, w), d) for w, d in zip(PROJ_WIDTHS, PROJ_DTYPES)]),
        compiler_params=pltpu.CompilerParams(vmem_limit_bytes=VMEM_LIMIT),
        name="meta_prep",
    )(meta_tokens, w_in_t, *small, *rope)


def _inproj_kernel(x0_ref, xa_ref, nw_ref, w_ref, wg_ref, bg_ref, cq_ref, cs_ref, spread_ref, unrot_ref,
                   w1_ref, w2_ref, wo_ref, *refs):
    *proj_refs, w1b_ref, w2b_ref, wob_ref, u_ref = refs

    @pl.when(pl.program_id(0) == 0)
    def _():
        u_ref[...] = _rms(x0_ref[...], nw_ref[...]).astype(BF16)

    def norm_ahead(rows):
        u_ref[rows, :] = _rms(xa_ref[rows, :], nw_ref[...]).astype(BF16)

    _project(lambda rows: u_ref[rows, :], xa_ref.shape[0], w_ref, wg_ref, bg_ref, cq_ref,
             cs_ref, spread_ref, unrot_ref, *proj_refs, after=norm_ahead)
    w1b_ref[...] = w1_ref[...].astype(BF16)
    w2b_ref[...] = w2_ref[...].astype(BF16)
    wob_ref[...] = wo_ref[...].astype(BF16)


def _in_projection(x2, rope, w_all, small, w1, w2, wo, rows):
    n = x2.shape[0]
    steps = n // rows
    cs, spread, unrot = rope
    nblk_pos = cs.shape[0] // rows
    assert w1.shape[0] % steps == 0 and w2.shape[0] % steps == 0 and wo.shape[0] == steps * SWA_HD

    def row_spec(width):
        return pl.BlockSpec((rows, width), lambda i: (i, 0))

    def full(a):
        return pl.BlockSpec(a.shape, lambda i: (0,) * a.ndim)

    def slice_spec(a):
        return pl.BlockSpec((a.shape[0] // steps, a.shape[1]), lambda i: (i, 0))

    n_gla = GV // SWA_HD
    half = SWA_HEADS // 2

    def wo_src(i):
        j = i - n_gla
        return (jnp.where(j < 0, i, n_gla + j // 2 + half * (j % 2)), 0)

    x_first = pl.BlockSpec((rows, D_MODEL), lambda i: (0, 0), pipeline_mode=pl.Buffered(1))
    x_ahead = pl.BlockSpec((rows, D_MODEL), lambda i: (jnp.minimum(i + 1, steps - 1), 0))
    in_specs = ([x_first, x_ahead, full(small[0]), full(w_all)] + [full(a) for a in small[1:]]
                + [pl.BlockSpec((rows, cs.shape[1]), lambda i: (i % nblk_pos, 0)), full(spread), full(unrot),
                   slice_spec(w1), slice_spec(w2), pl.BlockSpec((SWA_HD, wo.shape[1]), wo_src)])
    weights_out = [jax.ShapeDtypeStruct(a.shape, BF16) for a in (w1, w2, wo)]
    return pl.pallas_call(
        _inproj_kernel,
        grid=(steps,),
        in_specs=in_specs,
        out_specs=[row_spec(w) for w in PROJ_WIDTHS] + [slice_spec(w1), slice_spec(w2), slice_spec(wo)],
        out_shape=[jax.ShapeDtypeStruct((n, w), d) for w, d in zip(PROJ_WIDTHS, PROJ_DTYPES)] + weights_out,
        scratch_shapes=[pltpu.VMEM((rows, D_MODEL), BF16)],
        compiler_params=pltpu.CompilerParams(dimension_semantics=("arbitrary",),
                                             vmem_limit_bytes=VMEM_LIMIT),
        name="in_projection",
    )(x2, x2, small[0], w_all, *small[1:], cs, spread, unrot, w1, w2, wo)


def _block_cumsums(g):
    rows = g.shape[0]
    row = lax.broadcasted_iota(jnp.int32, g.shape, 0)
    w, t = [g], [g]
    for k in range(GLA_LEVELS):
        s = 1 << k
        upper = (row & s) != 0
        below = pltpu.roll(t[k], s, axis=0)
        above = pltpu.roll(t[k], rows - s, axis=0)
        w.append(w[k] + jnp.where(upper, below, 0.0))
        t.append(t[k] + jnp.where(upper, below, above))
    return w, t


def _gla_levels():
    ri = lax.broadcasted_iota(jnp.int32, (GLA_CHUNK, LANES), 0)
    ci = lax.broadcasted_iota(jnp.int32, (GLA_CHUNK, LANES), 1) & (GLA_CHUNK - 1)
    x = ri ^ ci
    level = jnp.zeros_like(x)
    for b in range(GLA_LEVELS):
        level = level + (x >= (1 << b)).astype(jnp.int32)
    return jnp.where(ci > ri, -1, level)


def _gla_chunk(q, k, v, g, st, level):
    w, t = _block_cumsums(g)
    n_slab = GQ // LANES

    def block_diag_t(m):
        r = m.shape[0]
        m2 = jnp.concatenate([m, m], axis=0).T
        same_head = ((lax.broadcasted_iota(jnp.int32, m2.shape, 0) < GLA_DK)
                     == (lax.broadcasted_iota(jnp.int32, m2.shape, 1) < r))
        return jnp.where(same_head, m2, jnp.zeros_like(m2))

    o = None
    if level is not None:
        a = [jnp.zeros((GLA_CHUNK, LANES), F32) for _ in range(n_slab)]
        for lv in range(GLA_LEVELS + 1):
            if lv == 0:
                qt, kt = q.astype(BF16), k.astype(BF16)
            else:
                qt = (q * jnp.exp2(w[lv - 1])).astype(BF16)
                kt = (k * jnp.exp2(t[lv - 1] - w[lv - 1])).astype(BF16)
            for s in range(n_slab):
                sl = slice(s * LANES, (s + 1) * LANES)
                scores = jnp.dot(qt[:, sl], block_diag_t(kt[:, sl]), preferred_element_type=F32)
                a[s] = jnp.where(level == lv, scores, a[s])

        qb = (q * jnp.exp2(w[GLA_LEVELS])).astype(BF16)
        st_b = st.astype(BF16)
        outs = []
        for s in range(n_slab):
            sl = slice(s * LANES, (s + 1) * LANES)
            v_ab = v[:, (2 * s) * GLA_DV:(2 * s + 2) * GLA_DV]
            first = lax.broadcasted_iota(jnp.int32, v_ab.shape, 1) < GLA_DV
            zero = jnp.zeros_like(v_ab)
            v_bd = jnp.concatenate([jnp.where(first, v_ab, zero), jnp.where(first, zero, v_ab)], axis=0)
            outs.append(jnp.dot(a[s].astype(BF16), v_bd, preferred_element_type=F32)
                        + jnp.dot(qb[:, sl], block_diag_t(st_b[:, sl]), preferred_element_type=F32))
        o = jnp.concatenate(outs, axis=1)

    tot = t[GLA_LEVELS]
    ku = (k * jnp.exp2(tot - w[GLA_LEVELS])).astype(BF16)
    decayed = st * jnp.exp2(tot[0:1, :])
    lo = lax.broadcasted_iota(jnp.int32, (GLA_DV, LANES), 1) < GLA_DK
    new_st = []
    for s in range(n_slab):
        sl = slice(s * LANES, (s + 1) * LANES)
        upd = _tn_dot(v[:, (2 * s) * GLA_DV:(2 * s + 2) * GLA_DV], ku[:, sl])
        new_st.append(decayed[:, sl] + jnp.where(lo, upd[:GLA_DV], upd[GLA_DV:]))
    return o, jnp.concatenate(new_st, axis=1)


def _gla_gate(o, r, nw):
    out = []
    for h in range(GLA_HEADS):
        sl = slice(h * GLA_DV, (h + 1) * GLA_DV)
        rh = r[:, sl].astype(F32)
        out.append((_rms(o[:, sl], nw) * (rh * jax.nn.sigmoid(rh))).astype(BF16))
    return jnp.concatenate(out, axis=1)


def _gla_kernel(q_ref, k_ref, v_ref, r_ref, g_ref, km_ref, vm_ref, gm_ref, nw_ref, o_ref, st_ref):
    @pl.when(pl.program_id(1) == 0)
    def _():
        st_ref[...] = jnp.zeros(st_ref.shape, F32)
        _, st0 = _gla_chunk(None, km_ref[...].astype(F32), vm_ref[...], gm_ref[...], st_ref[...], None)
        st_ref[...] = st0

    nw = nw_ref[...]
    level = _gla_levels()
    st = st_ref[...]
    for c in range(q_ref.shape[1] // GLA_CHUNK):
        rows = slice(c * GLA_CHUNK, (c + 1) * GLA_CHUNK)
        o, st = _gla_chunk(q_ref[0, rows, :].astype(F32), k_ref[0, rows, :].astype(F32),
                           v_ref[0, rows, :], g_ref[0, rows, :], st, level)
        o_ref[0, rows, :] = _gla_gate(o, r_ref[0, rows, :], nw)
    st_ref[...] = st


def _gla_mixer(gla_in, g, km, vm, gm, gla_norm_w, rows):
    b, s, _ = gla_in.shape

    def seq_spec(width, lane_block=0):
        return pl.BlockSpec((1, rows, width), lambda i, j: (i, j, lane_block))

    def full(a):
        return pl.BlockSpec(a.shape, lambda i, j: (0,) * a.ndim)

    return pl.pallas_call(
        _gla_kernel,
        grid=(b, s // rows),
        in_specs=[seq_spec(GQ, 0), seq_spec(GQ, 1), seq_spec(GV, 2 * GQ // GV), seq_spec(GV, 2 * GQ // GV + 1),
                  seq_spec(GQ), full(km), full(vm), full(gm), full(gla_norm_w)],
        out_specs=seq_spec(GV),
        out_shape=jax.ShapeDtypeStruct((b, s, GV), BF16),
        scratch_shapes=[pltpu.VMEM((GLA_DV, GQ), F32)],
        compiler_params=pltpu.CompilerParams(dimension_semantics=("arbitrary", "arbitrary"),
                                             vmem_limit_bytes=VMEM_LIMIT),
        name="gla_mixer",
    )(gla_in, gla_in, gla_in, gla_in, g, km, vm, gm, gla_norm_w)


def _swa_block(sink_ref, q, kc, kp, vc, vp, km, vm, has_prev):
    n_win = SWA_BLOCK + SWA_HALF
    n_keys = n_win + N_META
    r = lax.broadcasted_iota(jnp.int32, (SWA_HALF, n_keys), 0)
    c = lax.broadcasted_iota(jnp.int32, (SWA_HALF, n_keys), 1)
    in_band = (c > r) & (c <= r + SWA_BLOCK)
    lane = lax.broadcasted_iota(jnp.int32, (SWA_HALF, LANES), 1)
    low = lane < SWA_HD

    halves = []
    for half in range(2):
        rows = slice(half * SWA_HALF, (half + 1) * SWA_HALF)
        if half == 0:
            k_all = jnp.concatenate([kp, kc[:SWA_HALF], km], axis=0)
            v_all = jnp.concatenate([vp, vc[:SWA_HALF], vm], axis=0)
            n_prev = SWA_BLOCK
        else:
            k_all = jnp.concatenate([kp[SWA_HALF:], kc, km], axis=0)
            v_all = jnp.concatenate([vp[SWA_HALF:], vc, vm], axis=0)
            n_prev = SWA_HALF
        valid = (c >= n_win) | in_band
        if has_prev is not None:
            valid = (c >= n_win) | (in_band & (has_prev | (c >= n_prev)))
        bias = jnp.where(valid, 0.0, -jnp.inf)
        pieces = []
        for t in range(SWA_HEADS // 2):
            qt = q[rows, t * LANES:(t + 1) * LANES]
            pieces += [jnp.where(low, qt, jnp.zeros_like(qt)), jnp.where(low, jnp.zeros_like(qt), qt)]
        q_all = jnp.concatenate(pieces, axis=0)
        s_all = _nt_dot(q_all, k_all)
        p, denom = [], []
        for i in range(SWA_HEADS):
            s = s_all[i * SWA_HALF:(i + 1) * SWA_HALF] + bias
            sink = sink_ref[(i // 2) + (i % 2) * (SWA_HEADS // 2)] * LOG2_E
            m = jnp.maximum(jnp.max(s, axis=-1, keepdims=True), sink)
            p.append(jnp.exp2(s - m))
            denom.append(jnp.sum(p[i], axis=-1, keepdims=True) + jnp.exp2(sink - m))
        o = jnp.dot(jnp.concatenate(p, axis=0).astype(BF16), v_all, preferred_element_type=F32)
        slabs = []
        for t in range(SWA_HEADS // 2):
            lo = o[(2 * t) * SWA_HALF:(2 * t + 1) * SWA_HALF] / denom[2 * t]
            hi = o[(2 * t + 1) * SWA_HALF:(2 * t + 2) * SWA_HALF] / denom[2 * t + 1]
            slabs.append(jnp.where(low, lo, hi).astype(BF16))
        halves.append(jnp.concatenate(slabs, axis=1))
    return jnp.concatenate(halves, axis=0)


def _swa_ffn_kernel(tiles_per_seq, sink_ref, sq0_ref, sk0_ref, sv0_ref, sq_ref, sk_ref, sv_ref, skp_ref, svp_ref,
                    km_ref, vm_ref, x_ref, og_ref, wo_ref, nf_ref, w1_ref, w2_ref, nl_ref, y_ref, os_ref):
    s = pl.program_id(0)
    ahead = jnp.minimum(s + 1, pl.num_programs(0) - 1)
    seq_start = lax.rem(ahead, tiles_per_seq) == 0
    cur = lax.rem(s, 2)

    def swa_tile_block(j, slot, q_ref, k_ref, v_ref, first_prev):
        rows = slice(j * SWA_BLOCK, (j + 1) * SWA_BLOCK)
        if j == 0:
            kp, vp, has_prev = first_prev
        else:
            before = slice((j - 1) * SWA_BLOCK, j * SWA_BLOCK)
            kp, vp, has_prev = k_ref[before, :], v_ref[before, :], None
        os_ref[slot, rows, :] = _swa_block(sink_ref, q_ref[rows, :], k_ref[rows, :], kp, v_ref[rows, :], vp,
                                           km_ref[...], vm_ref[...], has_prev)

    n_blocks = x_ref.shape[0] // SWA_BLOCK

    @pl.when(s == 0)
    def _():
        for j in range(n_blocks):
            swa_tile_block(j, 0, sq0_ref, sk0_ref, sv0_ref, (sk0_ref[0:SWA_BLOCK, :], sv0_ref[0:SWA_BLOCK, :], s > 0))

    n_ff = D_FF // FF_CHUNK
    swa_per_ff = n_blocks // n_ff

    h = (x_ref[...]
         + jnp.dot(og_ref[...], wo_ref[0:GV, :], preferred_element_type=F32)
         + jnp.dot(os_ref[cur], wo_ref[GV:GV + SQ, :], preferred_element_type=F32))
    f = _rms(h, nf_ref[...]).astype(BF16)
    ff = None
    for c in range(n_ff):
        for j in range(c * swa_per_ff, (c + 1) * swa_per_ff):
            swa_tile_block(j, 1 - cur, sq_ref, sk_ref, sv_ref,
                           (skp_ref[...], svp_ref[...], jnp.logical_not(seq_start)))
        sl = slice(c * FF_CHUNK, (c + 1) * FF_CHUNK)
        a = jnp.maximum(jnp.dot(f, w1_ref[:, sl], preferred_element_type=F32), 0.0)
        d = jnp.dot((a * a).astype(BF16), w2_ref[sl, :], preferred_element_type=F32)
        ff = d if ff is None else ff + d
    y_ref[...] = _rms(h + ff, nl_ref[...])


def _swa_ffn(sinks, swa_in, km, vm, x2, og, wo, nf, w1, w2, nl, rows, tiles_per_seq):
    n = x2.shape[0]
    n_tiles = n // rows
    prev_per_tile = rows // SWA_BLOCK
    lane_block = {SQ: 0, SKV: SQ // SKV}

    def first_spec(width, k=0):
        return pl.BlockSpec((rows, width), lambda i: (0, lane_block[width] + k), pipeline_mode=pl.Buffered(1))

    def ahead_spec(width, k=0):
        return pl.BlockSpec((rows, width), lambda i: (jnp.minimum(i + 1, n_tiles - 1), lane_block[width] + k))

    def prev_spec(width, k=0):
        return pl.BlockSpec((SWA_BLOCK, width),
                            lambda i: (jnp.minimum(i + 1, n_tiles - 1) * prev_per_tile - 1, lane_block[width] + k))

    def ffn_spec(width):
        return pl.BlockSpec((rows, width), lambda i: (i, 0))

    def full(a):
        return pl.BlockSpec(a.shape, lambda i: (0,) * a.ndim, pipeline_mode=pl.Buffered(1))

    weights = (wo, nf, w1, w2, nl)
    return pl.pallas_call(
        functools.partial(_swa_ffn_kernel, tiles_per_seq),
        grid=(n_tiles,),
        in_specs=([pl.BlockSpec(memory_space=pltpu.SMEM), first_spec(SQ), first_spec(SKV), first_spec(SKV, 1),
                   ahead_spec(SQ), ahead_spec(SKV), ahead_spec(SKV, 1), prev_spec(SKV), prev_spec(SKV, 1),
                   full(km), full(vm), ffn_spec(D_MODEL), ffn_spec(GV)] + [full(a) for a in weights]),
        out_specs=ffn_spec(D_MODEL),
        out_shape=jax.ShapeDtypeStruct((n, D_MODEL), F32),
        scratch_shapes=[pltpu.VMEM((2, rows, SQ), BF16)],
        compiler_params=pltpu.CompilerParams(dimension_semantics=("arbitrary",),
                                             vmem_limit_bytes=VMEM_LIMIT),
        name="swa_ffn",
    )(sinks, *([swa_in] * 8), km, vm, x2, og, *weights)


def _rope_cos_sin(first_pos, n_pos):
    inv_freq = 1.0 / (ROPE_THETA ** (jnp.arange(0, ROPE_DIM, 2, dtype=F32) / ROPE_DIM))
    ang = jnp.arange(first_pos, first_pos + n_pos, dtype=jnp.int32).astype(F32)[:, None] * inv_freq[None, :]
    return jnp.concatenate([jnp.cos(ang), jnp.sin(ang)], axis=1)


def _rope_spread():
    half = ROPE_DIM // 2
    lane = jnp.arange(3 * LANES)
    d = lane % SWA_HD
    which = lane // LANES
    src = jnp.where(which == 0, d % half, half + d % half)
    live = jnp.where(which == 0, d < ROPE_DIM, jnp.where(which == 1, d < half, (d >= half) & (d < ROPE_DIM)))
    sign = jnp.where(which == 1, -1.0, 1.0)
    spread = jnp.where(live[None, :] & (jnp.arange(2 * half)[:, None] == src[None, :]), sign[None, :], 0.0)
    unrotated = jnp.where((which == 0) & (d >= ROPE_DIM), 1.0, 0.0)
    return jnp.concatenate([spread, spread], axis=0).astype(BF16), unrotated[None, :].astype(F32)


def _swa_head_order():
    order = []
    for t in range(SWA_HEADS // 2):
        order += [t, t + SWA_HEADS // 2]
    return order


def kernel(x, meta_tokens, norm_mix_w, w_in, w_gate_up, b_gate, gla_norm_w, sinks, w_out, norm_ff_w,
           w_ff1, w_ff2, final_norm_w):
    b, s, d = x.shape
    assert d == D_MODEL and s % ROWS_IN == 0 and s % ROWS_OUT == 0 and w_in.shape[0] == 1
    assert s % ROWS_GLA == 0 and ROWS_GLA % GLA_CHUNK == 0
    assert ROWS_OUT % (SWA_BLOCK * (D_FF // FF_CHUNK)) == 0

    wg = jnp.pad(w_gate_up[0], ((0, LANES - GLA_RANK), (0, 0))).astype(BF16)
    bg = b_gate[0].reshape(1, GQ)
    nw = norm_mix_w[0].reshape(1, D_MODEL)
    cq = jnp.full((1, LANES), SWA_HD ** -0.5 * LOG2_E, F32)
    small = (nw, wg, bg, cq)

    spread, unrot = _rope_spread()
    meta_tables = (_rope_cos_sin(0, N_META), spread, unrot)
    tok_tables = (_rope_cos_sin(N_META, s), spread, unrot)

    w_all, m_gla, m_g, m_swa = _meta_prep(meta_tokens.astype(F32), w_in[0].T, small, meta_tables)
    front = ((GLA_CHUNK - N_META, 0), (0, 0))
    km = jnp.pad(m_gla[:, GQ:2 * GQ], front)
    vm = jnp.pad(m_gla[:, 2 * GQ:2 * GQ + GV], front)
    gm = jnp.pad(m_g, front)
    m_sk, m_sv = m_swa[:, SQ:SQ + SKV], m_swa[:, SQ + SKV:]

    x2 = x.reshape(b * s, d)
    gla_in, g, swa_in, w1, w2, wo = _in_projection(x2, tok_tables, w_all, small,
                                                   w_ff1[0], w_ff2[0], w_out[0], ROWS_IN)

    def seq(a):
        return a.reshape(b, s, a.shape[-1])

    o_gla = _gla_mixer(seq(gla_in), seq(g), km, vm, gm, gla_norm_w[0].reshape(1, GLA_DV), ROWS_GLA)
    y = _swa_ffn(sinks[0].astype(F32), swa_in, m_sk, m_sv, x2, o_gla.reshape(b * s, GV), wo,
                 norm_ff_w[0].reshape(1, D_MODEL), w1, w2, final_norm_w.reshape(1, D_MODEL),
                 ROWS_OUT, s // ROWS_OUT)
    return y.reshape(b, s, d)
```

```python
import functools

import jax
import jax.numpy as jnp
from jax import lax
from jax.experimental import pallas as pl
from jax.experimental.pallas import tpu as pltpu

F32 = jnp.float32
BF16 = jnp.bfloat16

D_MODEL = 1024
N_META = 16
GLA_HEADS = 4
GLA_DK = 64
GLA_DV = 128
GLA_RANK = 16
GLA_TAU = 16.0
GLA_CHUNK = 64
GLA_LEVELS = 6
SWA_HEADS = 8
SWA_KV_HEADS = 2
SWA_HD = 64
SWA_BLOCK = 128
SWA_HALF = SWA_BLOCK // 2
ROPE_DIM = 16
ROPE_THETA = 500000.0
D_FF = 4096
EPS = 1e-5
LOG2_E = 1.4426950408889634

LANES = 128
GQ = GLA_HEADS * GLA_DK
GV = GLA_HEADS * GLA_DV
SQ = SWA_HEADS * SWA_HD
SKV = SWA_KV_HEADS * SWA_HD
FF_CHUNK = 1024

ROWS_IN = 1024
ROWS_IN_SUB = 256
W_SQ = 2 * GQ + 2 * GV
W_SK = W_SQ + SQ
W_LR = W_SK + 2 * SKV
ROWS_GLA = 1024
ROWS_OUT = 512
VMEM_LIMIT = 56 * 1024 * 1024


def _nt_dot(a, b):
    return lax.dot_general(a, b, (((1,), (1,)), ((), ())), preferred_element_type=F32)


def _tn_dot(a, b):
    return lax.dot_general(a, b, (((0,), (0,)), ((), ())), preferred_element_type=F32)


def _rms(x, w):
    return x * lax.rsqrt(jnp.mean(x * x, axis=-1, keepdims=True) + EPS) * w


def _project(normed, n_rows, w_ref, wg_ref, bg_ref, cq_ref, cs_ref, spread_ref, unrot_ref, gla_ref, g_ref, swa_ref,
             after=None):
    half = ROPE_DIM // 2
    n_sub = n_rows // ROWS_IN_SUB if n_rows > ROWS_IN_SUB else 1
    sub = n_rows // n_sub
    for i in range(n_sub):
        rows = slice(i * sub, (i + 1) * sub)
        u = normed(rows)

        def proj(lo, width):
            return jnp.dot(u, w_ref[:, lo:lo + width], preferred_element_type=F32)


        cs = cs_ref[rows, :]
        hi = cs.astype(BF16)
        lo = (cs - hi.astype(F32)).astype(BF16)
        pat = jnp.dot(jnp.concatenate([hi, lo], axis=1), spread_ref[...], preferred_element_type=F32) + unrot_ref[...]
        cos, s_up, s_dn = pat[:, 0:LANES], pat[:, LANES:2 * LANES], pat[:, 2 * LANES:]

        def rope(t):
            return t * cos + pltpu.roll(t, LANES - half, axis=1) * s_up + pltpu.roll(t, half, axis=1) * s_dn

        sq = proj(W_SQ, SQ)
        for s in range(SQ // LANES):
            sl = slice(s * LANES, (s + 1) * LANES)
            swa_ref[rows, sl] = (rope(sq[:, sl]) * cq_ref[...]).astype(BF16)
        kv = proj(W_SK, 2 * SKV)
        swa_ref[rows, SQ:SQ + SKV] = rope(kv[:, :SKV]).astype(BF16)
        swa_ref[rows, SQ + SKV:SQ + 2 * SKV] = kv[:, SKV:].astype(BF16)

        lr = proj(W_LR, LANES).astype(BF16)
        z = jnp.dot(lr, wg_ref[...], preferred_element_type=F32) + bg_ref[...]
        g_ref[rows, :] = (jnp.minimum(z, 0.0) - jnp.log1p(jnp.exp(-jnp.abs(z)))) * (LOG2_E / GLA_TAU)

        qk = proj(0, 2 * GQ)
        gla_ref[rows, 0:GQ] = (qk[:, :GQ] * (GLA_DK ** -0.5)).astype(BF16)
        gla_ref[rows, GQ:2 * GQ] = qk[:, GQ:].astype(BF16)
        gla_ref[rows, 2 * GQ:2 * GQ + GV] = proj(2 * GQ, GV).astype(BF16)
        gla_ref[rows, 2 * GQ + GV:2 * GQ + 2 * GV] = proj(2 * GQ + GV, GV).astype(BF16)
        if after is not None:
            after(rows)


PROJ_WIDTHS = (2 * GQ + 2 * GV, GQ, SQ + 2 * SKV)
PROJ_DTYPES = (BF16, F32, BF16)


def _meta_prep_kernel(x_ref, wt_ref, nw_ref, wg_ref, bg_ref, cq_ref, cs_ref, spread_ref, unrot_ref, w_ref, *out_refs):
    o_lr = 2 * GQ + 2 * GV
    o_sq = o_lr + GLA_RANK
    o_sk = o_sq + SQ
    half = SWA_HEADS // 2

    def rows(lo, n):
        return wt_ref[lo:lo + n, :].astype(BF16)

    slabs = [rows(c * LANES, LANES) for c in range(o_lr // LANES)]
    slabs += [jnp.concatenate([rows(o_sq + t * SWA_HD, SWA_HD), rows(o_sq + (t + half) * SWA_HD, SWA_HD)], axis=0)
              for t in range(half)]
    slabs += [rows(o_sk + c * LANES, LANES) for c in range(2 * SKV // LANES)]
    slabs += [jnp.concatenate([rows(o_lr, GLA_RANK), jnp.zeros((LANES - GLA_RANK, D_MODEL), BF16)], axis=0)]
    for c, slab in enumerate(slabs):
        w_ref[:, c * LANES:(c + 1) * LANES] = slab.T
    _project(lambda rows: _rms(x_ref[rows, :], nw_ref[...]).astype(BF16), x_ref.shape[0],
             w_ref, wg_ref, bg_ref, cq_ref, cs_ref, spread_ref, unrot_ref, *out_refs)


def _meta_prep(meta_tokens, w_in_t, small, rope):
    n = meta_tokens.shape[0]
    return pl.pallas_call(
        _meta_prep_kernel,
        out_shape=([jax.ShapeDtypeStruct((D_MODEL, W_LR + LANES), BF16)]
                   + [jax.ShapeDtypeStruct((n, w), d) for w, d in zip(PROJ_WIDTHS, PROJ_DTYPES)]),
        compiler_params=pltpu.CompilerParams(vmem_limit_bytes=VMEM_LIMIT),
        name="meta_prep",
    )(meta_tokens, w_in_t, *small, *rope)


def _inproj_kernel(x0_ref, xa_ref, nw_ref, w_ref, wg_ref, bg_ref, cq_ref, cs_ref, spread_ref, unrot_ref,
                   w1_ref, w2_ref, wo_ref, *refs):
    *proj_refs, w1b_ref, w2b_ref, wob_ref, u_ref = refs

    @pl.when(pl.program_id(0) == 0)
    def _():
        u_ref[...] = _rms(x0_ref[...], nw_ref[...]).astype(BF16)

    def norm_ahead(rows):
        u_ref[rows, :] = _rms(xa_ref[rows, :], nw_ref[...]).astype(BF16)

    _project(lambda rows: u_ref[rows, :], xa_ref.shape[0], w_ref, wg_ref, bg_ref, cq_ref,
             cs_ref, spread_ref, unrot_ref, *proj_refs, after=norm_ahead)
    w1b_ref[...] = w1_ref[...].astype(BF16)
    w2b_ref[...] = w2_ref[...].astype(BF16)
    wob_ref[...] = wo_ref[...].astype(BF16)


def _in_projection(x2, rope, w_all, small, w1, w2, wo, rows):
    n = x2.shape[0]
    steps = n // rows
    cs, spread, unrot = rope
    nblk_pos = cs.shape[0] // rows
    assert w1.shape[0] % steps == 0 and w2.shape[0] % steps == 0 and wo.shape[0] == steps * SWA_HD

    def row_spec(width):
        return pl.BlockSpec((rows, width), lambda i: (i, 0))

    def full(a):
        return pl.BlockSpec(a.shape, lambda i: (0,) * a.ndim)

    def slice_spec(a):
        return pl.BlockSpec((a.shape[0] // steps, a.shape[1]), lambda i: (i, 0))

    n_gla = GV // SWA_HD
    half = SWA_HEADS // 2

    def wo_src(i):
        j = i - n_gla
        return (jnp.where(j < 0, i, n_gla + j // 2 + half * (j % 2)), 0)

    x_first = pl.BlockSpec((rows, D_MODEL), lambda i: (0, 0), pipeline_mode=pl.Buffered(1))
    x_ahead = pl.BlockSpec((rows, D_MODEL), lambda i: (jnp.minimum(i + 1, steps - 1), 0))
    in_specs = ([x_first, x_ahead, full(small[0]), full(w_all)] + [full(a) for a in small[1:]]
                + [pl.BlockSpec((rows, cs.shape[1]), lambda i: (i % nblk_pos, 0)), full(spread), full(unrot),
                   slice_spec(w1), slice_spec(w2), pl.BlockSpec((SWA_HD, wo.shape[1]), wo_src)])
    weights_out = [jax.ShapeDtypeStruct(a.shape, BF16) for a in (w1, w2, wo)]
    return pl.pallas_call(
        _inproj_kernel,
        grid=(steps,),
        in_specs=in_specs,
        out_specs=[row_spec(w) for w in PROJ_WIDTHS] + [slice_spec(w1), slice_spec(w2), slice_spec(wo)],
        out_shape=[jax.ShapeDtypeStruct((n, w), d) for w, d in zip(PROJ_WIDTHS, PROJ_DTYPES)] + weights_out,
        scratch_shapes=[pltpu.VMEM((rows, D_MODEL), BF16)],
        compiler_params=pltpu.CompilerParams(dimension_semantics=("arbitrary",),
                                             vmem_limit_bytes=VMEM_LIMIT),
        name="in_projection",
    )(x2, x2, small[0], w_all, *small[1:], cs, spread, unrot, w1, w2, wo)


def _block_cumsums(g):
    rows = g.shape[0]
    row = lax.broadcasted_iota(jnp.int32, g.shape, 0)
    w, t = [g], [g]
    for k in range(GLA_LEVELS):
        s = 1 << k
        upper = (row & s) != 0
        below = pltpu.roll(t[k], s, axis=0)
        above = pltpu.roll(t[k], rows - s, axis=0)
        w.append(w[k] + jnp.where(upper, below, 0.0))
        t.append(t[k] + jnp.where(upper, below, above))
    return w, t


def _gla_levels():
    ri = lax.broadcasted_iota(jnp.int32, (GLA_CHUNK, LANES), 0)
    ci = lax.broadcasted_iota(jnp.int32, (GLA_CHUNK, LANES), 1) & (GLA_CHUNK - 1)
    x = ri ^ ci
    level = jnp.zeros_like(x)
    for b in range(GLA_LEVELS):
        level = level + (x >= (1 << b)).astype(jnp.int32)
    return jnp.where(ci > ri, -1, level)


def _gla_chunk(q, k, v, g, st, level):
    w, t = _block_cumsums(g)
    n_slab = GQ // LANES

    def block_diag_t(m):
        r = m.shape[0]
        m2 = jnp.concatenate([m, m], axis=0).T
        same_head = ((lax.broadcasted_iota(jnp.int32, m2.shape, 0) < GLA_DK)
                     == (lax.broadcasted_iota(jnp.int32, m2.shape, 1) < r))
        return jnp.where(same_head, m2, jnp.zeros_like(m2))

    o = None
    if level is not None:
        a = [jnp.zeros((GLA_CHUNK, LANES), F32) for _ in range(n_slab)]
        for lv in range(GLA_LEVELS + 1):
            if lv == 0:
                qt, kt = q.astype(BF16), k.astype(BF16)
            else:
                qt = (q * jnp.exp2(w[lv - 1])).astype(BF16)
                kt = (k * jnp.exp2(t[lv - 1] - w[lv - 1])).astype(BF16)
            for s in range(n_slab):
                sl = slice(s * LANES, (s + 1) * LANES)
                scores = jnp.dot(qt[:, sl], block_diag_t(kt[:, sl]), preferred_element_type=F32)
                a[s] = jnp.where(level == lv, scores, a[s])

        qb = (q * jnp.exp2(w[GLA_LEVELS])).astype(BF16)
        st_b = st.astype(BF16)
        outs = []
        for s in range(n_slab):
            sl = slice(s * LANES, (s + 1) * LANES)
            v_ab = v[:, (2 * s) * GLA_DV:(2 * s + 2) * GLA_DV]
            first = lax.broadcasted_iota(jnp.int32, v_ab.shape, 1) < GLA_DV
            zero = jnp.zeros_like(v_ab)
            v_bd = jnp.concatenate([jnp.where(first, v_ab, zero), jnp.where(first, zero, v_ab)], axis=0)
            outs.append(jnp.dot(a[s].astype(BF16), v_bd, preferred_element_type=F32)
                        + jnp.dot(qb[:, sl], block_diag_t(st_b[:, sl]), preferred_element_type=F32))
        o = jnp.concatenate(outs, axis=1)

    tot = t[GLA_LEVELS]
    ku = (k * jnp.exp2(tot - w[GLA_LEVELS])).astype(BF16)
    decayed = st * jnp.exp2(tot[0:1, :])
    lo = lax.broadcasted_iota(jnp.int32, (GLA_DV, LANES), 1) < GLA_DK
    new_st = []
    for s in range(n_slab):
        sl = slice(s * LANES, (s + 1) * LANES)
        upd = _tn_dot(v[:, (2 * s) * GLA_DV:(2 * s + 2) * GLA_DV], ku[:, sl])
        new_st.append(decayed[:, sl] + jnp.where(lo, upd[:GLA_DV], upd[GLA_DV:]))
    return o, jnp.concatenate(new_st, axis=1)


def _gla_gate(o, r, nw):
    out = []
    for h in range(GLA_HEADS):
        sl = slice(h * GLA_DV, (h + 1) * GLA_DV)
        rh = r[:, sl].astype(F32)
        out.append((_rms(o[:, sl], nw) * (rh * jax.nn.sigmoid(rh))).astype(BF16))
    return jnp.concatenate(out, axis=1)


def _gla_kernel(in_ref, g_ref, km_ref, vm_ref, gm_ref, nw_ref, o_ref, st_ref):
    @pl.when(pl.program_id(1) == 0)
    def _():
        st_ref[...] = jnp.zeros(st_ref.shape, F32)
        _, st0 = _gla_chunk(None, km_ref[...].astype(F32), vm_ref[...], gm_ref[...], st_ref[...], None)
        st_ref[...] = st0

    nw = nw_ref[...]
    level = _gla_levels()
    st = st_ref[...]
    for c in range(in_ref.shape[1] // GLA_CHUNK):
        rows = slice(c * GLA_CHUNK, (c + 1) * GLA_CHUNK)
        o, st = _gla_chunk(in_ref[0, rows, 0:GQ].astype(F32), in_ref[0, rows, GQ:2 * GQ].astype(F32),
                           in_ref[0, rows, 2 * GQ:2 * GQ + GV], g_ref[0, rows, :], st, level)
        o_ref[0, rows, :] = _gla_gate(o, in_ref[0, rows, 2 * GQ + GV:2 * GQ + 2 * GV], nw)
    st_ref[...] = st


def _gla_mixer(gla_in, g, km, vm, gm, gla_norm_w, rows):
    b, s, _ = gla_in.shape

    def seq_spec(width):
        return pl.BlockSpec((1, rows, width), lambda i, j: (i, j, 0))

    def full(a):
        return pl.BlockSpec(a.shape, lambda i, j: (0,) * a.ndim)

    return pl.pallas_call(
        _gla_kernel,
        grid=(b, s // rows),
        in_specs=[seq_spec(gla_in.shape[-1]), seq_spec(GQ), full(km), full(vm), full(gm), full(gla_norm_w)],
        out_specs=seq_spec(GV),
        out_shape=jax.ShapeDtypeStruct((b, s, GV), BF16),
        scratch_shapes=[pltpu.VMEM((GLA_DV, GQ), F32)],
        compiler_params=pltpu.CompilerParams(dimension_semantics=("arbitrary", "arbitrary"),
                                             vmem_limit_bytes=VMEM_LIMIT),
        name="gla_mixer",
    )(gla_in, g, km, vm, gm, gla_norm_w)


def _swa_block(sink_ref, q, kc, kp, vc, vp, km, vm, has_prev):
    n_win = SWA_BLOCK + SWA_HALF
    n_keys = n_win + N_META
    r = lax.broadcasted_iota(jnp.int32, (SWA_HALF, n_keys), 0)
    c = lax.broadcasted_iota(jnp.int32, (SWA_HALF, n_keys), 1)
    in_band = (c > r) & (c <= r + SWA_BLOCK)
    lane = lax.broadcasted_iota(jnp.int32, (SWA_HALF, LANES), 1)
    low = lane < SWA_HD

    halves = []
    for half in range(2):
        rows = slice(half * SWA_HALF, (half + 1) * SWA_HALF)
        if half == 0:
            k_all = jnp.concatenate([kp, kc[:SWA_HALF], km], axis=0)
            v_all = jnp.concatenate([vp, vc[:SWA_HALF], vm], axis=0)
            n_prev = SWA_BLOCK
        else:
            k_all = jnp.concatenate([kp[SWA_HALF:], kc, km], axis=0)
            v_all = jnp.concatenate([vp[SWA_HALF:], vc, vm], axis=0)
            n_prev = SWA_HALF
        valid = (c >= n_win) | in_band
        if has_prev is not None:
            valid = (c >= n_win) | (in_band & (has_prev | (c >= n_prev)))
        bias = jnp.where(valid, 0.0, -jnp.inf)
        pieces = []
        for t in range(SWA_HEADS // 2):
            qt = q[rows, t * LANES:(t + 1) * LANES]
            pieces += [jnp.where(low, qt, jnp.zeros_like(qt)), jnp.where(low, jnp.zeros_like(qt), qt)]
        q_all = jnp.concatenate(pieces, axis=0)
        s_all = _nt_dot(q_all, k_all)
        p, denom = [], []
        for i in range(SWA_HEADS):
            s = s_all[i * SWA_HALF:(i + 1) * SWA_HALF] + bias
            sink = sink_ref[(i // 2) + (i % 2) * (SWA_HEADS // 2)] * LOG2_E
            m = jnp.maximum(jnp.max(s, axis=-1, keepdims=True), sink)
            p.append(jnp.exp2(s - m))
            denom.append(jnp.sum(p[i], axis=-1, keepdims=True) + jnp.exp2(sink - m))
        o = jnp.dot(jnp.concatenate(p, axis=0).astype(BF16), v_all, preferred_element_type=F32)
        slabs = []
        for t in range(SWA_HEADS // 2):
            lo = o[(2 * t) * SWA_HALF:(2 * t + 1) * SWA_HALF] / denom[2 * t]
            hi = o[(2 * t + 1) * SWA_HALF:(2 * t + 2) * SWA_HALF] / denom[2 * t + 1]
            slabs.append(jnp.where(low, lo, hi).astype(BF16))
        halves.append(jnp.concatenate(slabs, axis=1))
    return jnp.concatenate(halves, axis=0)


def _swa_ffn_kernel(tiles_per_seq, sink_ref, sq0_ref, sk0_ref, sv0_ref, sq_ref, sk_ref, sv_ref, skp_ref, svp_ref,
                    km_ref, vm_ref, x_ref, og_ref, wo_ref, nf_ref, w1_ref, w2_ref, nl_ref, y_ref, os_ref):
    s = pl.program_id(0)
    ahead = jnp.minimum(s + 1, pl.num_programs(0) - 1)
    seq_start = lax.rem(ahead, tiles_per_seq) == 0
    cur = lax.rem(s, 2)

    def swa_tile_block(j, slot, q_ref, k_ref, v_ref, first_prev):
        rows = slice(j * SWA_BLOCK, (j + 1) * SWA_BLOCK)
        if j == 0:
            kp, vp, has_prev = first_prev
        else:
            before = slice((j - 1) * SWA_BLOCK, j * SWA_BLOCK)
            kp, vp, has_prev = k_ref[before, :], v_ref[before, :], None
        os_ref[slot, rows, :] = _swa_block(sink_ref, q_ref[rows, :], k_ref[rows, :], kp, v_ref[rows, :], vp,
                                           km_ref[...], vm_ref[...], has_prev)

    n_blocks = x_ref.shape[0] // SWA_BLOCK

    @pl.when(s == 0)
    def _():
        for j in range(n_blocks):
            swa_tile_block(j, 0, sq0_ref, sk0_ref, sv0_ref, (sk0_ref[0:SWA_BLOCK, :], sv0_ref[0:SWA_BLOCK, :], s > 0))

    n_ff = D_FF // FF_CHUNK
    swa_per_ff = n_blocks // n_ff

    h = (x_ref[...]
         + jnp.dot(og_ref[...], wo_ref[0:GV, :], preferred_element_type=F32)
         + jnp.dot(os_ref[cur], wo_ref[GV:GV + SQ, :], preferred_element_type=F32))
    f = _rms(h, nf_ref[...]).astype(BF16)
    ff = None
    for c in range(n_ff):
        for j in range(c * swa_per_ff, (c + 1) * swa_per_ff):
            swa_tile_block(j, 1 - cur, sq_ref, sk_ref, sv_ref,
                           (skp_ref[...], svp_ref[...], jnp.logical_not(seq_start)))
        sl = slice(c * FF_CHUNK, (c + 1) * FF_CHUNK)
        a = jnp.maximum(jnp.dot(f, w1_ref[:, sl], preferred_element_type=F32), 0.0)
        d = jnp.dot((a * a).astype(BF16), w2_ref[sl, :], preferred_element_type=F32)
        ff = d if ff is None else ff + d
    y_ref[...] = _rms(h + ff, nl_ref[...])


def _swa_ffn(sinks, swa_in, km, vm, x2, og, wo, nf, w1, w2, nl, rows, tiles_per_seq):
    n = x2.shape[0]
    n_tiles = n // rows
    prev_per_tile = rows // SWA_BLOCK
    lane_block = {SQ: 0, SKV: SQ // SKV}

    def first_spec(width, k=0):
        return pl.BlockSpec((rows, width), lambda i: (0, lane_block[width] + k), pipeline_mode=pl.Buffered(1))

    def ahead_spec(width, k=0):
        return pl.BlockSpec((rows, width), lambda i: (jnp.minimum(i + 1, n_tiles - 1), lane_block[width] + k))

    def prev_spec(width, k=0):
        return pl.BlockSpec((SWA_BLOCK, width),
                            lambda i: (jnp.minimum(i + 1, n_tiles - 1) * prev_per_tile - 1, lane_block[width] + k))

    def ffn_spec(width):
        return pl.BlockSpec((rows, width), lambda i: (i, 0))

    def full(a):
        return pl.BlockSpec(a.shape, lambda i: (0,) * a.ndim, pipeline_mode=pl.Buffered(1))

    weights = (wo, nf, w1, w2, nl)
    return pl.pallas_call(
        functools.partial(_swa_ffn_kernel, tiles_per_seq),
        grid=(n_tiles,),
        in_specs=([pl.BlockSpec(memory_space=pltpu.SMEM), first_spec(SQ), first_spec(SKV), first_spec(SKV, 1),
                   ahead_spec(SQ), ahead_spec(SKV), ahead_spec(SKV, 1), prev_spec(SKV), prev_spec(SKV, 1),
                   full(km), full(vm), ffn_spec(D_MODEL), ffn_spec(GV)] + [full(a) for a in weights]),
        out_specs=ffn_spec(D_MODEL),
        out_shape=jax.ShapeDtypeStruct((n, D_MODEL), F32),
        scratch_shapes=[pltpu.VMEM((2, rows, SQ), BF16)],
        compiler_params=pltpu.CompilerParams(dimension_semantics=("arbitrary",),
                                             vmem_limit_bytes=VMEM_LIMIT),
        name="swa_ffn",
    )(sinks, *([swa_in] * 8), km, vm, x2, og, *weights)


def _rope_cos_sin(first_pos, n_pos):
    inv_freq = 1.0 / (ROPE_THETA ** (jnp.arange(0, ROPE_DIM, 2, dtype=F32) / ROPE_DIM))
    ang = jnp.arange(first_pos, first_pos + n_pos, dtype=jnp.int32).astype(F32)[:, None] * inv_freq[None, :]
    return jnp.concatenate([jnp.cos(ang), jnp.sin(ang)], axis=1)


def _rope_spread():
    half = ROPE_DIM // 2
    lane = jnp.arange(3 * LANES)
    d = lane % SWA_HD
    which = lane // LANES
    src = jnp.where(which == 0, d % half, half + d % half)
    live = jnp.where(which == 0, d < ROPE_DIM, jnp.where(which == 1, d < half, (d >= half) & (d < ROPE_DIM)))
    sign = jnp.where(which == 1, -1.0, 1.0)
    spread = jnp.where(live[None, :] & (jnp.arange(2 * half)[:, None] == src[None, :]), sign[None, :], 0.0)
    unrotated = jnp.where((which == 0) & (d >= ROPE_DIM), 1.0, 0.0)
    return jnp.concatenate([spread, spread], axis=0).astype(BF16), unrotated[None, :].astype(F32)


def _swa_head_order():
    order = []
    for t in range(SWA_HEADS // 2):
        order += [t, t + SWA_HEADS // 2]
    return order


def kernel(x, meta_tokens, norm_mix_w, w_in, w_gate_up, b_gate, gla_norm_w, sinks, w_out, norm_ff_w,
           w_ff1, w_ff2, final_norm_w):
    b, s, d = x.shape
    assert d == D_MODEL and s % ROWS_IN == 0 and s % ROWS_OUT == 0 and w_in.shape[0] == 1
    assert s % ROWS_GLA == 0 and ROWS_GLA % GLA_CHUNK == 0
    assert ROWS_OUT % (SWA_BLOCK * (D_FF // FF_CHUNK)) == 0

    wg = jnp.pad(w_gate_up[0], ((0, LANES - GLA_RANK), (0, 0))).astype(BF16)
    bg = b_gate[0].reshape(1, GQ)
    nw = norm_mix_w[0].reshape(1, D_MODEL)
    cq = jnp.full((1, LANES), SWA_HD ** -0.5 * LOG2_E, F32)
    small = (nw, wg, bg, cq)

    spread, unrot = _rope_spread()
    meta_tables = (_rope_cos_sin(0, N_META), spread, unrot)
    tok_tables = (_rope_cos_sin(N_META, s), spread, unrot)

    w_all, m_gla, m_g, m_swa = _meta_prep(meta_tokens.astype(F32), w_in[0].T, small, meta_tables)
    front = ((GLA_CHUNK - N_META, 0), (0, 0))
    km = jnp.pad(m_gla[:, GQ:2 * GQ], front)
    vm = jnp.pad(m_gla[:, 2 * GQ:2 * GQ + GV], front)
    gm = jnp.pad(m_g, front)
    m_sk, m_sv = m_swa[:, SQ:SQ + SKV], m_swa[:, SQ + SKV:]

    x2 = x.reshape(b * s, d)
    gla_in, g, swa_in, w1, w2, wo = _in_projection(x2, tok_tables, w_all, small,
                                                   w_ff1[0], w_ff2[0], w_out[0], ROWS_IN)

    def seq(a):
        return a.reshape(b, s, a.shape[-1])

    o_gla = _gla_mixer(seq(gla_in), seq(g), km, vm, gm, gla_norm_w[0].reshape(1, GLA_DV), ROWS_GLA)
    y = _swa_ffn(sinks[0].astype(F32), swa_in, m_sk, m_sv, x2, o_gla.reshape(b * s, GV), wo,
                 norm_ff_w[0].reshape(1, D_MODEL), w1, w2, final_norm_w.reshape(1, D_MODEL),
                 ROWS_OUT, s // ROWS_OUT)
    return y.reshape(b, s, d)
```

```python
import functools

import jax
import jax.numpy as jnp
from jax import lax
from jax.experimental import pallas as pl
from jax.experimental.pallas import tpu as pltpu

F32 = jnp.float32
BF16 = jnp.bfloat16

D_MODEL = 1024
N_META = 16
GLA_HEADS = 4
GLA_DK = 64
GLA_DV = 128
GLA_RANK = 16
GLA_TAU = 16.0
GLA_CHUNK = 64
GLA_LEVELS = 6
SWA_HEADS = 8
SWA_KV_HEADS = 2
SWA_HD = 64
SWA_BLOCK = 128
SWA_HALF = SWA_BLOCK // 2
ROPE_DIM = 16
ROPE_THETA = 500000.0
D_FF = 4096
EPS = 1e-5
LOG2_E = 1.4426950408889634

LANES = 128
GQ = GLA_HEADS * GLA_DK
GV = GLA_HEADS * GLA_DV
SQ = SWA_HEADS * SWA_HD
SKV = SWA_KV_HEADS * SWA_HD
FF_CHUNK = 1024

ROWS_IN = 1024
ROWS_IN_SUB = 256
W_SQ = 2 * GQ + 2 * GV
W_SK = W_SQ + SQ
W_LR = W_SK + 2 * SKV
ROWS_GLA = 1024
ROWS_OUT = 512
VMEM_LIMIT = 56 * 1024 * 1024


def _nt_dot(a, b):
    return lax.dot_general(a, b, (((1,), (1,)), ((), ())), preferred_element_type=F32)


def _tn_dot(a, b):
    return lax.dot_general(a, b, (((0,), (0,)), ((), ())), preferred_element_type=F32)


def _rms(x, w):
    return x * lax.rsqrt(jnp.mean(x * x, axis=-1, keepdims=True) + EPS) * w


def _project(normed, n_rows, w_ref, wg_ref, bg_ref, cq_ref, cs_ref, spread_ref, unrot_ref, gla_ref, g_ref, swa_ref,
             after=None):
    half = ROPE_DIM // 2
    n_sub = n_rows // ROWS_IN_SUB if n_rows > ROWS_IN_SUB else 1
    sub = n_rows // n_sub
    for i in range(n_sub):
        rows = slice(i * sub, (i + 1) * sub)
        u = normed(rows)

        def proj(lo, width):
            return jnp.dot(u, w_ref[:, lo:lo + width], preferred_element_type=F32)


        cs = cs_ref[rows, :]
        hi = cs.astype(BF16)
        lo = (cs - hi.astype(F32)).astype(BF16)
        pat = jnp.dot(jnp.concatenate([hi, lo], axis=1), spread_ref[...], preferred_element_type=F32) + unrot_ref[...]
        cos, s_up, s_dn = pat[:, 0:LANES], pat[:, LANES:2 * LANES], pat[:, 2 * LANES:]

        def rope(t):
            return t * cos + pltpu.roll(t, LANES - half, axis=1) * s_up + pltpu.roll(t, half, axis=1) * s_dn

        sq = proj(W_SQ, SQ)
        for s in range(SQ // LANES):
            sl = slice(s * LANES, (s + 1) * LANES)
            swa_ref[rows, sl] = (rope(sq[:, sl]) * cq_ref[...]).astype(BF16)
        kv = proj(W_SK, 2 * SKV)
        swa_ref[rows, SQ:SQ + SKV] = rope(kv[:, :SKV]).astype(BF16)
        swa_ref[rows, SQ + SKV:SQ + 2 * SKV] = kv[:, SKV:].astype(BF16)

        lr = proj(W_LR, LANES).astype(BF16)
        z = jnp.dot(lr, wg_ref[...], preferred_element_type=F32) + bg_ref[...]
        g_ref[rows, :] = (jnp.minimum(z, 0.0) - jnp.log1p(jnp.exp(-jnp.abs(z)))) * (LOG2_E / GLA_TAU)

        qk = proj(0, 2 * GQ)
        gla_ref[rows, 0:GQ] = (qk[:, :GQ] * (GLA_DK ** -0.5)).astype(BF16)
        gla_ref[rows, GQ:2 * GQ] = qk[:, GQ:].astype(BF16)
        gla_ref[rows, 2 * GQ:2 * GQ + GV] = proj(2 * GQ, GV).astype(BF16)
        gla_ref[rows, 2 * GQ + GV:2 * GQ + 2 * GV] = proj(2 * GQ + GV, GV).astype(BF16)
        if after is not None:
            after(rows)


PROJ_WIDTHS = (2 * GQ + 2 * GV, GQ, SQ + 2 * SKV)
PROJ_DTYPES = (BF16, F32, BF16)


def _meta_prep_kernel(x_ref, wt_ref, nw_ref, wgu_ref, bg_ref, cq_ref, cs_ref, spread_ref, unrot_ref,
                      w_ref, wg_ref, km_ref, vm_ref, gm_ref, sk_ref, sv_ref, gla_s, g_s, swa_s):
    wg_ref[0:GLA_RANK, :] = wgu_ref[...].astype(BF16)
    wg_ref[GLA_RANK:, :] = jnp.zeros((LANES - GLA_RANK, GQ), BF16)
    o_lr = 2 * GQ + 2 * GV
    o_sq = o_lr + GLA_RANK
    o_sk = o_sq + SQ
    half = SWA_HEADS // 2

    def rows(lo, n):
        return wt_ref[lo:lo + n, :].astype(BF16)

    slabs = [rows(c * LANES, LANES) for c in range(o_lr // LANES)]
    slabs += [jnp.concatenate([rows(o_sq + t * SWA_HD, SWA_HD), rows(o_sq + (t + half) * SWA_HD, SWA_HD)], axis=0)
              for t in range(half)]
    slabs += [rows(o_sk + c * LANES, LANES) for c in range(2 * SKV // LANES)]
    slabs += [jnp.concatenate([rows(o_lr, GLA_RANK), jnp.zeros((LANES - GLA_RANK, D_MODEL), BF16)], axis=0)]
    for c, slab in enumerate(slabs):
        w_ref[:, c * LANES:(c + 1) * LANES] = slab.T
    _project(lambda rows: _rms(x_ref[rows, :], nw_ref[...]).astype(BF16), x_ref.shape[0],
             w_ref, wg_ref, bg_ref, cq_ref, cs_ref, spread_ref, unrot_ref, gla_s, g_s, swa_s)
    pad = GLA_CHUNK - N_META
    for ref, val in ((km_ref, gla_s[:, GQ:2 * GQ]), (vm_ref, gla_s[:, 2 * GQ:2 * GQ + GV]), (gm_ref, g_s[...])):
        ref[0:pad, :] = jnp.zeros((pad, ref.shape[1]), ref.dtype)
        ref[pad:, :] = val
    sk_ref[...] = swa_s[:, SQ:SQ + SKV]
    sv_ref[...] = swa_s[:, SQ + SKV:]


def _meta_prep(meta_tokens, w_in_t, w_gate_up, small, rope):
    n = meta_tokens.shape[0]
    nw, bg, cq = small
    out_shape = [jax.ShapeDtypeStruct((D_MODEL, W_LR + LANES), BF16), jax.ShapeDtypeStruct((LANES, GQ), BF16),
                 jax.ShapeDtypeStruct((GLA_CHUNK, GQ), BF16), jax.ShapeDtypeStruct((GLA_CHUNK, GV), BF16),
                 jax.ShapeDtypeStruct((GLA_CHUNK, GQ), F32),
                 jax.ShapeDtypeStruct((n, SKV), BF16), jax.ShapeDtypeStruct((n, SKV), BF16)]
    return pl.pallas_call(
        _meta_prep_kernel,
        out_shape=out_shape,
        scratch_shapes=[pltpu.VMEM((n, w), d) for w, d in zip(PROJ_WIDTHS, PROJ_DTYPES)],
        compiler_params=pltpu.CompilerParams(vmem_limit_bytes=VMEM_LIMIT),
        name="meta_prep",
    )(meta_tokens, w_in_t, nw, w_gate_up, bg, cq, *rope)


def _inproj_kernel(x0_ref, xa_ref, nw_ref, w_ref, wg_ref, bg_ref, cq_ref, cs_ref, spread_ref, unrot_ref,
                   w1_ref, w2_ref, wo_ref, *refs):
    *proj_refs, w1b_ref, w2b_ref, wob_ref, u_ref = refs

    @pl.when(pl.program_id(0) == 0)
    def _():
        u_ref[...] = _rms(x0_ref[...], nw_ref[...]).astype(BF16)

    def norm_ahead(rows):
        u_ref[rows, :] = _rms(xa_ref[rows, :], nw_ref[...]).astype(BF16)

    _project(lambda rows: u_ref[rows, :], xa_ref.shape[0], w_ref, wg_ref, bg_ref, cq_ref,
             cs_ref, spread_ref, unrot_ref, *proj_refs, after=norm_ahead)
    w1b_ref[...] = w1_ref[...].astype(BF16)
    w2b_ref[...] = w2_ref[...].astype(BF16)
    wob_ref[...] = wo_ref[...].astype(BF16)


def _in_projection(x2, rope, w_all, small, w1, w2, wo, rows):
    n = x2.shape[0]
    steps = n // rows
    cs, spread, unrot = rope
    nblk_pos = cs.shape[0] // rows
    assert w1.shape[0] % steps == 0 and w2.shape[0] % steps == 0 and wo.shape[0] == steps * SWA_HD

    def row_spec(width):
        return pl.BlockSpec((rows, width), lambda i: (i, 0))

    def full(a):
        return pl.BlockSpec(a.shape, lambda i: (0,) * a.ndim)

    def slice_spec(a):
        return pl.BlockSpec((a.shape[0] // steps, a.shape[1]), lambda i: (i, 0))

    n_gla = GV // SWA_HD
    half = SWA_HEADS // 2

    def wo_src(i):
        j = i - n_gla
        return (jnp.where(j < 0, i, n_gla + j // 2 + half * (j % 2)), 0)

    x_first = pl.BlockSpec((rows, D_MODEL), lambda i: (0, 0), pipeline_mode=pl.Buffered(1))
    x_ahead = pl.BlockSpec((rows, D_MODEL), lambda i: (jnp.minimum(i + 1, steps - 1), 0))
    in_specs = ([x_first, x_ahead, full(small[0]), full(w_all)] + [full(a) for a in small[1:]]
                + [pl.BlockSpec((rows, cs.shape[1]), lambda i: (i % nblk_pos, 0)), full(spread), full(unrot),
                   slice_spec(w1), slice_spec(w2), pl.BlockSpec((SWA_HD, wo.shape[1]), wo_src)])
    weights_out = [jax.ShapeDtypeStruct(a.shape, BF16) for a in (w1, w2, wo)]
    return pl.pallas_call(
        _inproj_kernel,
        grid=(steps,),
        in_specs=in_specs,
        out_specs=[row_spec(w) for w in PROJ_WIDTHS] + [slice_spec(w1), slice_spec(w2), slice_spec(wo)],
        out_shape=[jax.ShapeDtypeStruct((n, w), d) for w, d in zip(PROJ_WIDTHS, PROJ_DTYPES)] + weights_out,
        scratch_shapes=[pltpu.VMEM((rows, D_MODEL), BF16)],
        compiler_params=pltpu.CompilerParams(dimension_semantics=("arbitrary",),
                                             vmem_limit_bytes=VMEM_LIMIT),
        name="in_projection",
    )(x2, x2, small[0], w_all, *small[1:], cs, spread, unrot, w1, w2, wo)


def _block_cumsums(g):
    rows = g.shape[0]
    row = lax.broadcasted_iota(jnp.int32, g.shape, 0)
    w, t = [g], [g]
    for k in range(GLA_LEVELS):
        s = 1 << k
        upper = (row & s) != 0
        below = pltpu.roll(t[k], s, axis=0)
        above = pltpu.roll(t[k], rows - s, axis=0)
        w.append(w[k] + jnp.where(upper, below, 0.0))
        t.append(t[k] + jnp.where(upper, below, above))
    return w, t


def _gla_levels():
    ri = lax.broadcasted_iota(jnp.int32, (GLA_CHUNK, LANES), 0)
    ci = lax.broadcasted_iota(jnp.int32, (GLA_CHUNK, LANES), 1) & (GLA_CHUNK - 1)
    x = ri ^ ci
    level = jnp.zeros_like(x)
    for b in range(GLA_LEVELS):
        level = level + (x >= (1 << b)).astype(jnp.int32)
    return jnp.where(ci > ri, -1, level)


def _gla_chunk(q, k, v, g, st, level):
    w, t = _block_cumsums(g)
    n_slab = GQ // LANES

    def block_diag_t(m):
        r = m.shape[0]
        m2 = jnp.concatenate([m, m], axis=0).T
        same_head = ((lax.broadcasted_iota(jnp.int32, m2.shape, 0) < GLA_DK)
                     == (lax.broadcasted_iota(jnp.int32, m2.shape, 1) < r))
        return jnp.where(same_head, m2, jnp.zeros_like(m2))

    o = None
    if level is not None:
        a = [jnp.zeros((GLA_CHUNK, LANES), F32) for _ in range(n_slab)]
        for lv in range(GLA_LEVELS + 1):
            if lv == 0:
                qt, kt = q.astype(BF16), k.astype(BF16)
            else:
                qt = (q * jnp.exp2(w[lv - 1])).astype(BF16)
                kt = (k * jnp.exp2(t[lv - 1] - w[lv - 1])).astype(BF16)
            for s in range(n_slab):
                sl = slice(s * LANES, (s + 1) * LANES)
                scores = jnp.dot(qt[:, sl], block_diag_t(kt[:, sl]), preferred_element_type=F32)
                a[s] = jnp.where(level == lv, scores, a[s])

        qb = (q * jnp.exp2(w[GLA_LEVELS])).astype(BF16)
        st_b = st.astype(BF16)
        outs = []
        for s in range(n_slab):
            sl = slice(s * LANES, (s + 1) * LANES)
            v_ab = v[:, (2 * s) * GLA_DV:(2 * s + 2) * GLA_DV]
            first = lax.broadcasted_iota(jnp.int32, v_ab.shape, 1) < GLA_DV
            zero = jnp.zeros_like(v_ab)
            v_bd = jnp.concatenate([jnp.where(first, v_ab, zero), jnp.where(first, zero, v_ab)], axis=0)
            outs.append(jnp.dot(a[s].astype(BF16), v_bd, preferred_element_type=F32)
                        + jnp.dot(qb[:, sl], block_diag_t(st_b[:, sl]), preferred_element_type=F32))
        o = jnp.concatenate(outs, axis=1)

    tot = t[GLA_LEVELS]
    ku = (k * jnp.exp2(tot - w[GLA_LEVELS])).astype(BF16)
    decayed = st * jnp.exp2(tot[0:1, :])
    lo = lax.broadcasted_iota(jnp.int32, (GLA_DV, LANES), 1) < GLA_DK
    new_st = []
    for s in range(n_slab):
        sl = slice(s * LANES, (s + 1) * LANES)
        upd = _tn_dot(v[:, (2 * s) * GLA_DV:(2 * s + 2) * GLA_DV], ku[:, sl])
        new_st.append(decayed[:, sl] + jnp.where(lo, upd[:GLA_DV], upd[GLA_DV:]))
    return o, jnp.concatenate(new_st, axis=1)


def _gla_gate(o, r, nw):
    out = []
    for h in range(GLA_HEADS):
        sl = slice(h * GLA_DV, (h + 1) * GLA_DV)
        rh = r[:, sl].astype(F32)
        out.append((_rms(o[:, sl], nw) * (rh * jax.nn.sigmoid(rh))).astype(BF16))
    return jnp.concatenate(out, axis=1)


def _gla_kernel(in_ref, g_ref, km_ref, vm_ref, gm_ref, nw_ref, o_ref, st_ref):
    @pl.when(pl.program_id(1) == 0)
    def _():
        st_ref[...] = jnp.zeros(st_ref.shape, F32)
        _, st0 = _gla_chunk(None, km_ref[...].astype(F32), vm_ref[...], gm_ref[...], st_ref[...], None)
        st_ref[...] = st0

    nw = nw_ref[...]
    level = _gla_levels()
    st = st_ref[...]
    for c in range(in_ref.shape[1] // GLA_CHUNK):
        rows = slice(c * GLA_CHUNK, (c + 1) * GLA_CHUNK)
        o, st = _gla_chunk(in_ref[0, rows, 0:GQ].astype(F32), in_ref[0, rows, GQ:2 * GQ].astype(F32),
                           in_ref[0, rows, 2 * GQ:2 * GQ + GV], g_ref[0, rows, :], st, level)
        o_ref[0, rows, :] = _gla_gate(o, in_ref[0, rows, 2 * GQ + GV:2 * GQ + 2 * GV], nw)
    st_ref[...] = st


def _gla_mixer(gla_in, g, km, vm, gm, gla_norm_w, rows):
    b, s, _ = gla_in.shape

    def seq_spec(width):
        return pl.BlockSpec((1, rows, width), lambda i, j: (i, j, 0))

    def full(a):
        return pl.BlockSpec(a.shape, lambda i, j: (0,) * a.ndim)

    return pl.pallas_call(
        _gla_kernel,
        grid=(b, s // rows),
        in_specs=[seq_spec(gla_in.shape[-1]), seq_spec(GQ), full(km), full(vm), full(gm), full(gla_norm_w)],
        out_specs=seq_spec(GV),
        out_shape=jax.ShapeDtypeStruct((b, s, GV), BF16),
        scratch_shapes=[pltpu.VMEM((GLA_DV, GQ), F32)],
        compiler_params=pltpu.CompilerParams(dimension_semantics=("arbitrary", "arbitrary"),
                                             vmem_limit_bytes=VMEM_LIMIT),
        name="gla_mixer",
    )(gla_in, g, km, vm, gm, gla_norm_w)


def _swa_block(sink_ref, q, kc, kp, vc, vp, km, vm, has_prev):
    n_win = SWA_BLOCK + SWA_HALF
    n_keys = n_win + N_META
    r = lax.broadcasted_iota(jnp.int32, (SWA_HALF, n_keys), 0)
    c = lax.broadcasted_iota(jnp.int32, (SWA_HALF, n_keys), 1)
    in_band = (c > r) & (c <= r + SWA_BLOCK)
    lane = lax.broadcasted_iota(jnp.int32, (SWA_HALF, LANES), 1)
    low = lane < SWA_HD

    halves = []
    for half in range(2):
        rows = slice(half * SWA_HALF, (half + 1) * SWA_HALF)
        if half == 0:
            k_all = jnp.concatenate([kp, kc[:SWA_HALF], km], axis=0)
            v_all = jnp.concatenate([vp, vc[:SWA_HALF], vm], axis=0)
            n_prev = SWA_BLOCK
        else:
            k_all = jnp.concatenate([kp[SWA_HALF:], kc, km], axis=0)
            v_all = jnp.concatenate([vp[SWA_HALF:], vc, vm], axis=0)
            n_prev = SWA_HALF
        valid = (c >= n_win) | in_band
        if has_prev is not None:
            valid = (c >= n_win) | (in_band & (has_prev | (c >= n_prev)))
        bias = jnp.where(valid, 0.0, -jnp.inf)
        pieces = []
        for t in range(SWA_HEADS // 2):
            qt = q[rows, t * LANES:(t + 1) * LANES]
            pieces += [jnp.where(low, qt, jnp.zeros_like(qt)), jnp.where(low, jnp.zeros_like(qt), qt)]
        q_all = jnp.concatenate(pieces, axis=0)
        s_all = _nt_dot(q_all, k_all)
        p, denom = [], []
        for i in range(SWA_HEADS):
            s = s_all[i * SWA_HALF:(i + 1) * SWA_HALF] + bias
            sink = sink_ref[(i // 2) + (i % 2) * (SWA_HEADS // 2)] * LOG2_E
            m = jnp.maximum(jnp.max(s, axis=-1, keepdims=True), sink)
            p.append(jnp.exp2(s - m))
            denom.append(jnp.sum(p[i], axis=-1, keepdims=True) + jnp.exp2(sink - m))
        o = jnp.dot(jnp.concatenate(p, axis=0).astype(BF16), v_all, preferred_element_type=F32)
        slabs = []
        for t in range(SWA_HEADS // 2):
            lo = o[(2 * t) * SWA_HALF:(2 * t + 1) * SWA_HALF] / denom[2 * t]
            hi = o[(2 * t + 1) * SWA_HALF:(2 * t + 2) * SWA_HALF] / denom[2 * t + 1]
            slabs.append(jnp.where(low, lo, hi).astype(BF16))
        halves.append(jnp.concatenate(slabs, axis=1))
    return jnp.concatenate(halves, axis=0)


def _swa_ffn_kernel(tiles_per_seq, sink_ref, swa0_ref, swa_ref, prev_ref, km_ref, vm_ref,
                    x_ref, og_ref, wo_ref, nf_ref, w1_ref, w2_ref, nl_ref, y_ref, os_ref):
    s = pl.program_id(0)
    ahead = jnp.minimum(s + 1, pl.num_programs(0) - 1)
    seq_start = lax.rem(ahead, tiles_per_seq) == 0
    cur = lax.rem(s, 2)
    k_lanes, v_lanes = slice(SQ, SQ + SKV), slice(SQ + SKV, SQ + 2 * SKV)

    def swa_tile_block(j, slot, tile_ref, first_prev):
        rows = slice(j * SWA_BLOCK, (j + 1) * SWA_BLOCK)
        if j == 0:
            kp, vp, has_prev = first_prev
        else:
            before = slice((j - 1) * SWA_BLOCK, j * SWA_BLOCK)
            kp, vp, has_prev = tile_ref[before, k_lanes], tile_ref[before, v_lanes], None
        os_ref[slot, rows, :] = _swa_block(sink_ref, tile_ref[rows, 0:SQ], tile_ref[rows, k_lanes], kp,
                                           tile_ref[rows, v_lanes], vp, km_ref[...], vm_ref[...], has_prev)

    n_blocks = x_ref.shape[0] // SWA_BLOCK

    @pl.when(s == 0)
    def _():
        stand_in = (swa0_ref[0:SWA_BLOCK, k_lanes], swa0_ref[0:SWA_BLOCK, v_lanes], s > 0)
        for j in range(n_blocks):
            swa_tile_block(j, 0, swa0_ref, stand_in)

    n_ff = D_FF // FF_CHUNK
    swa_per_ff = n_blocks // n_ff

    h = (x_ref[...]
         + jnp.dot(og_ref[...], wo_ref[0:GV, :], preferred_element_type=F32)
         + jnp.dot(os_ref[cur], wo_ref[GV:GV + SQ, :], preferred_element_type=F32))
    f = _rms(h, nf_ref[...]).astype(BF16)
    ff = None
    for c in range(n_ff):
        for j in range(c * swa_per_ff, (c + 1) * swa_per_ff):
            swa_tile_block(j, 1 - cur, swa_ref,
                           (prev_ref[:, 0:SKV], prev_ref[:, SKV:], jnp.logical_not(seq_start)))
        sl = slice(c * FF_CHUNK, (c + 1) * FF_CHUNK)
        a = jnp.maximum(jnp.dot(f, w1_ref[:, sl], preferred_element_type=F32), 0.0)
        d = jnp.dot((a * a).astype(BF16), w2_ref[sl, :], preferred_element_type=F32)
        ff = d if ff is None else ff + d
    y_ref[...] = _rms(h + ff, nl_ref[...])


def _swa_ffn(sinks, swa_in, km, vm, x2, og, wo, nf, w1, w2, nl, rows, tiles_per_seq):
    n = x2.shape[0]
    n_tiles = n // rows
    prev_per_tile = rows // SWA_BLOCK
    width = swa_in.shape[1]

    first_spec = pl.BlockSpec((rows, width), lambda i: (0, 0), pipeline_mode=pl.Buffered(1))
    ahead_spec = pl.BlockSpec((rows, width), lambda i: (jnp.minimum(i + 1, n_tiles - 1), 0))
    prev_spec = pl.BlockSpec((SWA_BLOCK, 2 * SKV),
                             lambda i: (jnp.minimum(i + 1, n_tiles - 1) * prev_per_tile - 1, SQ // (2 * SKV)))

    def ffn_spec(width):
        return pl.BlockSpec((rows, width), lambda i: (i, 0))

    def full(a):
        return pl.BlockSpec(a.shape, lambda i: (0,) * a.ndim, pipeline_mode=pl.Buffered(1))

    weights = (wo, nf, w1, w2, nl)
    return pl.pallas_call(
        functools.partial(_swa_ffn_kernel, tiles_per_seq),
        grid=(n_tiles,),
        in_specs=([pl.BlockSpec(memory_space=pltpu.SMEM), first_spec, ahead_spec, prev_spec,
                   full(km), full(vm), ffn_spec(D_MODEL), ffn_spec(GV)] + [full(a) for a in weights]),
        out_specs=ffn_spec(D_MODEL),
        out_shape=jax.ShapeDtypeStruct((n, D_MODEL), F32),
        scratch_shapes=[pltpu.VMEM((2, rows, SQ), BF16)],
        compiler_params=pltpu.CompilerParams(dimension_semantics=("arbitrary",),
                                             vmem_limit_bytes=VMEM_LIMIT),
        name="swa_ffn",
    )(sinks, swa_in, swa_in, swa_in, km, vm, x2, og, *weights)


def _rope_cos_sin(first_pos, n_pos):
    inv_freq = 1.0 / (ROPE_THETA ** (jnp.arange(0, ROPE_DIM, 2, dtype=F32) / ROPE_DIM))
    inv_freq = jnp.concatenate([inv_freq, inv_freq])
    ang = jnp.arange(first_pos, first_pos + n_pos, dtype=jnp.int32).astype(F32)[:, None] * inv_freq[None, :]
    return jnp.where(jnp.arange(ROPE_DIM)[None, :] < ROPE_DIM // 2, jnp.cos(ang), jnp.sin(ang))


def _rope_spread():
    half = ROPE_DIM // 2
    lane = jnp.arange(3 * LANES)
    d = lane % SWA_HD
    which = lane // LANES
    src = jnp.where(which == 0, d % half, half + d % half)
    live = jnp.where(which == 0, d < ROPE_DIM, jnp.where(which == 1, d < half, (d >= half) & (d < ROPE_DIM)))
    sign = jnp.where(which == 1, -1.0, 1.0)
    spread = jnp.where(live[None, :] & (jnp.arange(2 * half)[:, None] == src[None, :]), sign[None, :], 0.0)
    unrotated = jnp.where((which == 0) & (d >= ROPE_DIM), 1.0, 0.0)
    return jnp.concatenate([spread, spread], axis=0).astype(BF16), unrotated[None, :].astype(F32)


def _swa_head_order():
    order = []
    for t in range(SWA_HEADS // 2):
        order += [t, t + SWA_HEADS // 2]
    return order


def kernel(x, meta_tokens, norm_mix_w, w_in, w_gate_up, b_gate, gla_norm_w, sinks, w_out, norm_ff_w,
           w_ff1, w_ff2, final_norm_w):
    b, s, d = x.shape
    assert d == D_MODEL and s % ROWS_IN == 0 and s % ROWS_OUT == 0 and w_in.shape[0] == 1
    assert s % ROWS_GLA == 0 and ROWS_GLA % GLA_CHUNK == 0
    assert ROWS_OUT % (SWA_BLOCK * (D_FF // FF_CHUNK)) == 0

    bg = b_gate[0].reshape(1, GQ)
    nw = norm_mix_w[0].reshape(1, D_MODEL)
    cq = jnp.full((1, LANES), SWA_HD ** -0.5 * LOG2_E, F32)

    spread, unrot = _rope_spread()
    meta_tables = (_rope_cos_sin(0, N_META), spread, unrot)
    tok_tables = (_rope_cos_sin(N_META, s), spread, unrot)

    w_all, wg, km, vm, gm, m_sk, m_sv = _meta_prep(meta_tokens.astype(F32), w_in[0].T, w_gate_up[0],
                                                   (nw, bg, cq), meta_tables)
    small = (nw, wg, bg, cq)

    x2 = x.reshape(b * s, d)
    gla_in, g, swa_in, w1, w2, wo = _in_projection(x2, tok_tables, w_all, small,
                                                   w_ff1[0], w_ff2[0], w_out[0], ROWS_IN)

    def seq(a):
        return a.reshape(b, s, a.shape[-1])

    o_gla = _gla_mixer(seq(gla_in), seq(g), km, vm, gm, gla_norm_w[0].reshape(1, GLA_DV), ROWS_GLA)
    y = _swa_ffn(sinks[0].astype(F32), swa_in, m_sk, m_sv, x2, o_gla.reshape(b * s, GV), wo,
                 norm_ff_w[0].reshape(1, D_MODEL), w1, w2, final_norm_w.reshape(1, D_MODEL),
                 ROWS_OUT, s // ROWS_OUT)
    return y.reshape(b, s, d)
```

```python
import functools

import jax
import jax.numpy as jnp
from jax import lax
from jax.experimental import pallas as pl
from jax.experimental.pallas import tpu as pltpu

F32 = jnp.float32
BF16 = jnp.bfloat16

D_MODEL = 1024
N_META = 16
GLA_HEADS = 4
GLA_DK = 64
GLA_DV = 128
GLA_RANK = 16
GLA_TAU = 16.0
GLA_CHUNK = 64
GLA_LEVELS = 6
SWA_HEADS = 8
SWA_KV_HEADS = 2
SWA_HD = 64
SWA_BLOCK = 128
SWA_HALF = SWA_BLOCK // 2
ROPE_DIM = 16
ROPE_THETA = 500000.0
D_FF = 4096
EPS = 1e-5
LOG2_E = 1.4426950408889634

LANES = 128
GQ = GLA_HEADS * GLA_DK
GV = GLA_HEADS * GLA_DV
SQ = SWA_HEADS * SWA_HD
SKV = SWA_KV_HEADS * SWA_HD
FF_CHUNK = 1024

ROWS_IN = 1024
ROWS_IN_SUB = 256
W_SQ = 2 * GQ + 2 * GV
W_SK = W_SQ + SQ
W_LR = W_SK + 2 * SKV
ROWS_GLA = 1024
ROWS_OUT = 512
VMEM_LIMIT = 56 * 1024 * 1024


def _nt_dot(a, b):
    return lax.dot_general(a, b, (((1,), (1,)), ((), ())), preferred_element_type=F32)


def _tn_dot(a, b):
    return lax.dot_general(a, b, (((0,), (0,)), ((), ())), preferred_element_type=F32)


def _rms(x, w):
    return x * lax.rsqrt(jnp.mean(x * x, axis=-1, keepdims=True) + EPS) * w


def _project(normed, n_rows, w_ref, wg_ref, bg_ref, cq_ref, cs_ref, spread_ref, unrot_ref, gla_ref, g_ref, swa_ref,
             after=None):
    half = ROPE_DIM // 2
    n_sub = n_rows // ROWS_IN_SUB if n_rows > ROWS_IN_SUB else 1
    sub = n_rows // n_sub
    for i in range(n_sub):
        rows = slice(i * sub, (i + 1) * sub)
        u = normed(rows)

        def proj(lo, width):
            return jnp.dot(u, w_ref[:, lo:lo + width], preferred_element_type=F32)


        cs = cs_ref[rows, :]
        hi = cs.astype(BF16)
        lo = (cs - hi.astype(F32)).astype(BF16)
        pat = jnp.dot(jnp.concatenate([hi, lo], axis=1), spread_ref[...], preferred_element_type=F32) + unrot_ref[...]
        cos, sin = pat[:, 0:LANES], pat[:, LANES:]
        d = lax.broadcasted_iota(jnp.int32, cos.shape, 1) & (SWA_HD - 1)

        def rope(t):
            ts = t * sin
            turned = jnp.where(d < half, -pltpu.roll(ts, LANES - half, axis=1),
                               jnp.where(d < ROPE_DIM, pltpu.roll(ts, half, axis=1), 0.0))
            return t * cos + turned

        sq = proj(W_SQ, SQ)
        for s in range(SQ // LANES):
            sl = slice(s * LANES, (s + 1) * LANES)
            swa_ref[rows, sl] = (rope(sq[:, sl]) * cq_ref[...]).astype(BF16)
        kv = proj(W_SK, 2 * SKV)
        swa_ref[rows, SQ:SQ + SKV] = rope(kv[:, :SKV]).astype(BF16)
        swa_ref[rows, SQ + SKV:SQ + 2 * SKV] = kv[:, SKV:].astype(BF16)

        lr = proj(W_LR, LANES).astype(BF16)
        z = jnp.dot(lr, wg_ref[...], preferred_element_type=F32) + bg_ref[...]
        g_ref[rows, :] = (jnp.minimum(z, 0.0) - jnp.log1p(jnp.exp(-jnp.abs(z)))) * (LOG2_E / GLA_TAU)

        qk = proj(0, 2 * GQ)
        gla_ref[rows, 0:GQ] = (qk[:, :GQ] * (GLA_DK ** -0.5)).astype(BF16)
        gla_ref[rows, GQ:2 * GQ] = qk[:, GQ:].astype(BF16)
        gla_ref[rows, 2 * GQ:2 * GQ + GV] = proj(2 * GQ, GV).astype(BF16)
        gla_ref[rows, 2 * GQ + GV:2 * GQ + 2 * GV] = proj(2 * GQ + GV, GV).astype(BF16)
        if after is not None:
            after(rows)


PROJ_WIDTHS = (2 * GQ + 2 * GV, GQ, SQ + 2 * SKV)
PROJ_DTYPES = (BF16, F32, BF16)


def _meta_prep_kernel(x_ref, wt_ref, nw_ref, wgu_ref, bg_ref, cq_ref, cs_ref, spread_ref, unrot_ref,
                      w_ref, wg_ref, km_ref, vm_ref, gm_ref, sk_ref, sv_ref, gla_s, g_s, swa_s):
    wg_ref[0:GLA_RANK, :] = wgu_ref[...].astype(BF16)
    wg_ref[GLA_RANK:, :] = jnp.zeros((LANES - GLA_RANK, GQ), BF16)
    o_lr = 2 * GQ + 2 * GV
    o_sq = o_lr + GLA_RANK
    o_sk = o_sq + SQ
    half = SWA_HEADS // 2

    def rows(lo, n):
        return wt_ref[lo:lo + n, :].astype(BF16)

    slabs = [rows(c * LANES, LANES) for c in range(o_lr // LANES)]
    slabs += [jnp.concatenate([rows(o_sq + t * SWA_HD, SWA_HD), rows(o_sq + (t + half) * SWA_HD, SWA_HD)], axis=0)
              for t in range(half)]
    slabs += [rows(o_sk + c * LANES, LANES) for c in range(2 * SKV // LANES)]
    slabs += [jnp.concatenate([rows(o_lr, GLA_RANK), jnp.zeros((LANES - GLA_RANK, D_MODEL), BF16)], axis=0)]
    for c, slab in enumerate(slabs):
        w_ref[:, c * LANES:(c + 1) * LANES] = slab.T
    _project(lambda rows: _rms(x_ref[rows, :], nw_ref[...]).astype(BF16), x_ref.shape[0],
             w_ref, wg_ref, bg_ref, cq_ref, cs_ref, spread_ref, unrot_ref, gla_s, g_s, swa_s)
    pad = GLA_CHUNK - N_META
    for ref, val in ((km_ref, gla_s[:, GQ:2 * GQ]), (vm_ref, gla_s[:, 2 * GQ:2 * GQ + GV]), (gm_ref, g_s[...])):
        ref[0:pad, :] = jnp.zeros((pad, ref.shape[1]), ref.dtype)
        ref[pad:, :] = val
    sk_ref[...] = swa_s[:, SQ:SQ + SKV]
    sv_ref[...] = swa_s[:, SQ + SKV:]


def _meta_prep(meta_tokens, w_in_t, w_gate_up, small, rope):
    n = meta_tokens.shape[0]
    nw, bg, cq = small
    out_shape = [jax.ShapeDtypeStruct((D_MODEL, W_LR + LANES), BF16), jax.ShapeDtypeStruct((LANES, GQ), BF16),
                 jax.ShapeDtypeStruct((GLA_CHUNK, GQ), BF16), jax.ShapeDtypeStruct((GLA_CHUNK, GV), BF16),
                 jax.ShapeDtypeStruct((GLA_CHUNK, GQ), F32),
                 jax.ShapeDtypeStruct((n, SKV), BF16), jax.ShapeDtypeStruct((n, SKV), BF16)]
    return pl.pallas_call(
        _meta_prep_kernel,
        out_shape=out_shape,
        scratch_shapes=[pltpu.VMEM((n, w), d) for w, d in zip(PROJ_WIDTHS, PROJ_DTYPES)],
        compiler_params=pltpu.CompilerParams(vmem_limit_bytes=VMEM_LIMIT),
        name="meta_prep",
    )(meta_tokens, w_in_t, nw, w_gate_up, bg, cq, *rope)


def _inproj_kernel(x0_ref, xa_ref, nw_ref, w_ref, wg_ref, bg_ref, cq_ref, cs_ref, spread_ref, unrot_ref,
                   w1_ref, w2_ref, wo_ref, *refs):
    *proj_refs, w1b_ref, w2b_ref, wob_ref, u_ref = refs

    @pl.when(pl.program_id(0) == 0)
    def _():
        u_ref[...] = _rms(x0_ref[...], nw_ref[...]).astype(BF16)

    def norm_ahead(rows):
        u_ref[rows, :] = _rms(xa_ref[rows, :], nw_ref[...]).astype(BF16)

    _project(lambda rows: u_ref[rows, :], xa_ref.shape[0], w_ref, wg_ref, bg_ref, cq_ref,
             cs_ref, spread_ref, unrot_ref, *proj_refs, after=norm_ahead)
    w1b_ref[...] = w1_ref[...].astype(BF16)
    w2b_ref[...] = w2_ref[...].astype(BF16)
    wob_ref[...] = wo_ref[...].astype(BF16)


def _in_projection(x2, rope, w_all, small, w1, w2, wo, rows):
    n = x2.shape[0]
    steps = n // rows
    cs, spread, unrot = rope
    nblk_pos = cs.shape[0] // rows
    assert w1.shape[0] % steps == 0 and w2.shape[0] % steps == 0 and wo.shape[0] == steps * SWA_HD

    def row_spec(width):
        return pl.BlockSpec((rows, width), lambda i: (i, 0))

    def full(a):
        return pl.BlockSpec(a.shape, lambda i: (0,) * a.ndim)

    def slice_spec(a):
        return pl.BlockSpec((a.shape[0] // steps, a.shape[1]), lambda i: (i, 0))

    n_gla = GV // SWA_HD
    half = SWA_HEADS // 2

    def wo_src(i):
        j = i - n_gla
        return (jnp.where(j < 0, i, n_gla + j // 2 + half * (j % 2)), 0)

    x_first = pl.BlockSpec((rows, D_MODEL), lambda i: (0, 0), pipeline_mode=pl.Buffered(1))
    x_ahead = pl.BlockSpec((rows, D_MODEL), lambda i: (jnp.minimum(i + 1, steps - 1), 0))
    in_specs = ([x_first, x_ahead, full(small[0]), full(w_all)] + [full(a) for a in small[1:]]
                + [pl.BlockSpec((rows, cs.shape[1]), lambda i: (i % nblk_pos, 0)), full(spread), full(unrot),
                   slice_spec(w1), slice_spec(w2), pl.BlockSpec((SWA_HD, wo.shape[1]), wo_src)])
    weights_out = [jax.ShapeDtypeStruct(a.shape, BF16) for a in (w1, w2, wo)]
    return pl.pallas_call(
        _inproj_kernel,
        grid=(steps,),
        in_specs=in_specs,
        out_specs=[row_spec(w) for w in PROJ_WIDTHS] + [slice_spec(w1), slice_spec(w2), slice_spec(wo)],
        out_shape=[jax.ShapeDtypeStruct((n, w), d) for w, d in zip(PROJ_WIDTHS, PROJ_DTYPES)] + weights_out,
        scratch_shapes=[pltpu.VMEM((rows, D_MODEL), BF16)],
        compiler_params=pltpu.CompilerParams(dimension_semantics=("arbitrary",),
                                             vmem_limit_bytes=VMEM_LIMIT),
        name="in_projection",
    )(x2, x2, small[0], w_all, *small[1:], cs, spread, unrot, w1, w2, wo)


def _block_cumsums(g):
    rows = g.shape[0]
    row = lax.broadcasted_iota(jnp.int32, g.shape, 0)
    w, t = [g], [g]
    for k in range(GLA_LEVELS):
        s = 1 << k
        upper = (row & s) != 0
        below = pltpu.roll(t[k], s, axis=0)
        above = pltpu.roll(t[k], rows - s, axis=0)
        w.append(w[k] + jnp.where(upper, below, 0.0))
        t.append(t[k] + jnp.where(upper, below, above))
    return w, t


def _gla_levels():
    ri = lax.broadcasted_iota(jnp.int32, (GLA_CHUNK, LANES), 0)
    ci = lax.broadcasted_iota(jnp.int32, (GLA_CHUNK, LANES), 1) & (GLA_CHUNK - 1)
    x = ri ^ ci
    level = jnp.zeros_like(x)
    for b in range(GLA_LEVELS):
        level = level + (x >= (1 << b)).astype(jnp.int32)
    return jnp.where(ci > ri, -1, level)


def _gla_chunk(q, k, v, g, st, level):
    w, t = _block_cumsums(g)
    n_slab = GQ // LANES

    def block_diag_t(m):
        r = m.shape[0]
        m2 = jnp.concatenate([m, m], axis=0).T
        same_head = ((lax.broadcasted_iota(jnp.int32, m2.shape, 0) < GLA_DK)
                     == (lax.broadcasted_iota(jnp.int32, m2.shape, 1) < r))
        return jnp.where(same_head, m2, jnp.zeros_like(m2))

    o = None
    if level is not None:
        a = [jnp.zeros((GLA_CHUNK, LANES), F32) for _ in range(n_slab)]
        for lv in range(GLA_LEVELS + 1):
            if lv == 0:
                qt, kt = q.astype(BF16), k.astype(BF16)
            else:
                qt = (q * jnp.exp2(w[lv - 1])).astype(BF16)
                kt = (k * jnp.exp2(t[lv - 1] - w[lv - 1])).astype(BF16)
            for s in range(n_slab):
                sl = slice(s * LANES, (s + 1) * LANES)
                scores = jnp.dot(qt[:, sl], block_diag_t(kt[:, sl]), preferred_element_type=F32)
                a[s] = jnp.where(level == lv, scores, a[s])

        qb = (q * jnp.exp2(w[GLA_LEVELS])).astype(BF16)
        st_b = st.astype(BF16)
        outs = []
        for s in range(n_slab):
            sl = slice(s * LANES, (s + 1) * LANES)
            v_ab = v[:, (2 * s) * GLA_DV:(2 * s + 2) * GLA_DV]
            first = lax.broadcasted_iota(jnp.int32, v_ab.shape, 1) < GLA_DV
            zero = jnp.zeros_like(v_ab)
            v_bd = jnp.concatenate([jnp.where(first, v_ab, zero), jnp.where(first, zero, v_ab)], axis=0)
            outs.append(jnp.dot(jnp.concatenate([a[s].astype(BF16), qb[:, sl]], axis=1),
                                jnp.concatenate([v_bd, block_diag_t(st_b[:, sl])], axis=0),
                                preferred_element_type=F32))
        o = jnp.concatenate(outs, axis=1)

    tot = t[GLA_LEVELS]
    ku = (k * jnp.exp2(tot - w[GLA_LEVELS])).astype(BF16)
    decayed = st * jnp.exp2(tot[0:1, :])
    lo = lax.broadcasted_iota(jnp.int32, (GLA_DV, LANES), 1) < GLA_DK
    new_st = []
    for s in range(n_slab):
        sl = slice(s * LANES, (s + 1) * LANES)
        upd = _tn_dot(v[:, (2 * s) * GLA_DV:(2 * s + 2) * GLA_DV], ku[:, sl])
        new_st.append(decayed[:, sl] + jnp.where(lo, upd[:GLA_DV], upd[GLA_DV:]))
    return o, jnp.concatenate(new_st, axis=1)


def _gla_gate(o, r, nw):
    out = []
    for h in range(GLA_HEADS):
        sl = slice(h * GLA_DV, (h + 1) * GLA_DV)
        rh = r[:, sl].astype(F32)
        out.append((_rms(o[:, sl], nw) * (rh * jax.nn.sigmoid(rh))).astype(BF16))
    return jnp.concatenate(out, axis=1)


def _gla_kernel(in_ref, g_ref, km_ref, vm_ref, gm_ref, nw_ref, o_ref, st_ref):
    @pl.when(pl.program_id(1) == 0)
    def _():
        st_ref[...] = jnp.zeros(st_ref.shape, F32)
        _, st0 = _gla_chunk(None, km_ref[...].astype(F32), vm_ref[...], gm_ref[...], st_ref[...], None)
        st_ref[...] = st0

    nw = nw_ref[...]
    level = _gla_levels()
    st = st_ref[...]
    for c in range(in_ref.shape[1] // GLA_CHUNK):
        rows = slice(c * GLA_CHUNK, (c + 1) * GLA_CHUNK)
        o, st = _gla_chunk(in_ref[0, rows, 0:GQ].astype(F32), in_ref[0, rows, GQ:2 * GQ].astype(F32),
                           in_ref[0, rows, 2 * GQ:2 * GQ + GV], g_ref[0, rows, :], st, level)
        o_ref[0, rows, :] = _gla_gate(o, in_ref[0, rows, 2 * GQ + GV:2 * GQ + 2 * GV], nw)
    st_ref[...] = st


def _gla_mixer(gla_in, g, km, vm, gm, gla_norm_w, rows):
    b, s, _ = gla_in.shape

    def seq_spec(width):
        return pl.BlockSpec((1, rows, width), lambda i, j: (i, j, 0))

    def full(a):
        return pl.BlockSpec(a.shape, lambda i, j: (0,) * a.ndim)

    return pl.pallas_call(
        _gla_kernel,
        grid=(b, s // rows),
        in_specs=[seq_spec(gla_in.shape[-1]), seq_spec(GQ), full(km), full(vm), full(gm), full(gla_norm_w)],
        out_specs=seq_spec(GV),
        out_shape=jax.ShapeDtypeStruct((b, s, GV), BF16),
        scratch_shapes=[pltpu.VMEM((GLA_DV, GQ), F32)],
        compiler_params=pltpu.CompilerParams(dimension_semantics=("arbitrary", "arbitrary"),
                                             vmem_limit_bytes=VMEM_LIMIT),
        name="gla_mixer",
    )(gla_in, g, km, vm, gm, gla_norm_w)


def _swa_block(sink_ref, q, kc, kp, vc, vp, km, vm, has_prev):
    n_win = SWA_BLOCK + SWA_HALF
    n_keys = n_win + N_META
    r = lax.broadcasted_iota(jnp.int32, (SWA_HALF, n_keys), 0)
    c = lax.broadcasted_iota(jnp.int32, (SWA_HALF, n_keys), 1)
    in_band = (c > r) & (c <= r + SWA_BLOCK)
    lane = lax.broadcasted_iota(jnp.int32, (SWA_HALF, LANES), 1)
    low = lane < SWA_HD

    halves = []
    for half in range(2):
        rows = slice(half * SWA_HALF, (half + 1) * SWA_HALF)
        if half == 0:
            k_all = jnp.concatenate([kp, kc[:SWA_HALF], km], axis=0)
            v_all = jnp.concatenate([vp, vc[:SWA_HALF], vm], axis=0)
            n_prev = SWA_BLOCK
        else:
            k_all = jnp.concatenate([kp[SWA_HALF:], kc, km], axis=0)
            v_all = jnp.concatenate([vp[SWA_HALF:], vc, vm], axis=0)
            n_prev = SWA_HALF
        valid = (c >= n_win) | in_band
        if has_prev is not None:
            valid = (c >= n_win) | (in_band & (has_prev | (c >= n_prev)))
        bias = jnp.where(valid, 0.0, -jnp.inf)
        pieces = []
        for t in range(SWA_HEADS // 2):
            qt = q[rows, t * LANES:(t + 1) * LANES]
            pieces += [jnp.where(low, qt, jnp.zeros_like(qt)), jnp.where(low, jnp.zeros_like(qt), qt)]
        q_all = jnp.concatenate(pieces, axis=0)
        s_all = _nt_dot(q_all, k_all)
        p, denom = [], []
        for i in range(SWA_HEADS):
            s = s_all[i * SWA_HALF:(i + 1) * SWA_HALF] + bias
            sink = sink_ref[(i // 2) + (i % 2) * (SWA_HEADS // 2)] * LOG2_E
            m = jnp.maximum(jnp.max(s, axis=-1, keepdims=True), sink)
            p.append(jnp.exp2(s - m))
            denom.append(jnp.sum(p[i], axis=-1, keepdims=True) + jnp.exp2(sink - m))
        o = jnp.dot(jnp.concatenate(p, axis=0).astype(BF16), v_all, preferred_element_type=F32)
        slabs = []
        for t in range(SWA_HEADS // 2):
            lo = o[(2 * t) * SWA_HALF:(2 * t + 1) * SWA_HALF] / denom[2 * t]
            hi = o[(2 * t + 1) * SWA_HALF:(2 * t + 2) * SWA_HALF] / denom[2 * t + 1]
            slabs.append(jnp.where(low, lo, hi).astype(BF16))
        halves.append(jnp.concatenate(slabs, axis=1))
    return jnp.concatenate(halves, axis=0)


def _swa_ffn_kernel(tiles_per_seq, sink_ref, swa0_ref, swa_ref, prev_ref, km_ref, vm_ref,
                    x_ref, og_ref, wo_ref, nf_ref, w1_ref, w2_ref, nl_ref, y_ref, os_ref):
    s = pl.program_id(0)
    ahead = jnp.minimum(s + 1, pl.num_programs(0) - 1)
    seq_start = lax.rem(ahead, tiles_per_seq) == 0
    cur = lax.rem(s, 2)
    k_lanes, v_lanes = slice(SQ, SQ + SKV), slice(SQ + SKV, SQ + 2 * SKV)

    def swa_tile_block(j, slot, tile_ref, first_prev):
        rows = slice(j * SWA_BLOCK, (j + 1) * SWA_BLOCK)
        if j == 0:
            kp, vp, has_prev = first_prev
        else:
            before = slice((j - 1) * SWA_BLOCK, j * SWA_BLOCK)
            kp, vp, has_prev = tile_ref[before, k_lanes], tile_ref[before, v_lanes], None
        os_ref[slot, rows, :] = _swa_block(sink_ref, tile_ref[rows, 0:SQ], tile_ref[rows, k_lanes], kp,
                                           tile_ref[rows, v_lanes], vp, km_ref[...], vm_ref[...], has_prev)

    n_blocks = x_ref.shape[0] // SWA_BLOCK

    @pl.when(s == 0)
    def _():
        stand_in = (swa0_ref[0:SWA_BLOCK, k_lanes], swa0_ref[0:SWA_BLOCK, v_lanes], s > 0)
        for j in range(n_blocks):
            swa_tile_block(j, 0, swa0_ref, stand_in)

    n_ff = D_FF // FF_CHUNK
    swa_per_ff = n_blocks // n_ff

    h = (x_ref[...]
         + jnp.dot(og_ref[...], wo_ref[0:GV, :], preferred_element_type=F32)
         + jnp.dot(os_ref[cur], wo_ref[GV:GV + SQ, :], preferred_element_type=F32))
    f = _rms(h, nf_ref[...]).astype(BF16)
    ff = None
    for c in range(n_ff):
        sl = slice(c * FF_CHUNK, (c + 1) * FF_CHUNK)
        a = jnp.maximum(jnp.dot(f, w1_ref[:, sl], preferred_element_type=F32), 0.0)
        for j in range(c * swa_per_ff, (c + 1) * swa_per_ff):
            swa_tile_block(j, 1 - cur, swa_ref,
                           (prev_ref[:, 0:SKV], prev_ref[:, SKV:], jnp.logical_not(seq_start)))
        d = jnp.dot((a * a).astype(BF16), w2_ref[sl, :], preferred_element_type=F32)
        ff = d if ff is None else ff + d
    y_ref[...] = _rms(h + ff, nl_ref[...])


def _swa_ffn(sinks, swa_in, km, vm, x2, og, wo, nf, w1, w2, nl, rows, tiles_per_seq):
    n = x2.shape[0]
    n_tiles = n // rows
    prev_per_tile = rows // SWA_BLOCK
    width = swa_in.shape[1]

    first_spec = pl.BlockSpec((rows, width), lambda i: (0, 0), pipeline_mode=pl.Buffered(1))
    ahead_spec = pl.BlockSpec((rows, width), lambda i: (jnp.minimum(i + 1, n_tiles - 1), 0))
    prev_spec = pl.BlockSpec((SWA_BLOCK, 2 * SKV),
                             lambda i: (jnp.minimum(i + 1, n_tiles - 1) * prev_per_tile - 1, SQ // (2 * SKV)))

    def ffn_spec(width):
        return pl.BlockSpec((rows, width), lambda i: (i, 0))

    def full(a):
        return pl.BlockSpec(a.shape, lambda i: (0,) * a.ndim, pipeline_mode=pl.Buffered(1))

    weights = (wo, nf, w1, w2, nl)
    return pl.pallas_call(
        functools.partial(_swa_ffn_kernel, tiles_per_seq),
        grid=(n_tiles,),
        in_specs=([pl.BlockSpec(memory_space=pltpu.SMEM), first_spec, ahead_spec, prev_spec,
                   full(km), full(vm), ffn_spec(D_MODEL), ffn_spec(GV)] + [full(a) for a in weights]),
        out_specs=ffn_spec(D_MODEL),
        out_shape=jax.ShapeDtypeStruct((n, D_MODEL), F32),
        scratch_shapes=[pltpu.VMEM((2, rows, SQ), BF16)],
        compiler_params=pltpu.CompilerParams(dimension_semantics=("arbitrary",),
                                             vmem_limit_bytes=VMEM_LIMIT),
        name="swa_ffn",
    )(sinks, swa_in, swa_in, swa_in, km, vm, x2, og, *weights)


def _rope_cos_sin(first_pos, n_pos):
    inv_freq = 1.0 / (ROPE_THETA ** (jnp.arange(0, ROPE_DIM, 2, dtype=F32) / ROPE_DIM))
    ang = jnp.arange(first_pos, first_pos + n_pos, dtype=jnp.int32).astype(F32)[:, None] * inv_freq[None, :]
    return jnp.concatenate([jnp.cos(ang), jnp.sin(ang)], axis=1)


def _rope_spread():
    half = ROPE_DIM // 2
    lane = jnp.arange(2 * LANES)
    d = lane % SWA_HD
    which = lane // LANES
    src = which * half + d % half
    spread = jnp.where((d < ROPE_DIM)[None, :] & (jnp.arange(2 * half)[:, None] == src[None, :]), 1.0, 0.0)
    unrotated = jnp.where((which == 0) & (d >= ROPE_DIM), 1.0, 0.0)
    return jnp.concatenate([spread, spread], axis=0).astype(BF16), unrotated[None, :].astype(F32)


def _swa_head_order():
    order = []
    for t in range(SWA_HEADS // 2):
        order += [t, t + SWA_HEADS // 2]
    return order


def kernel(x, meta_tokens, norm_mix_w, w_in, w_gate_up, b_gate, gla_norm_w, sinks, w_out, norm_ff_w,
           w_ff1, w_ff2, final_norm_w):
    b, s, d = x.shape
    assert d == D_MODEL and s % ROWS_IN == 0 and s % ROWS_OUT == 0 and w_in.shape[0] == 1
    assert s % ROWS_GLA == 0 and ROWS_GLA % GLA_CHUNK == 0
    assert ROWS_OUT % (SWA_BLOCK * (D_FF // FF_CHUNK)) == 0

    bg = b_gate[0].reshape(1, GQ)
    nw = norm_mix_w[0].reshape(1, D_MODEL)
    cq = jnp.full((1, LANES), SWA_HD ** -0.5 * LOG2_E, F32)

    spread, unrot = _rope_spread()
    meta_tables = (_rope_cos_sin(0, N_META), spread, unrot)
    tok_tables = (_rope_cos_sin(N_META, s), spread, unrot)

    w_all, wg, km, vm, gm, m_sk, m_sv = _meta_prep(meta_tokens.astype(F32), w_in[0].T, w_gate_up[0],
                                                   (nw, bg, cq), meta_tables)
    small = (nw, wg, bg, cq)

    x2 = x.reshape(b * s, d)
    gla_in, g, swa_in, w1, w2, wo = _in_projection(x2, tok_tables, w_all, small,
                                                   w_ff1[0], w_ff2[0], w_out[0], ROWS_IN)

    def seq(a):
        return a.reshape(b, s, a.shape[-1])

    o_gla = _gla_mixer(seq(gla_in), seq(g), km, vm, gm, gla_norm_w[0].reshape(1, GLA_DV), ROWS_GLA)
    y = _swa_ffn(sinks[0].astype(F32), swa_in, m_sk, m_sv, x2, o_gla.reshape(b * s, GV), wo,
                 norm_ff_w[0].reshape(1, D_MODEL), w1, w2, final_norm_w.reshape(1, D_MODEL),
                 ROWS_OUT, s // ROWS_OUT)
    return y.reshape(b, s, d)
```

```python
import functools

import jax
import jax.numpy as jnp
from jax import lax
from jax.experimental import pallas as pl
from jax.experimental.pallas import tpu as pltpu

F32 = jnp.float32
BF16 = jnp.bfloat16

D_MODEL = 1024
N_META = 16
GLA_HEADS = 4
GLA_DK = 64
GLA_DV = 128
GLA_RANK = 16
GLA_TAU = 16.0
GLA_CHUNK = 64
GLA_LEVELS = 6
SWA_HEADS = 8
SWA_KV_HEADS = 2
SWA_HD = 64
SWA_BLOCK = 128
SWA_HALF = SWA_BLOCK // 2
ROPE_DIM = 16
ROPE_THETA = 500000.0
D_FF = 4096
EPS = 1e-5
LOG2_E = 1.4426950408889634

LANES = 128
GQ = GLA_HEADS * GLA_DK
GV = GLA_HEADS * GLA_DV
SQ = SWA_HEADS * SWA_HD
SKV = SWA_KV_HEADS * SWA_HD
FF_CHUNK = 1024

ROWS_IN = 1024
ROWS_IN_SUB = 256
W_SQ = 2 * GQ + 2 * GV
W_SK = W_SQ + SQ
W_LR = W_SK + 2 * SKV
ROWS_GLA = 2048
ROWS_OUT = 512
VMEM_LIMIT = 56 * 1024 * 1024


def _nt_dot(a, b):
    return lax.dot_general(a, b, (((1,), (1,)), ((), ())), preferred_element_type=F32)


def _tn_dot(a, b):
    return lax.dot_general(a, b, (((0,), (0,)), ((), ())), preferred_element_type=F32)


def _rms(x, w):
    return x * lax.rsqrt(jnp.mean(x * x, axis=-1, keepdims=True) + EPS) * w


def _project(normed, n_rows, w_ref, wg_ref, bg_ref, cq_ref, cs_ref, spread_ref, unrot_ref, gla_ref, g_ref, swa_ref,
             after=None):
    half = ROPE_DIM // 2
    n_sub = n_rows // ROWS_IN_SUB if n_rows > ROWS_IN_SUB else 1
    sub = n_rows // n_sub
    for i in range(n_sub):
        rows = slice(i * sub, (i + 1) * sub)
        u = normed(rows)

        def proj(lo, width):
            return jnp.dot(u, w_ref[:, lo:lo + width], preferred_element_type=F32)


        cs = cs_ref[rows, :]
        hi = cs.astype(BF16)
        lo = (cs - hi.astype(F32)).astype(BF16)
        pat = jnp.dot(jnp.concatenate([hi, lo], axis=1), spread_ref[...], preferred_element_type=F32) + unrot_ref[...]
        cos, sin = pat[:, 0:LANES], pat[:, LANES:]
        d = lax.broadcasted_iota(jnp.int32, cos.shape, 1) & (SWA_HD - 1)

        def rope(t):
            ts = t * sin
            turned = jnp.where(d < half, -pltpu.roll(ts, LANES - half, axis=1),
                               jnp.where(d < ROPE_DIM, pltpu.roll(ts, half, axis=1), 0.0))
            return t * cos + turned

        sq = proj(W_SQ, SQ)
        for s in range(SQ // LANES):
            sl = slice(s * LANES, (s + 1) * LANES)
            swa_ref[rows, sl] = (rope(sq[:, sl]) * cq_ref[...]).astype(BF16)
        kv = proj(W_SK, 2 * SKV)
        swa_ref[rows, SQ:SQ + SKV] = rope(kv[:, :SKV]).astype(BF16)
        swa_ref[rows, SQ + SKV:SQ + 2 * SKV] = kv[:, SKV:].astype(BF16)

        lr = proj(W_LR, LANES).astype(BF16)
        z = jnp.dot(lr, wg_ref[...], preferred_element_type=F32) + bg_ref[...]
        g_ref[rows, :] = (jnp.minimum(z, 0.0) - jnp.log1p(jnp.exp(-jnp.abs(z)))) * (LOG2_E / GLA_TAU)

        qk = proj(0, 2 * GQ)
        gla_ref[rows, 0:GQ] = (qk[:, :GQ] * (GLA_DK ** -0.5)).astype(BF16)
        gla_ref[rows, GQ:2 * GQ] = qk[:, GQ:].astype(BF16)
        gla_ref[rows, 2 * GQ:2 * GQ + GV] = proj(2 * GQ, GV).astype(BF16)
        gla_ref[rows, 2 * GQ + GV:2 * GQ + 2 * GV] = proj(2 * GQ + GV, GV).astype(BF16)
        if after is not None:
            after(rows)


PROJ_WIDTHS = (2 * GQ + 2 * GV, GQ, SQ + 2 * SKV)
PROJ_DTYPES = (BF16, F32, BF16)


def _meta_prep_kernel(x_ref, wt_ref, nw_ref, wgu_ref, bg_ref, cq_ref, cs_ref, spread_ref, unrot_ref,
                      w_ref, wg_ref, km_ref, vm_ref, gm_ref, sk_ref, sv_ref, gla_s, g_s, swa_s):
    wg_ref[0:GLA_RANK, :] = wgu_ref[...].astype(BF16)
    wg_ref[GLA_RANK:, :] = jnp.zeros((LANES - GLA_RANK, GQ), BF16)
    o_lr = 2 * GQ + 2 * GV
    o_sq = o_lr + GLA_RANK
    o_sk = o_sq + SQ
    half = SWA_HEADS // 2

    def rows(lo, n):
        return wt_ref[lo:lo + n, :].astype(BF16)

    slabs = [rows(c * LANES, LANES) for c in range(o_lr // LANES)]
    slabs += [jnp.concatenate([rows(o_sq + t * SWA_HD, SWA_HD), rows(o_sq + (t + half) * SWA_HD, SWA_HD)], axis=0)
              for t in range(half)]
    slabs += [rows(o_sk + c * LANES, LANES) for c in range(2 * SKV // LANES)]
    slabs += [jnp.concatenate([rows(o_lr, GLA_RANK), jnp.zeros((LANES - GLA_RANK, D_MODEL), BF16)], axis=0)]
    for c, slab in enumerate(slabs):
        w_ref[:, c * LANES:(c + 1) * LANES] = slab.T
    _project(lambda rows: _rms(x_ref[rows, :], nw_ref[...]).astype(BF16), x_ref.shape[0],
             w_ref, wg_ref, bg_ref, cq_ref, cs_ref, spread_ref, unrot_ref, gla_s, g_s, swa_s)
    pad = GLA_CHUNK - N_META
    for ref, val in ((km_ref, gla_s[:, GQ:2 * GQ]), (vm_ref, gla_s[:, 2 * GQ:2 * GQ + GV]), (gm_ref, g_s[...])):
        ref[0:pad, :] = jnp.zeros((pad, ref.shape[1]), ref.dtype)
        ref[pad:, :] = val
    sk_ref[...] = swa_s[:, SQ:SQ + SKV]
    sv_ref[...] = swa_s[:, SQ + SKV:]


def _meta_prep(meta_tokens, w_in_t, w_gate_up, small, rope):
    n = meta_tokens.shape[0]
    nw, bg, cq = small
    out_shape = [jax.ShapeDtypeStruct((D_MODEL, W_LR + LANES), BF16), jax.ShapeDtypeStruct((LANES, GQ), BF16),
                 jax.ShapeDtypeStruct((GLA_CHUNK, GQ), BF16), jax.ShapeDtypeStruct((GLA_CHUNK, GV), BF16),
                 jax.ShapeDtypeStruct((GLA_CHUNK, GQ), F32),
                 jax.ShapeDtypeStruct((n, SKV), BF16), jax.ShapeDtypeStruct((n, SKV), BF16)]
    return pl.pallas_call(
        _meta_prep_kernel,
        out_shape=out_shape,
        scratch_shapes=[pltpu.VMEM((n, w), d) for w, d in zip(PROJ_WIDTHS, PROJ_DTYPES)],
        compiler_params=pltpu.CompilerParams(vmem_limit_bytes=VMEM_LIMIT),
        name="meta_prep",
    )(meta_tokens, w_in_t, nw, w_gate_up, bg, cq, *rope)


def _inproj_kernel(x0_ref, xa_ref, nw_ref, w_ref, wg_ref, bg_ref, cq_ref, cs_ref, spread_ref, unrot_ref,
                   w1_ref, w2_ref, wo_ref, *refs):
    *proj_refs, w1b_ref, w2b_ref, wob_ref, u_ref = refs

    @pl.when(pl.program_id(0) == 0)
    def _():
        u_ref[...] = _rms(x0_ref[...], nw_ref[...]).astype(BF16)

    def norm_ahead(rows):
        u_ref[rows, :] = _rms(xa_ref[rows, :], nw_ref[...]).astype(BF16)

    _project(lambda rows: u_ref[rows, :], xa_ref.shape[0], w_ref, wg_ref, bg_ref, cq_ref,
             cs_ref, spread_ref, unrot_ref, *proj_refs, after=norm_ahead)
    w1b_ref[...] = w1_ref[...].astype(BF16)
    w2b_ref[...] = w2_ref[...].astype(BF16)
    wob_ref[...] = wo_ref[...].astype(BF16)


def _in_projection(x2, rope, w_all, small, w1, w2, wo, rows):
    n = x2.shape[0]
    steps = n // rows
    cs, spread, unrot = rope
    nblk_pos = cs.shape[0] // rows
    assert w1.shape[0] % steps == 0 and w2.shape[0] % steps == 0 and wo.shape[0] == steps * SWA_HD

    def row_spec(width):
        return pl.BlockSpec((rows, width), lambda i: (i, 0))

    def full(a):
        return pl.BlockSpec(a.shape, lambda i: (0,) * a.ndim)

    def slice_spec(a):
        return pl.BlockSpec((a.shape[0] // steps, a.shape[1]), lambda i: (i, 0))

    n_gla = GV // SWA_HD
    half = SWA_HEADS // 2

    def wo_src(i):
        j = i - n_gla
        return (jnp.where(j < 0, i, n_gla + j // 2 + half * (j % 2)), 0)

    x_first = pl.BlockSpec((rows, D_MODEL), lambda i: (0, 0), pipeline_mode=pl.Buffered(1))
    x_ahead = pl.BlockSpec((rows, D_MODEL), lambda i: (jnp.minimum(i + 1, steps - 1), 0))
    in_specs = ([x_first, x_ahead, full(small[0]), full(w_all)] + [full(a) for a in small[1:]]
                + [pl.BlockSpec((rows, cs.shape[1]), lambda i: (i % nblk_pos, 0)), full(spread), full(unrot),
                   slice_spec(w1), slice_spec(w2), pl.BlockSpec((SWA_HD, wo.shape[1]), wo_src)])
    weights_out = [jax.ShapeDtypeStruct(a.shape, BF16) for a in (w1, w2, wo)]
    return pl.pallas_call(
        _inproj_kernel,
        grid=(steps,),
        in_specs=in_specs,
        out_specs=[row_spec(w) for w in PROJ_WIDTHS] + [slice_spec(w1), slice_spec(w2), slice_spec(wo)],
        out_shape=[jax.ShapeDtypeStruct((n, w), d) for w, d in zip(PROJ_WIDTHS, PROJ_DTYPES)] + weights_out,
        scratch_shapes=[pltpu.VMEM((rows, D_MODEL), BF16)],
        compiler_params=pltpu.CompilerParams(dimension_semantics=("arbitrary",),
                                             vmem_limit_bytes=VMEM_LIMIT),
        name="in_projection",
    )(x2, x2, small[0], w_all, *small[1:], cs, spread, unrot, w1, w2, wo)


def _block_cumsums(g):
    rows = g.shape[0]
    row = lax.broadcasted_iota(jnp.int32, g.shape, 0)
    w, t = [g], [g]
    for k in range(GLA_LEVELS):
        s = 1 << k
        upper = (row & s) != 0
        below = pltpu.roll(t[k], s, axis=0)
        above = pltpu.roll(t[k], rows - s, axis=0)
        w.append(w[k] + jnp.where(upper, below, 0.0))
        t.append(t[k] + jnp.where(upper, below, above))
    return w, t


def _gla_levels():
    ri = lax.broadcasted_iota(jnp.int32, (GLA_CHUNK, LANES), 0)
    ci = lax.broadcasted_iota(jnp.int32, (GLA_CHUNK, LANES), 1) & (GLA_CHUNK - 1)
    x = ri ^ ci
    level = jnp.zeros_like(x)
    for b in range(GLA_LEVELS):
        level = level + (x >= (1 << b)).astype(jnp.int32)
    return jnp.where(ci > ri, -1, level)


def _gla_chunk(q, k, v, g, st, level):
    w, t = _block_cumsums(g)
    n_slab = GQ // LANES

    def block_diag_t(m):
        r = m.shape[0]
        m2 = jnp.concatenate([m, m], axis=0).T
        same_head = ((lax.broadcasted_iota(jnp.int32, m2.shape, 0) < GLA_DK)
                     == (lax.broadcasted_iota(jnp.int32, m2.shape, 1) < r))
        return jnp.where(same_head, m2, jnp.zeros_like(m2))

    o = None
    if level is not None:
        a = [jnp.zeros((GLA_CHUNK, LANES), F32) for _ in range(n_slab)]
        for lv in range(GLA_LEVELS + 1):
            if lv == 0:
                qt, kt = q.astype(BF16), k.astype(BF16)
            else:
                qt = (q * jnp.exp2(w[lv - 1])).astype(BF16)
                kt = (k * jnp.exp2(t[lv - 1] - w[lv - 1])).astype(BF16)
            for s in range(n_slab):
                sl = slice(s * LANES, (s + 1) * LANES)
                scores = jnp.dot(qt[:, sl], block_diag_t(kt[:, sl]), preferred_element_type=F32)
                a[s] = jnp.where(level == lv, scores, a[s])

        qb = (q * jnp.exp2(w[GLA_LEVELS])).astype(BF16)
        st_b = st.astype(BF16)
        outs = []
        for s in range(n_slab):
            sl = slice(s * LANES, (s + 1) * LANES)
            v_ab = v[:, (2 * s) * GLA_DV:(2 * s + 2) * GLA_DV]
            first = lax.broadcasted_iota(jnp.int32, v_ab.shape, 1) < GLA_DV
            zero = jnp.zeros_like(v_ab)
            v_bd = jnp.concatenate([jnp.where(first, v_ab, zero), jnp.where(first, zero, v_ab)], axis=0)
            outs.append(jnp.dot(jnp.concatenate([a[s].astype(BF16), qb[:, sl]], axis=1),
                                jnp.concatenate([v_bd, block_diag_t(st_b[:, sl])], axis=0),
                                preferred_element_type=F32))
        o = jnp.concatenate(outs, axis=1)

    tot = t[GLA_LEVELS]
    ku = (k * jnp.exp2(tot - w[GLA_LEVELS])).astype(BF16)
    decayed = st * jnp.exp2(tot[0:1, :])
    lo = lax.broadcasted_iota(jnp.int32, (GLA_DV, LANES), 1) < GLA_DK
    new_st = []
    for s in range(n_slab):
        sl = slice(s * LANES, (s + 1) * LANES)
        upd = _tn_dot(v[:, (2 * s) * GLA_DV:(2 * s + 2) * GLA_DV], ku[:, sl])
        new_st.append(decayed[:, sl] + jnp.where(lo, upd[:GLA_DV], upd[GLA_DV:]))
    return o, jnp.concatenate(new_st, axis=1)


def _gla_gate(o, r, nw):
    out = []
    for h in range(GLA_HEADS):
        sl = slice(h * GLA_DV, (h + 1) * GLA_DV)
        rh = r[:, sl].astype(F32)
        out.append((_rms(o[:, sl], nw) * (rh * jax.nn.sigmoid(rh))).astype(BF16))
    return jnp.concatenate(out, axis=1)


def _gla_kernel(in_ref, g_ref, km_ref, vm_ref, gm_ref, nw_ref, o_ref, st_ref):
    @pl.when(pl.program_id(1) == 0)
    def _():
        st_ref[...] = jnp.zeros(st_ref.shape, F32)
        _, st0 = _gla_chunk(None, km_ref[...].astype(F32), vm_ref[...], gm_ref[...], st_ref[...], None)
        st_ref[...] = st0

    nw = nw_ref[...]
    level = _gla_levels()
    st = st_ref[...]
    for c in range(in_ref.shape[1] // GLA_CHUNK):
        rows = slice(c * GLA_CHUNK, (c + 1) * GLA_CHUNK)
        o, st = _gla_chunk(in_ref[0, rows, 0:GQ].astype(F32), in_ref[0, rows, GQ:2 * GQ].astype(F32),
                           in_ref[0, rows, 2 * GQ:2 * GQ + GV], g_ref[0, rows, :], st, level)
        o_ref[0, rows, :] = _gla_gate(o, in_ref[0, rows, 2 * GQ + GV:2 * GQ + 2 * GV], nw)
    st_ref[...] = st


def _gla_mixer(gla_in, g, km, vm, gm, gla_norm_w, rows):
    b, s, _ = gla_in.shape

    def seq_spec(width):
        return pl.BlockSpec((1, rows, width), lambda i, j: (i, j, 0))

    def full(a):
        return pl.BlockSpec(a.shape, lambda i, j: (0,) * a.ndim)

    return pl.pallas_call(
        _gla_kernel,
        grid=(b, s // rows),
        in_specs=[seq_spec(gla_in.shape[-1]), seq_spec(GQ), full(km), full(vm), full(gm), full(gla_norm_w)],
        out_specs=seq_spec(GV),
        out_shape=jax.ShapeDtypeStruct((b, s, GV), BF16),
        scratch_shapes=[pltpu.VMEM((GLA_DV, GQ), F32)],
        compiler_params=pltpu.CompilerParams(dimension_semantics=("arbitrary", "arbitrary"),
                                             vmem_limit_bytes=VMEM_LIMIT),
        name="gla_mixer",
    )(gla_in, g, km, vm, gm, gla_norm_w)


def _swa_block(sink_ref, q, kc, kp, vc, vp, km, vm, has_prev):
    n_win = SWA_BLOCK + SWA_HALF
    n_keys = n_win + N_META
    r = lax.broadcasted_iota(jnp.int32, (SWA_HALF, n_keys), 0)
    c = lax.broadcasted_iota(jnp.int32, (SWA_HALF, n_keys), 1)
    in_band = (c > r) & (c <= r + SWA_BLOCK)
    lane = lax.broadcasted_iota(jnp.int32, (SWA_HALF, LANES), 1)
    low = lane < SWA_HD

    halves = []
    for half in range(2):
        rows = slice(half * SWA_HALF, (half + 1) * SWA_HALF)
        if half == 0:
            k_all = jnp.concatenate([kp, kc[:SWA_HALF], km], axis=0)
            v_all = jnp.concatenate([vp, vc[:SWA_HALF], vm], axis=0)
            n_prev = SWA_BLOCK
        else:
            k_all = jnp.concatenate([kp[SWA_HALF:], kc, km], axis=0)
            v_all = jnp.concatenate([vp[SWA_HALF:], vc, vm], axis=0)
            n_prev = SWA_HALF
        valid = (c >= n_win) | in_band
        if has_prev is not None:
            valid = (c >= n_win) | (in_band & (has_prev | (c >= n_prev)))
        bias = jnp.where(valid, 0.0, -jnp.inf)
        pieces = []
        for t in range(SWA_HEADS // 2):
            qt = q[rows, t * LANES:(t + 1) * LANES]
            pieces += [jnp.where(low, qt, jnp.zeros_like(qt)), jnp.where(low, jnp.zeros_like(qt), qt)]
        q_all = jnp.concatenate(pieces, axis=0)
        s_all = _nt_dot(q_all, k_all)
        p, denom = [], []
        for i in range(SWA_HEADS):
            s = s_all[i * SWA_HALF:(i + 1) * SWA_HALF] + bias
            sink = sink_ref[(i // 2) + (i % 2) * (SWA_HEADS // 2)] * LOG2_E
            m = jnp.maximum(jnp.max(s, axis=-1, keepdims=True), sink)
            p.append(jnp.exp2(s - m))
            denom.append(jnp.sum(p[i], axis=-1, keepdims=True) + jnp.exp2(sink - m))
        o = jnp.dot(jnp.concatenate(p, axis=0).astype(BF16), v_all, preferred_element_type=F32)
        slabs = []
        for t in range(SWA_HEADS // 2):
            lo = o[(2 * t) * SWA_HALF:(2 * t + 1) * SWA_HALF] / denom[2 * t]
            hi = o[(2 * t + 1) * SWA_HALF:(2 * t + 2) * SWA_HALF] / denom[2 * t + 1]
            slabs.append(jnp.where(low, lo, hi).astype(BF16))
        halves.append(jnp.concatenate(slabs, axis=1))
    return jnp.concatenate(halves, axis=0)


def _swa_ffn_kernel(tiles_per_seq, sink_ref, swa0_ref, swa_ref, prev_ref, km_ref, vm_ref,
                    x_ref, og_ref, wo_ref, nf_ref, w1_ref, w2_ref, nl_ref, y_ref, os_ref):
    s = pl.program_id(0)
    ahead = jnp.minimum(s + 1, pl.num_programs(0) - 1)
    seq_start = lax.rem(ahead, tiles_per_seq) == 0
    cur = lax.rem(s, 2)
    k_lanes, v_lanes = slice(SQ, SQ + SKV), slice(SQ + SKV, SQ + 2 * SKV)

    def swa_tile_block(j, slot, tile_ref, first_prev):
        rows = slice(j * SWA_BLOCK, (j + 1) * SWA_BLOCK)
        if j == 0:
            kp, vp, has_prev = first_prev
        else:
            before = slice((j - 1) * SWA_BLOCK, j * SWA_BLOCK)
            kp, vp, has_prev = tile_ref[before, k_lanes], tile_ref[before, v_lanes], None
        os_ref[slot, rows, :] = _swa_block(sink_ref, tile_ref[rows, 0:SQ], tile_ref[rows, k_lanes], kp,
                                           tile_ref[rows, v_lanes], vp, km_ref[...], vm_ref[...], has_prev)

    n_blocks = x_ref.shape[0] // SWA_BLOCK

    @pl.when(s == 0)
    def _():
        stand_in = (swa0_ref[0:SWA_BLOCK, k_lanes], swa0_ref[0:SWA_BLOCK, v_lanes], s > 0)
        for j in range(n_blocks):
            swa_tile_block(j, 0, swa0_ref, stand_in)

    n_ff = D_FF // FF_CHUNK
    swa_per_ff = n_blocks // n_ff

    h = (x_ref[...]
         + jnp.dot(og_ref[...], wo_ref[0:GV, :], preferred_element_type=F32)
         + jnp.dot(os_ref[cur], wo_ref[GV:GV + SQ, :], preferred_element_type=F32))
    f = _rms(h, nf_ref[...]).astype(BF16)
    ff = None
    for c in range(n_ff):
        sl = slice(c * FF_CHUNK, (c + 1) * FF_CHUNK)
        a = jnp.maximum(jnp.dot(f, w1_ref[:, sl], preferred_element_type=F32), 0.0)
        for j in range(c * swa_per_ff, (c + 1) * swa_per_ff):
            swa_tile_block(j, 1 - cur, swa_ref,
                           (prev_ref[:, 0:SKV], prev_ref[:, SKV:], jnp.logical_not(seq_start)))
        d = jnp.dot((a * a).astype(BF16), w2_ref[sl, :], preferred_element_type=F32)
        ff = d if ff is None else ff + d
    y_ref[...] = _rms(h + ff, nl_ref[...])


def _swa_ffn(sinks, swa_in, km, vm, x2, og, wo, nf, w1, w2, nl, rows, tiles_per_seq):
    n = x2.shape[0]
    n_tiles = n // rows
    prev_per_tile = rows // SWA_BLOCK
    width = swa_in.shape[1]

    first_spec = pl.BlockSpec((rows, width), lambda i: (0, 0), pipeline_mode=pl.Buffered(1))
    ahead_spec = pl.BlockSpec((rows, width), lambda i: (jnp.minimum(i + 1, n_tiles - 1), 0))
    prev_spec = pl.BlockSpec((SWA_BLOCK, 2 * SKV),
                             lambda i: (jnp.minimum(i + 1, n_tiles - 1) * prev_per_tile - 1, SQ // (2 * SKV)))

    def ffn_spec(width):
        return pl.BlockSpec((rows, width), lambda i: (i, 0))

    def full(a):
        return pl.BlockSpec(a.shape, lambda i: (0,) * a.ndim, pipeline_mode=pl.Buffered(1))

    weights = (wo, nf, w1, w2, nl)
    return pl.pallas_call(
        functools.partial(_swa_ffn_kernel, tiles_per_seq),
        grid=(n_tiles,),
        in_specs=([pl.BlockSpec(memory_space=pltpu.SMEM), first_spec, ahead_spec, prev_spec,
                   full(km), full(vm), ffn_spec(D_MODEL), ffn_spec(GV)] + [full(a) for a in weights]),
        out_specs=ffn_spec(D_MODEL),
        out_shape=jax.ShapeDtypeStruct((n, D_MODEL), F32),
        scratch_shapes=[pltpu.VMEM((2, rows, SQ), BF16)],
        compiler_params=pltpu.CompilerParams(dimension_semantics=("arbitrary",),
                                             vmem_limit_bytes=VMEM_LIMIT),
        name="swa_ffn",
    )(sinks, swa_in, swa_in, swa_in, km, vm, x2, og, *weights)


def _rope_cos_sin(first_pos, n_pos):
    inv_freq = 1.0 / (ROPE_THETA ** (jnp.arange(0, ROPE_DIM, 2, dtype=F32) / ROPE_DIM))
    ang = jnp.arange(first_pos, first_pos + n_pos, dtype=jnp.int32).astype(F32)[:, None] * inv_freq[None, :]
    return jnp.concatenate([jnp.cos(ang), jnp.sin(ang)], axis=1)


def _rope_spread():
    half = ROPE_DIM // 2
    lane = jnp.arange(2 * LANES)
    d = lane % SWA_HD
    which = lane // LANES
    src = which * half + d % half
    spread = jnp.where((d < ROPE_DIM)[None, :] & (jnp.arange(2 * half)[:, None] == src[None, :]), 1.0, 0.0)
    unrotated = jnp.where((which == 0) & (d >= ROPE_DIM), 1.0, 0.0)
    return jnp.concatenate([spread, spread], axis=0).astype(BF16), unrotated[None, :].astype(F32)


def _swa_head_order():
    order = []
    for t in range(SWA_HEADS // 2):
        order += [t, t + SWA_HEADS // 2]
    return order


def kernel(x, meta_tokens, norm_mix_w, w_in, w_gate_up, b_gate, gla_norm_w, sinks, w_out, norm_ff_w,
           w_ff1, w_ff2, final_norm_w):
    b, s, d = x.shape
    assert d == D_MODEL and s % ROWS_IN == 0 and s % ROWS_OUT == 0 and w_in.shape[0] == 1
    assert s % ROWS_GLA == 0 and ROWS_GLA % GLA_CHUNK == 0
    assert ROWS_OUT % (SWA_BLOCK * (D_FF // FF_CHUNK)) == 0

    bg = b_gate[0].reshape(1, GQ)
    nw = norm_mix_w[0].reshape(1, D_MODEL)
    cq = jnp.full((1, LANES), SWA_HD ** -0.5 * LOG2_E, F32)

    spread, unrot = _rope_spread()
    meta_tables = (_rope_cos_sin(0, N_META), spread, unrot)
    tok_tables = (_rope_cos_sin(N_META, s), spread, unrot)

    w_all, wg, km, vm, gm, m_sk, m_sv = _meta_prep(meta_tokens.astype(F32), w_in[0].T, w_gate_up[0],
                                                   (nw, bg, cq), meta_tables)
    small = (nw, wg, bg, cq)

    x2 = x.reshape(b * s, d)
    gla_in, g, swa_in, w1, w2, wo = _in_projection(x2, tok_tables, w_all, small,
                                                   w_ff1[0], w_ff2[0], w_out[0], ROWS_IN)

    def seq(a):
        return a.reshape(b, s, a.shape[-1])

    o_gla = _gla_mixer(seq(gla_in), seq(g), km, vm, gm, gla_norm_w[0].reshape(1, GLA_DV), ROWS_GLA)
    y = _swa_ffn(sinks[0].astype(F32), swa_in, m_sk, m_sv, x2, o_gla.reshape(b * s, GV), wo,
                 norm_ff_w[0].reshape(1, D_MODEL), w1, w2, final_norm_w.reshape(1, D_MODEL),
                 ROWS_OUT, s // ROWS_OUT)
    return y.reshape(b, s, d)
```

```python
import functools

import jax
import jax.numpy as jnp
from jax import lax
from jax.experimental import pallas as pl
from jax.experimental.pallas import tpu as pltpu

F32 = jnp.float32
BF16 = jnp.bfloat16

D_MODEL = 1024
N_META = 16
GLA_HEADS = 4
GLA_DK = 64
GLA_DV = 128
GLA_RANK = 16
GLA_TAU = 16.0
GLA_CHUNK = 64
GLA_LEVELS = 6
SWA_HEADS = 8
SWA_KV_HEADS = 2
SWA_HD = 64
SWA_BLOCK = 128
SWA_HALF = SWA_BLOCK // 2
ROPE_DIM = 16
ROPE_THETA = 500000.0
D_FF = 4096
EPS = 1e-5
LOG2_E = 1.4426950408889634

LANES = 128
GQ = GLA_HEADS * GLA_DK
GV = GLA_HEADS * GLA_DV
SQ = SWA_HEADS * SWA_HD
SKV = SWA_KV_HEADS * SWA_HD
FF_CHUNK = 1024

ROWS_IN = 1024
ROWS_IN_SUB = 256
W_SQ = 2 * GQ + 2 * GV
W_SK = W_SQ + SQ
W_LR = W_SK + 2 * SKV
ROWS_GLA = 1024
ROWS_OUT = 512
VMEM_LIMIT = 56 * 1024 * 1024


def _nt_dot(a, b):
    return lax.dot_general(a, b, (((1,), (1,)), ((), ())), preferred_element_type=F32)


def _tn_dot(a, b):
    return lax.dot_general(a, b, (((0,), (0,)), ((), ())), preferred_element_type=F32)


def _rms(x, w):
    return x * lax.rsqrt(jnp.mean(x * x, axis=-1, keepdims=True) + EPS) * w


def _project(normed, n_rows, w_ref, wg_ref, bg_ref, cq_ref, cs_ref, spread_ref, unrot_ref, gla_ref, g_ref, swa_ref,
             after=None):
    half = ROPE_DIM // 2
    n_sub = n_rows // ROWS_IN_SUB if n_rows > ROWS_IN_SUB else 1
    sub = n_rows // n_sub
    for i in range(n_sub):
        rows = slice(i * sub, (i + 1) * sub)
        u = normed(rows)

        def proj(lo, width):
            return jnp.dot(u, w_ref[:, lo:lo + width], preferred_element_type=F32)


        cs = cs_ref[rows, :]
        hi = cs.astype(BF16)
        lo = (cs - hi.astype(F32)).astype(BF16)
        pat = jnp.dot(jnp.concatenate([hi, lo], axis=1), spread_ref[...], preferred_element_type=F32) + unrot_ref[...]
        cos, sin = pat[:, 0:LANES], pat[:, LANES:]
        d = lax.broadcasted_iota(jnp.int32, cos.shape, 1) & (SWA_HD - 1)

        def rope(t):
            ts = t * sin
            turned = jnp.where(d < half, -pltpu.roll(ts, LANES - half, axis=1),
                               jnp.where(d < ROPE_DIM, pltpu.roll(ts, half, axis=1), 0.0))
            return t * cos + turned

        sq = proj(W_SQ, SQ)
        for s in range(SQ // LANES):
            sl = slice(s * LANES, (s + 1) * LANES)
            swa_ref[rows, sl] = (rope(sq[:, sl]) * cq_ref[...]).astype(BF16)
        kv = proj(W_SK, 2 * SKV)
        swa_ref[rows, SQ:SQ + SKV] = rope(kv[:, :SKV]).astype(BF16)
        swa_ref[rows, SQ + SKV:SQ + 2 * SKV] = kv[:, SKV:].astype(BF16)

        lr = proj(W_LR, LANES).astype(BF16)
        z = jnp.dot(lr, wg_ref[...], preferred_element_type=F32) + bg_ref[...]
        g_ref[rows, :] = (jnp.minimum(z, 0.0) - jnp.log1p(jnp.exp(-jnp.abs(z)))) * (LOG2_E / GLA_TAU)

        qk = proj(0, 2 * GQ)
        gla_ref[rows, 0:GQ] = (qk[:, :GQ] * (GLA_DK ** -0.5)).astype(BF16)
        gla_ref[rows, GQ:2 * GQ] = qk[:, GQ:].astype(BF16)
        gla_ref[rows, 2 * GQ:2 * GQ + GV] = proj(2 * GQ, GV).astype(BF16)
        gla_ref[rows, 2 * GQ + GV:2 * GQ + 2 * GV] = proj(2 * GQ + GV, GV).astype(BF16)
        if after is not None:
            after(rows)


PROJ_WIDTHS = (2 * GQ + 2 * GV, GQ, SQ + 2 * SKV)
PROJ_DTYPES = (BF16, F32, BF16)


def _meta_prep_kernel(x_ref, wt_ref, nw_ref, wgu_ref, bg_ref, cq_ref, cs_ref, spread_ref, unrot_ref,
                      w_ref, wg_ref, km_ref, vm_ref, gm_ref, sk_ref, sv_ref, gla_s, g_s, swa_s):
    wg_ref[0:GLA_RANK, :] = wgu_ref[...].astype(BF16)
    wg_ref[GLA_RANK:, :] = jnp.zeros((LANES - GLA_RANK, GQ), BF16)
    o_lr = 2 * GQ + 2 * GV
    o_sq = o_lr + GLA_RANK
    o_sk = o_sq + SQ
    half = SWA_HEADS // 2

    def rows(lo, n):
        return wt_ref[lo:lo + n, :].astype(BF16)

    slabs = [rows(c * LANES, LANES) for c in range(o_lr // LANES)]
    slabs += [jnp.concatenate([rows(o_sq + t * SWA_HD, SWA_HD), rows(o_sq + (t + half) * SWA_HD, SWA_HD)], axis=0)
              for t in range(half)]
    slabs += [rows(o_sk + c * LANES, LANES) for c in range(2 * SKV // LANES)]
    slabs += [jnp.concatenate([rows(o_lr, GLA_RANK), jnp.zeros((LANES - GLA_RANK, D_MODEL), BF16)], axis=0)]
    for c, slab in enumerate(slabs):
        w_ref[:, c * LANES:(c + 1) * LANES] = slab.T
    _project(lambda rows: _rms(x_ref[rows, :], nw_ref[...]).astype(BF16), x_ref.shape[0],
             w_ref, wg_ref, bg_ref, cq_ref, cs_ref, spread_ref, unrot_ref, gla_s, g_s, swa_s)
    pad = GLA_CHUNK - N_META
    for ref, val in ((km_ref, gla_s[:, GQ:2 * GQ]), (vm_ref, gla_s[:, 2 * GQ:2 * GQ + GV]), (gm_ref, g_s[...])):
        ref[0:pad, :] = jnp.zeros((pad, ref.shape[1]), ref.dtype)
        ref[pad:, :] = val
    sk_ref[...] = swa_s[:, SQ:SQ + SKV]
    sv_ref[...] = swa_s[:, SQ + SKV:]


def _meta_prep(meta_tokens, w_in_t, w_gate_up, small, rope):
    n = meta_tokens.shape[0]
    nw, bg, cq = small
    out_shape = [jax.ShapeDtypeStruct((D_MODEL, W_LR + LANES), BF16), jax.ShapeDtypeStruct((LANES, GQ), BF16),
                 jax.ShapeDtypeStruct((GLA_CHUNK, GQ), BF16), jax.ShapeDtypeStruct((GLA_CHUNK, GV), BF16),
                 jax.ShapeDtypeStruct((GLA_CHUNK, GQ), F32),
                 jax.ShapeDtypeStruct((n, SKV), BF16), jax.ShapeDtypeStruct((n, SKV), BF16)]
    return pl.pallas_call(
        _meta_prep_kernel,
        out_shape=out_shape,
        scratch_shapes=[pltpu.VMEM((n, w), d) for w, d in zip(PROJ_WIDTHS, PROJ_DTYPES)],
        compiler_params=pltpu.CompilerParams(vmem_limit_bytes=VMEM_LIMIT),
        name="meta_prep",
    )(meta_tokens, w_in_t, nw, w_gate_up, bg, cq, *rope)


def _inproj_kernel(x0_ref, xa_ref, nw_ref, w_ref, wg_ref, bg_ref, cq_ref, cs_ref, spread_ref, unrot_ref,
                   w1_ref, w2_ref, wo_ref, *refs):
    *proj_refs, w1b_ref, w2b_ref, wob_ref, u_ref = refs

    @pl.when(pl.program_id(0) == 0)
    def _():
        u_ref[...] = _rms(x0_ref[...], nw_ref[...]).astype(BF16)

    def norm_ahead(rows):
        u_ref[rows, :] = _rms(xa_ref[rows, :], nw_ref[...]).astype(BF16)

    _project(lambda rows: u_ref[rows, :], xa_ref.shape[0], w_ref, wg_ref, bg_ref, cq_ref,
             cs_ref, spread_ref, unrot_ref, *proj_refs, after=norm_ahead)
    w1b_ref[...] = w1_ref[...].astype(BF16)
    w2b_ref[...] = w2_ref[...].astype(BF16)
    wob_ref[...] = wo_ref[...].astype(BF16)


def _in_projection(x2, rope, w_all, small, w1, w2, wo, rows):
    n = x2.shape[0]
    steps = n // rows
    cs, spread, unrot = rope
    nblk_pos = cs.shape[0] // rows
    assert w1.shape[0] % steps == 0 and w2.shape[0] % steps == 0 and wo.shape[0] == steps * SWA_HD

    def row_spec(width):
        return pl.BlockSpec((rows, width), lambda i: (i, 0))

    def full(a):
        return pl.BlockSpec(a.shape, lambda i: (0,) * a.ndim)

    def slice_spec(a):
        return pl.BlockSpec((a.shape[0] // steps, a.shape[1]), lambda i: (i, 0))

    n_gla = GV // SWA_HD
    half = SWA_HEADS // 2

    def wo_src(i):
        j = i - n_gla
        return (jnp.where(j < 0, i, n_gla + j // 2 + half * (j % 2)), 0)

    x_first = pl.BlockSpec((rows, D_MODEL), lambda i: (0, 0), pipeline_mode=pl.Buffered(1))
    x_ahead = pl.BlockSpec((rows, D_MODEL), lambda i: (jnp.minimum(i + 1, steps - 1), 0))
    in_specs = ([x_first, x_ahead, full(small[0]), full(w_all)] + [full(a) for a in small[1:]]
                + [pl.BlockSpec((rows, cs.shape[1]), lambda i: (i % nblk_pos, 0)), full(spread), full(unrot),
                   slice_spec(w1), slice_spec(w2), pl.BlockSpec((SWA_HD, wo.shape[1]), wo_src)])
    weights_out = [jax.ShapeDtypeStruct(a.shape, BF16) for a in (w1, w2, wo)]
    return pl.pallas_call(
        _inproj_kernel,
        grid=(steps,),
        in_specs=in_specs,
        out_specs=[row_spec(w) for w in PROJ_WIDTHS] + [slice_spec(w1), slice_spec(w2), slice_spec(wo)],
        out_shape=[jax.ShapeDtypeStruct((n, w), d) for w, d in zip(PROJ_WIDTHS, PROJ_DTYPES)] + weights_out,
        scratch_shapes=[pltpu.VMEM((rows, D_MODEL), BF16)],
        compiler_params=pltpu.CompilerParams(dimension_semantics=("arbitrary",),
                                             vmem_limit_bytes=VMEM_LIMIT),
        name="in_projection",
    )(x2, x2, small[0], w_all, *small[1:], cs, spread, unrot, w1, w2, wo)


def _block_cumsums(g):
    rows = g.shape[0]
    row = lax.broadcasted_iota(jnp.int32, g.shape, 0)
    w, t = [g], [g]
    for k in range(GLA_LEVELS):
        s = 1 << k
        upper = (row & s) != 0
        below = pltpu.roll(t[k], s, axis=0)
        above = pltpu.roll(t[k], rows - s, axis=0)
        w.append(w[k] + jnp.where(upper, below, 0.0))
        t.append(t[k] + jnp.where(upper, below, above))
    return w, t


def _gla_levels():
    ri = lax.broadcasted_iota(jnp.int32, (GLA_CHUNK, LANES), 0)
    ci = lax.broadcasted_iota(jnp.int32, (GLA_CHUNK, LANES), 1) & (GLA_CHUNK - 1)
    x = ri ^ ci
    level = jnp.zeros_like(x)
    for b in range(GLA_LEVELS):
        level = level + (x >= (1 << b)).astype(jnp.int32)
    return jnp.where(ci > ri, -1, level)


def _gla_chunk(q, k, v, g, st, level):
    w, t = _block_cumsums(g)
    n_slab = GQ // LANES

    def block_diag_t(m):
        r = m.shape[0]
        m2 = jnp.concatenate([m, m], axis=0).T
        same_head = ((lax.broadcasted_iota(jnp.int32, m2.shape, 0) < GLA_DK)
                     == (lax.broadcasted_iota(jnp.int32, m2.shape, 1) < r))
        return jnp.where(same_head, m2, jnp.zeros_like(m2))

    o = None
    if level is not None:
        a = [jnp.zeros((GLA_CHUNK, LANES), F32) for _ in range(n_slab)]
        for lv in range(GLA_LEVELS + 1):
            if lv == 0:
                qt, kt = q.astype(BF16), k.astype(BF16)
            else:
                qt = (q * jnp.exp2(w[lv - 1])).astype(BF16)
                kt = (k * jnp.exp2(t[lv - 1] - w[lv - 1])).astype(BF16)
            for s in range(n_slab):
                sl = slice(s * LANES, (s + 1) * LANES)
                scores = jnp.dot(qt[:, sl], block_diag_t(kt[:, sl]), preferred_element_type=F32)
                a[s] = jnp.where(level == lv, scores, a[s])

        qb = (q * jnp.exp2(w[GLA_LEVELS])).astype(BF16)
        st_b = st.astype(BF16)
        outs = []
        for s in range(n_slab):
            sl = slice(s * LANES, (s + 1) * LANES)
            v_ab = v[:, (2 * s) * GLA_DV:(2 * s + 2) * GLA_DV]
            first = lax.broadcasted_iota(jnp.int32, v_ab.shape, 1) < GLA_DV
            zero = jnp.zeros_like(v_ab)
            v_bd = jnp.concatenate([jnp.where(first, v_ab, zero), jnp.where(first, zero, v_ab)], axis=0)
            outs.append(jnp.dot(jnp.concatenate([a[s].astype(BF16), qb[:, sl]], axis=1),
                                jnp.concatenate([v_bd, block_diag_t(st_b[:, sl])], axis=0),
                                preferred_element_type=F32))
        o = jnp.concatenate(outs, axis=1)

    tot = t[GLA_LEVELS]
    ku = (k * jnp.exp2(tot - w[GLA_LEVELS])).astype(BF16)
    decayed = st * jnp.exp2(tot[0:1, :])
    lo = lax.broadcasted_iota(jnp.int32, (GLA_DV, LANES), 1) < GLA_DK
    new_st = []
    for s in range(n_slab):
        sl = slice(s * LANES, (s + 1) * LANES)
        upd = _tn_dot(v[:, (2 * s) * GLA_DV:(2 * s + 2) * GLA_DV], ku[:, sl])
        new_st.append(decayed[:, sl] + jnp.where(lo, upd[:GLA_DV], upd[GLA_DV:]))
    return o, jnp.concatenate(new_st, axis=1)


def _gla_gate(o, r, nw):
    out = []
    for h in range(GLA_HEADS):
        sl = slice(h * GLA_DV, (h + 1) * GLA_DV)
        rh = r[:, sl].astype(F32)
        out.append((_rms(o[:, sl], nw) * (rh * jax.nn.sigmoid(rh))).astype(BF16))
    return jnp.concatenate(out, axis=1)


def _gla_kernel(in_ref, g_ref, km_ref, vm_ref, gm_ref, nw_ref, o_ref, st_ref):
    @pl.when(pl.program_id(1) == 0)
    def _():
        st_ref[...] = jnp.zeros(st_ref.shape, F32)
        _, st0 = _gla_chunk(None, km_ref[...].astype(F32), vm_ref[...], gm_ref[...], st_ref[...], None)
        st_ref[...] = st0

    nw = nw_ref[...]
    level = _gla_levels()
    st = st_ref[...]
    for c in range(in_ref.shape[1] // GLA_CHUNK):
        rows = slice(c * GLA_CHUNK, (c + 1) * GLA_CHUNK)
        o, st = _gla_chunk(in_ref[0, rows, 0:GQ].astype(F32), in_ref[0, rows, GQ:2 * GQ].astype(F32),
                           in_ref[0, rows, 2 * GQ:2 * GQ + GV], g_ref[0, rows, :], st, level)
        o_ref[0, rows, :] = _gla_gate(o, in_ref[0, rows, 2 * GQ + GV:2 * GQ + 2 * GV], nw)
    st_ref[...] = st


def _gla_mixer(gla_in, g, km, vm, gm, gla_norm_w, rows):
    b, s, _ = gla_in.shape

    def seq_spec(width):
        return pl.BlockSpec((1, rows, width), lambda i, j: (i, j, 0))

    def full(a):
        return pl.BlockSpec(a.shape, lambda i, j: (0,) * a.ndim)

    return pl.pallas_call(
        _gla_kernel,
        grid=(b, s // rows),
        in_specs=[seq_spec(gla_in.shape[-1]), seq_spec(GQ), full(km), full(vm), full(gm), full(gla_norm_w)],
        out_specs=seq_spec(GV),
        out_shape=jax.ShapeDtypeStruct((b, s, GV), BF16),
        scratch_shapes=[pltpu.VMEM((GLA_DV, GQ), F32)],
        compiler_params=pltpu.CompilerParams(dimension_semantics=("arbitrary", "arbitrary"),
                                             vmem_limit_bytes=VMEM_LIMIT),
        name="gla_mixer",
    )(gla_in, g, km, vm, gm, gla_norm_w)


def _swa_block(sink_ref, q, kc, kp, vc, vp, km, vm, has_prev):
    n_win = SWA_BLOCK + SWA_HALF
    n_keys = n_win + N_META
    r = lax.broadcasted_iota(jnp.int32, (SWA_HALF, n_keys), 0)
    c = lax.broadcasted_iota(jnp.int32, (SWA_HALF, n_keys), 1)
    in_band = (c > r) & (c <= r + SWA_BLOCK)
    lane = lax.broadcasted_iota(jnp.int32, (SWA_HALF, LANES), 1)
    low = lane < SWA_HD

    halves = []
    for half in range(2):
        rows = slice(half * SWA_HALF, (half + 1) * SWA_HALF)
        if half == 0:
            k_all = jnp.concatenate([kp, kc[:SWA_HALF], km], axis=0)
            v_all = jnp.concatenate([vp, vc[:SWA_HALF], vm], axis=0)
            n_prev = SWA_BLOCK
        else:
            k_all = jnp.concatenate([kp[SWA_HALF:], kc, km], axis=0)
            v_all = jnp.concatenate([vp[SWA_HALF:], vc, vm], axis=0)
            n_prev = SWA_HALF
        valid = (c >= n_win) | in_band
        if has_prev is not None:
            valid = (c >= n_win) | (in_band & (has_prev | (c >= n_prev)))
        bias = jnp.where(valid, 0.0, -jnp.inf)
        pieces = []
        for t in range(SWA_HEADS // 2):
            qt = q[rows, t * LANES:(t + 1) * LANES]
            pieces += [jnp.where(low, qt, jnp.zeros_like(qt)), jnp.where(low, jnp.zeros_like(qt), qt)]
        q_all = jnp.concatenate(pieces, axis=0)
        s_all = _nt_dot(q_all, k_all)
        p, denom = [], []
        for i in range(SWA_HEADS):
            s = s_all[i * SWA_HALF:(i + 1) * SWA_HALF] + bias
            sink = sink_ref[(i // 2) + (i % 2) * (SWA_HEADS // 2)] * LOG2_E
            m = jnp.maximum(jnp.max(s, axis=-1, keepdims=True), sink)
            p.append(jnp.exp2(s - m))
            denom.append(jnp.sum(p[i], axis=-1, keepdims=True) + jnp.exp2(sink - m))
        o = jnp.dot(jnp.concatenate(p, axis=0).astype(BF16), v_all, preferred_element_type=F32)
        slabs = []
        for t in range(SWA_HEADS // 2):
            lo = o[(2 * t) * SWA_HALF:(2 * t + 1) * SWA_HALF] / denom[2 * t]
            hi = o[(2 * t + 1) * SWA_HALF:(2 * t + 2) * SWA_HALF] / denom[2 * t + 1]
            slabs.append(jnp.where(low, lo, hi).astype(BF16))
        halves.append(jnp.concatenate(slabs, axis=1))
    return jnp.concatenate(halves, axis=0)


def _swa_ffn_kernel(tiles_per_seq, sink_ref, swa0_ref, swa_ref, prev_ref, km_ref, vm_ref,
                    x_ref, og_ref, wo_ref, nf_ref, w1_ref, w2_ref, nl_ref, y_ref, os_ref):
    s = pl.program_id(0)
    ahead = jnp.minimum(s + 1, pl.num_programs(0) - 1)
    seq_start = lax.rem(ahead, tiles_per_seq) == 0
    cur = lax.rem(s, 2)
    k_lanes, v_lanes = slice(SQ, SQ + SKV), slice(SQ + SKV, SQ + 2 * SKV)

    def swa_tile_block(j, slot, tile_ref, first_prev):
        rows = slice(j * SWA_BLOCK, (j + 1) * SWA_BLOCK)
        if j == 0:
            kp, vp, has_prev = first_prev
        else:
            before = slice((j - 1) * SWA_BLOCK, j * SWA_BLOCK)
            kp, vp, has_prev = tile_ref[before, k_lanes], tile_ref[before, v_lanes], None
        os_ref[slot, rows, :] = _swa_block(sink_ref, tile_ref[rows, 0:SQ], tile_ref[rows, k_lanes], kp,
                                           tile_ref[rows, v_lanes], vp, km_ref[...], vm_ref[...], has_prev)

    n_blocks = x_ref.shape[0] // SWA_BLOCK

    @pl.when(s == 0)
    def _():
        stand_in = (swa0_ref[0:SWA_BLOCK, k_lanes], swa0_ref[0:SWA_BLOCK, v_lanes], s > 0)
        for j in range(n_blocks):
            swa_tile_block(j, 0, swa0_ref, stand_in)

    n_ff = D_FF // FF_CHUNK
    swa_per_ff = n_blocks // n_ff

    h = (x_ref[...]
         + jnp.dot(og_ref[...], wo_ref[0:GV, :], preferred_element_type=F32)
         + jnp.dot(os_ref[cur], wo_ref[GV:GV + SQ, :], preferred_element_type=F32))
    f = _rms(h, nf_ref[...]).astype(BF16)
    ff = None
    for c in range(n_ff):
        sl = slice(c * FF_CHUNK, (c + 1) * FF_CHUNK)
        a = jnp.maximum(jnp.dot(f, w1_ref[:, sl], preferred_element_type=F32), 0.0)
        for j in range(c * swa_per_ff, (c + 1) * swa_per_ff):
            swa_tile_block(j, 1 - cur, swa_ref,
                           (prev_ref[:, 0:SKV], prev_ref[:, SKV:], jnp.logical_not(seq_start)))
        d = jnp.dot((a * a).astype(BF16), w2_ref[sl, :], preferred_element_type=F32)
        ff = d if ff is None else ff + d
    y_ref[...] = _rms(h + ff, nl_ref[...])


def _swa_ffn(sinks, swa_in, km, vm, x2, og, wo, nf, w1, w2, nl, rows, tiles_per_seq):
    n = x2.shape[0]
    n_tiles = n // rows
    prev_per_tile = rows // SWA_BLOCK
    width = swa_in.shape[1]

    first_spec = pl.BlockSpec((rows, width), lambda i: (0, 0), pipeline_mode=pl.Buffered(1))
    ahead_spec = pl.BlockSpec((rows, width), lambda i: (jnp.minimum(i + 1, n_tiles - 1), 0))
    prev_spec = pl.BlockSpec((SWA_BLOCK, 2 * SKV),
                             lambda i: (jnp.minimum(i + 1, n_tiles - 1) * prev_per_tile - 1, SQ // (2 * SKV)))

    def ffn_spec(width):
        return pl.BlockSpec((rows, width), lambda i: (i, 0))

    def full(a):
        return pl.BlockSpec(a.shape, lambda i: (0,) * a.ndim, pipeline_mode=pl.Buffered(1))

    weights = (wo, nf, w1, w2, nl)
    return pl.pallas_call(
        functools.partial(_swa_ffn_kernel, tiles_per_seq),
        grid=(n_tiles,),
        in_specs=([pl.BlockSpec(memory_space=pltpu.SMEM), first_spec, ahead_spec, prev_spec,
                   full(km), full(vm), ffn_spec(D_MODEL), ffn_spec(GV)] + [full(a) for a in weights]),
        out_specs=ffn_spec(D_MODEL),
        out_shape=jax.ShapeDtypeStruct((n, D_MODEL), F32),
        scratch_shapes=[pltpu.VMEM((2, rows, SQ), BF16)],
        compiler_params=pltpu.CompilerParams(dimension_semantics=("arbitrary",),
                                             vmem_limit_bytes=VMEM_LIMIT),
        name="swa_ffn",
    )(sinks, swa_in, swa_in, swa_in, km, vm, x2, og, *weights)


def _rope_cos_sin(first_pos, n_pos):
    inv_freq = 1.0 / (ROPE_THETA ** (jnp.arange(0, ROPE_DIM, 2, dtype=F32) / ROPE_DIM))
    ang = jnp.arange(first_pos, first_pos + n_pos, dtype=jnp.int32).astype(F32)[:, None] * inv_freq[None, :]
    return jnp.concatenate([jnp.cos(ang), jnp.sin(ang)], axis=1)


def _rope_spread():
    half = ROPE_DIM // 2
    lane = jnp.arange(2 * LANES)
    d = lane % SWA_HD
    which = lane // LANES
    src = which * half + d % half
    spread = jnp.where((d < ROPE_DIM)[None, :] & (jnp.arange(2 * half)[:, None] == src[None, :]), 1.0, 0.0)
    unrotated = jnp.where((which == 0) & (d >= ROPE_DIM), 1.0, 0.0)
    return jnp.concatenate([spread, spread], axis=0).astype(BF16), unrotated[None, :].astype(F32)


def _swa_head_order():
    order = []
    for t in range(SWA_HEADS // 2):
        order += [t, t + SWA_HEADS // 2]
    return order


def kernel(x, meta_tokens, norm_mix_w, w_in, w_gate_up, b_gate, gla_norm_w, sinks, w_out, norm_ff_w,
           w_ff1, w_ff2, final_norm_w):
    b, s, d = x.shape
    assert d == D_MODEL and s % ROWS_IN == 0 and s % ROWS_OUT == 0 and w_in.shape[0] == 1
    assert s % ROWS_GLA == 0 and ROWS_GLA % GLA_CHUNK == 0
    assert ROWS_OUT % (SWA_BLOCK * (D_FF // FF_CHUNK)) == 0

    bg = b_gate[0].reshape(1, GQ)
    nw = norm_mix_w[0].reshape(1, D_MODEL)
    cq = jnp.full((1, LANES), SWA_HD ** -0.5 * LOG2_E, F32)

    spread, unrot = _rope_spread()
    meta_tables = (_rope_cos_sin(0, N_META), spread, unrot)
    tok_tables = (_rope_cos_sin(N_META, s), spread, unrot)

    w_all, wg, km, vm, gm, m_sk, m_sv = _meta_prep(meta_tokens.astype(F32), w_in[0].T, w_gate_up[0],
                                                   (nw, bg, cq), meta_tables)
    small = (nw, wg, bg, cq)

    x2 = x.reshape(b * s, d)
    gla_in, g, swa_in, w1, w2, wo = _in_projection(x2, tok_tables, w_all, small,
                                                   w_ff1[0], w_ff2[0], w_out[0], ROWS_IN)

    def seq(a):
        return a.reshape(b, s, a.shape[-1])

    o_gla = _gla_mixer(seq(gla_in), seq(g), km, vm, gm, gla_norm_w[0].reshape(1, GLA_DV), ROWS_GLA)
    y = _swa_ffn(sinks[0].astype(F32), swa_in, m_sk, m_sv, x2, o_gla.reshape(b * s, GV), wo,
                 norm_ff_w[0].reshape(1, D_MODEL), w1, w2, final_norm_w.reshape(1, D_MODEL),
                 ROWS_OUT, s // ROWS_OUT)
    return y.reshape(b, s, d)
```

```python
import functools

import jax
import jax.numpy as jnp
from jax import lax
from jax.experimental import pallas as pl
from jax.experimental.pallas import tpu as pltpu

F32 = jnp.float32
BF16 = jnp.bfloat16

D_MODEL = 1024
N_META = 16
GLA_HEADS = 4
GLA_DK = 64
GLA_DV = 128
GLA_RANK = 16
GLA_TAU = 16.0
GLA_CHUNK = 64
GLA_LEVELS = 6
SWA_HEADS = 8
SWA_KV_HEADS = 2
SWA_HD = 64
SWA_BLOCK = 128
SWA_HALF = SWA_BLOCK // 2
ROPE_DIM = 16
ROPE_THETA = 500000.0
D_FF = 4096
EPS = 1e-5
LOG2_E = 1.4426950408889634

LANES = 128
GQ = GLA_HEADS * GLA_DK
GV = GLA_HEADS * GLA_DV
SQ = SWA_HEADS * SWA_HD
SKV = SWA_KV_HEADS * SWA_HD
FF_CHUNK = 1024

ROWS_IN = 1024
ROWS_IN_SUB = 256
W_SQ = 2 * GQ + 2 * GV
W_SK = W_SQ + SQ
W_LR = W_SK + 2 * SKV
ROWS_GLA = 1024
ROWS_OUT = 512
VMEM_LIMIT = 56 * 1024 * 1024


def _nt_dot(a, b):
    return lax.dot_general(a, b, (((1,), (1,)), ((), ())), preferred_element_type=F32)


def _tn_dot(a, b):
    return lax.dot_general(a, b, (((0,), (0,)), ((), ())), preferred_element_type=F32)


def _rms(x, w):
    return x * lax.rsqrt(jnp.mean(x * x, axis=-1, keepdims=True) + EPS) * w


def _project(normed, n_rows, w_ref, bg_ref, cq_ref, cs_ref, spread_ref, unrot_ref, gla_ref, g_ref, swa_ref,
             after=None):
    half = ROPE_DIM // 2
    n_sub = n_rows // ROWS_IN_SUB if n_rows > ROWS_IN_SUB else 1
    sub = n_rows // n_sub
    for i in range(n_sub):
        rows = slice(i * sub, (i + 1) * sub)
        u = normed(rows)

        def proj(lo, width):
            return jnp.dot(u, w_ref[:, lo:lo + width], preferred_element_type=F32)


        cs = cs_ref[rows, :]
        hi = cs.astype(BF16)
        lo = (cs - hi.astype(F32)).astype(BF16)
        pat = jnp.dot(jnp.concatenate([hi, lo], axis=1), spread_ref[...], preferred_element_type=F32) + unrot_ref[...]
        cos, sin = pat[:, 0:LANES], pat[:, LANES:]
        d = lax.broadcasted_iota(jnp.int32, cos.shape, 1) & (SWA_HD - 1)

        def rope(t):
            ts = t * sin
            turned = jnp.where(d < half, -pltpu.roll(ts, LANES - half, axis=1),
                               jnp.where(d < ROPE_DIM, pltpu.roll(ts, half, axis=1), 0.0))
            return t * cos + turned

        sq = proj(W_SQ, SQ)
        for s in range(SQ // LANES):
            sl = slice(s * LANES, (s + 1) * LANES)
            swa_ref[rows, sl] = (rope(sq[:, sl]) * cq_ref[...]).astype(BF16)
        kv = proj(W_SK, 2 * SKV)
        swa_ref[rows, SQ:SQ + SKV] = rope(kv[:, :SKV]).astype(BF16)
        swa_ref[rows, SQ + SKV:SQ + 2 * SKV] = kv[:, SKV:].astype(BF16)

        z = proj(W_LR, GQ) + bg_ref[...]
        g_ref[rows, :] = (jnp.minimum(z, 0.0) - jnp.log1p(jnp.exp(-jnp.abs(z)))) * (LOG2_E / GLA_TAU)

        qk = proj(0, 2 * GQ)
        gla_ref[rows, 0:GQ] = (qk[:, :GQ] * (GLA_DK ** -0.5)).astype(BF16)
        gla_ref[rows, GQ:2 * GQ] = qk[:, GQ:].astype(BF16)
        gla_ref[rows, 2 * GQ:2 * GQ + GV] = proj(2 * GQ, GV).astype(BF16)
        gla_ref[rows, 2 * GQ + GV:2 * GQ + 2 * GV] = proj(2 * GQ + GV, GV).astype(BF16)
        if after is not None:
            after(rows)


PROJ_WIDTHS = (2 * GQ + 2 * GV, GQ, SQ + 2 * SKV)
PROJ_DTYPES = (BF16, F32, BF16)


def _meta_prep_kernel(x_ref, wt_ref, nw_ref, wgu_ref, bg_ref, cq_ref, cs_ref, spread_ref, unrot_ref,
                      w_ref, km_ref, vm_ref, gm_ref, sk_ref, sv_ref, gla_s, g_s, swa_s):
    o_lr = 2 * GQ + 2 * GV
    o_sq = o_lr + GLA_RANK
    o_sk = o_sq + SQ
    half = SWA_HEADS // 2

    def rows(lo, n):
        return wt_ref[lo:lo + n, :].astype(BF16)

    slabs = [rows(c * LANES, LANES) for c in range(o_lr // LANES)]
    slabs += [jnp.concatenate([rows(o_sq + t * SWA_HD, SWA_HD), rows(o_sq + (t + half) * SWA_HD, SWA_HD)], axis=0)
              for t in range(half)]
    slabs += [rows(o_sk + c * LANES, LANES) for c in range(2 * SKV // LANES)]
    for c, slab in enumerate(slabs):
        w_ref[:, c * LANES:(c + 1) * LANES] = slab.T
    w_gate = lax.dot_general(wt_ref[o_lr:o_sq, :], wgu_ref[...], (((0,), (0,)), ((), ())),
                             precision=lax.Precision.HIGHEST, preferred_element_type=F32)
    w_ref[:, W_LR:W_LR + GQ] = w_gate.astype(BF16)
    _project(lambda rows: _rms(x_ref[rows, :], nw_ref[...]).astype(BF16), x_ref.shape[0],
             w_ref, bg_ref, cq_ref, cs_ref, spread_ref, unrot_ref, gla_s, g_s, swa_s)
    pad = GLA_CHUNK - N_META
    for ref, val in ((km_ref, gla_s[:, GQ:2 * GQ]), (vm_ref, gla_s[:, 2 * GQ:2 * GQ + GV]), (gm_ref, g_s[...])):
        ref[0:pad, :] = jnp.zeros((pad, ref.shape[1]), ref.dtype)
        ref[pad:, :] = val
    sk_ref[...] = swa_s[:, SQ:SQ + SKV]
    sv_ref[...] = swa_s[:, SQ + SKV:]


def _inproj_kernel(x0_ref, xa_ref, xm_ref, wt_ref, wgu_ref, nw_ref, bg_ref, cq_ref, cs_ref, csm_ref, spread_ref, unrot_ref,
                   w1_ref, w2_ref, wo_ref, *refs):
    *proj_refs, w1b_ref, w2b_ref, wob_ref, km_ref, vm_ref, gm_ref, sk_ref, sv_ref, u_ref, w_ref, gla_s, g_s, swa_s = refs

    @pl.when(pl.program_id(0) == 0)
    def _():
        _meta_prep_kernel(xm_ref, wt_ref, nw_ref, wgu_ref, bg_ref, cq_ref, csm_ref, spread_ref, unrot_ref,
                          w_ref, km_ref, vm_ref, gm_ref, sk_ref, sv_ref, gla_s, g_s, swa_s)
        u_ref[...] = _rms(x0_ref[...], nw_ref[...]).astype(BF16)

    def norm_ahead(rows):
        u_ref[rows, :] = _rms(xa_ref[rows, :], nw_ref[...]).astype(BF16)

    _project(lambda rows: u_ref[rows, :], xa_ref.shape[0], w_ref, bg_ref, cq_ref,
             cs_ref, spread_ref, unrot_ref, *proj_refs, after=norm_ahead)
    w1b_ref[...] = w1_ref[...].astype(BF16)
    w2b_ref[...] = w2_ref[...].astype(BF16)
    wob_ref[...] = wo_ref[...].astype(BF16)


def _in_projection(x2, meta_tokens, w_in_t, w_gate_up, small, cs, cs_meta, spread, unrot, w1, w2, wo, rows):
    n = x2.shape[0]
    n_meta = meta_tokens.shape[0]
    steps = n // rows
    nw, bg, cq = small
    nblk_pos = cs.shape[0] // rows
    assert w1.shape[0] % steps == 0 and w2.shape[0] % steps == 0 and wo.shape[0] == steps * SWA_HD

    def row_spec(width):
        return pl.BlockSpec((rows, width), lambda i: (i, 0))

    def full(a):
        return pl.BlockSpec(a.shape, lambda i: (0,) * a.ndim)

    def slice_spec(a):
        return pl.BlockSpec((a.shape[0] // steps, a.shape[1]), lambda i: (i, 0))

    n_gla = GV // SWA_HD
    half = SWA_HEADS // 2

    def wo_src(i):
        j = i - n_gla
        return (jnp.where(j < 0, i, n_gla + j // 2 + half * (j % 2)), 0)

    x_first = pl.BlockSpec((rows, D_MODEL), lambda i: (0, 0), pipeline_mode=pl.Buffered(1))
    x_ahead = pl.BlockSpec((rows, D_MODEL), lambda i: (jnp.minimum(i + 1, steps - 1), 0))
    def once(a):
        return pl.BlockSpec(a.shape, lambda i: (0,) * a.ndim, pipeline_mode=pl.Buffered(1))

    in_specs = ([x_first, x_ahead, once(meta_tokens), once(w_in_t), once(w_gate_up), full(nw), full(bg), full(cq),
                 pl.BlockSpec((rows, cs.shape[1]), lambda i: (i % nblk_pos, 0)), once(cs_meta), full(spread), full(unrot),
                 slice_spec(w1), slice_spec(w2), pl.BlockSpec((SWA_HD, wo.shape[1]), wo_src)])
    weights_out = [jax.ShapeDtypeStruct(a.shape, BF16) for a in (w1, w2, wo)]
    meta_out = [jax.ShapeDtypeStruct((GLA_CHUNK, GQ), BF16), jax.ShapeDtypeStruct((GLA_CHUNK, GV), BF16),
                jax.ShapeDtypeStruct((GLA_CHUNK, GQ), F32),
                jax.ShapeDtypeStruct((n_meta, SKV), BF16), jax.ShapeDtypeStruct((n_meta, SKV), BF16)]
    return pl.pallas_call(
        _inproj_kernel,
        grid=(steps,),
        in_specs=in_specs,
        out_specs=([row_spec(w) for w in PROJ_WIDTHS] + [slice_spec(w1), slice_spec(w2), slice_spec(wo)]
                   + [full(o) for o in meta_out]),
        out_shape=[jax.ShapeDtypeStruct((n, w), d) for w, d in zip(PROJ_WIDTHS, PROJ_DTYPES)] + weights_out + meta_out,
        scratch_shapes=([pltpu.VMEM((rows, D_MODEL), BF16),
                         pltpu.VMEM((D_MODEL, W_LR + GQ), BF16)]
                        + [pltpu.VMEM((n_meta, w), d) for w, d in zip(PROJ_WIDTHS, PROJ_DTYPES)]),
        compiler_params=pltpu.CompilerParams(dimension_semantics=("arbitrary",),
                                             vmem_limit_bytes=VMEM_LIMIT),
        name="in_projection",
    )(x2, x2, meta_tokens, w_in_t, w_gate_up, nw, bg, cq, cs, cs_meta, spread, unrot, w1, w2, wo)


def _block_cumsums(g):
    rows = g.shape[0]
    row = lax.broadcasted_iota(jnp.int32, g.shape, 0)
    w, t = [g], [g]
    for k in range(GLA_LEVELS):
        s = 1 << k
        upper = (row & s) != 0
        below = pltpu.roll(t[k], s, axis=0)
        above = pltpu.roll(t[k], rows - s, axis=0)
        w.append(w[k] + jnp.where(upper, below, 0.0))
        t.append(t[k] + jnp.where(upper, below, above))
    return w, t


def _gla_levels():
    ri = lax.broadcasted_iota(jnp.int32, (GLA_CHUNK, LANES), 0)
    ci = lax.broadcasted_iota(jnp.int32, (GLA_CHUNK, LANES), 1) & (GLA_CHUNK - 1)
    x = ri ^ ci
    level = jnp.zeros_like(x)
    for b in range(GLA_LEVELS):
        level = level + (x >= (1 << b)).astype(jnp.int32)
    return jnp.where(ci > ri, -1, level)


def _gla_chunk(q, k, v, g, st, level):
    w, t = _block_cumsums(g)
    n_slab = GQ // LANES

    def block_diag_t(m):
        r = m.shape[0]
        m2 = jnp.concatenate([m, m], axis=0).T
        same_head = ((lax.broadcasted_iota(jnp.int32, m2.shape, 0) < GLA_DK)
                     == (lax.broadcasted_iota(jnp.int32, m2.shape, 1) < r))
        return jnp.where(same_head, m2, jnp.zeros_like(m2))

    o = None
    if level is not None:
        a = [jnp.zeros((GLA_CHUNK, LANES), F32) for _ in range(n_slab)]
        for lv in range(GLA_LEVELS + 1):
            if lv == 0:
                qt, kt = q.astype(BF16), k.astype(BF16)
            else:
                qt = (q * jnp.exp2(w[lv - 1])).astype(BF16)
                kt = (k * jnp.exp2(t[lv - 1] - w[lv - 1])).astype(BF16)
            for s in range(n_slab):
                sl = slice(s * LANES, (s + 1) * LANES)
                scores = jnp.dot(qt[:, sl], block_diag_t(kt[:, sl]), preferred_element_type=F32)
                a[s] = jnp.where(level == lv, scores, a[s])

        qb = (q * jnp.exp2(w[GLA_LEVELS])).astype(BF16)
        st_b = st.astype(BF16)
        outs = []
        for s in range(n_slab):
            sl = slice(s * LANES, (s + 1) * LANES)
            v_ab = v[:, (2 * s) * GLA_DV:(2 * s + 2) * GLA_DV]
            first = lax.broadcasted_iota(jnp.int32, v_ab.shape, 1) < GLA_DV
            zero = jnp.zeros_like(v_ab)
            v_bd = jnp.concatenate([jnp.where(first, v_ab, zero), jnp.where(first, zero, v_ab)], axis=0)
            outs.append(jnp.dot(jnp.concatenate([a[s].astype(BF16), qb[:, sl]], axis=1),
                                jnp.concatenate([v_bd, block_diag_t(st_b[:, sl])], axis=0),
                                preferred_element_type=F32))
        o = jnp.concatenate(outs, axis=1)

    tot = t[GLA_LEVELS]
    ku = (k * jnp.exp2(tot - w[GLA_LEVELS])).astype(BF16)
    decayed = st * jnp.exp2(tot[0:1, :])
    lo = lax.broadcasted_iota(jnp.int32, (GLA_DV, LANES), 1) < GLA_DK
    new_st = []
    for s in range(n_slab):
        sl = slice(s * LANES, (s + 1) * LANES)
        upd = _tn_dot(v[:, (2 * s) * GLA_DV:(2 * s + 2) * GLA_DV], ku[:, sl])
        new_st.append(decayed[:, sl] + jnp.where(lo, upd[:GLA_DV], upd[GLA_DV:]))
    return o, jnp.concatenate(new_st, axis=1)


def _gla_gate(o, r, nw):
    out = []
    for h in range(GLA_HEADS):
        sl = slice(h * GLA_DV, (h + 1) * GLA_DV)
        rh = r[:, sl].astype(F32)
        out.append((_rms(o[:, sl], nw) * (rh * jax.nn.sigmoid(rh))).astype(BF16))
    return jnp.concatenate(out, axis=1)


def _gla_kernel(in_ref, g_ref, km_ref, vm_ref, gm_ref, nw_ref, o_ref, st_ref):
    @pl.when(pl.program_id(1) == 0)
    def _():
        st_ref[...] = jnp.zeros(st_ref.shape, F32)
        _, st0 = _gla_chunk(None, km_ref[...].astype(F32), vm_ref[...], gm_ref[...], st_ref[...], None)
        st_ref[...] = st0

    nw = nw_ref[...]
    level = _gla_levels()
    st = st_ref[...]
    for c in range(in_ref.shape[1] // GLA_CHUNK):
        rows = slice(c * GLA_CHUNK, (c + 1) * GLA_CHUNK)
        o, st = _gla_chunk(in_ref[0, rows, 0:GQ].astype(F32), in_ref[0, rows, GQ:2 * GQ].astype(F32),
                           in_ref[0, rows, 2 * GQ:2 * GQ + GV], g_ref[0, rows, :], st, level)
        o_ref[0, rows, :] = _gla_gate(o, in_ref[0, rows, 2 * GQ + GV:2 * GQ + 2 * GV], nw)
    st_ref[...] = st


def _gla_mixer(gla_in, g, km, vm, gm, gla_norm_w, rows):
    b, s, _ = gla_in.shape

    def seq_spec(width):
        return pl.BlockSpec((1, rows, width), lambda i, j: (i, j, 0))

    def full(a):
        return pl.BlockSpec(a.shape, lambda i, j: (0,) * a.ndim)

    return pl.pallas_call(
        _gla_kernel,
        grid=(b, s // rows),
        in_specs=[seq_spec(gla_in.shape[-1]), seq_spec(GQ), full(km), full(vm), full(gm), full(gla_norm_w)],
        out_specs=seq_spec(GV),
        out_shape=jax.ShapeDtypeStruct((b, s, GV), BF16),
        scratch_shapes=[pltpu.VMEM((GLA_DV, GQ), F32)],
        compiler_params=pltpu.CompilerParams(dimension_semantics=("arbitrary", "arbitrary"),
                                             vmem_limit_bytes=VMEM_LIMIT),
        name="gla_mixer",
    )(gla_in, g, km, vm, gm, gla_norm_w)


def _swa_block(sink_ref, q, kc, kp, vc, vp, km, vm, has_prev):
    n_win = SWA_BLOCK + SWA_HALF
    n_keys = n_win + N_META
    r = lax.broadcasted_iota(jnp.int32, (SWA_HALF, n_keys), 0)
    c = lax.broadcasted_iota(jnp.int32, (SWA_HALF, n_keys), 1)
    in_band = (c > r) & (c <= r + SWA_BLOCK)
    lane = lax.broadcasted_iota(jnp.int32, (SWA_HALF, LANES), 1)
    low = lane < SWA_HD

    halves = []
    for half in range(2):
        rows = slice(half * SWA_HALF, (half + 1) * SWA_HALF)
        if half == 0:
            k_all = jnp.concatenate([kp, kc[:SWA_HALF], km], axis=0)
            v_all = jnp.concatenate([vp, vc[:SWA_HALF], vm], axis=0)
            n_prev = SWA_BLOCK
        else:
            k_all = jnp.concatenate([kp[SWA_HALF:], kc, km], axis=0)
            v_all = jnp.concatenate([vp[SWA_HALF:], vc, vm], axis=0)
            n_prev = SWA_HALF
        valid = (c >= n_win) | in_band
        if has_prev is not None:
            valid = (c >= n_win) | (in_band & (has_prev | (c >= n_prev)))
        bias = jnp.where(valid, 0.0, -jnp.inf)
        pieces = []
        for t in range(SWA_HEADS // 2):
            qt = q[rows, t * LANES:(t + 1) * LANES]
            pieces += [jnp.where(low, qt, jnp.zeros_like(qt)), jnp.where(low, jnp.zeros_like(qt), qt)]
        q_all = jnp.concatenate(pieces, axis=0)
        s_all = _nt_dot(q_all, k_all)
        p, denom = [], []
        for i in range(SWA_HEADS):
            s = s_all[i * SWA_HALF:(i + 1) * SWA_HALF] + bias
            sink = sink_ref[(i // 2) + (i % 2) * (SWA_HEADS // 2)] * LOG2_E
            m = jnp.maximum(jnp.max(s, axis=-1, keepdims=True), sink)
            p.append(jnp.exp2(s - m))
            denom.append(jnp.sum(p[i], axis=-1, keepdims=True) + jnp.exp2(sink - m))
        o = jnp.dot(jnp.concatenate(p, axis=0).astype(BF16), v_all, preferred_element_type=F32)
        slabs = []
        for t in range(SWA_HEADS // 2):
            lo = o[(2 * t) * SWA_HALF:(2 * t + 1) * SWA_HALF] / denom[2 * t]
            hi = o[(2 * t + 1) * SWA_HALF:(2 * t + 2) * SWA_HALF] / denom[2 * t + 1]
            slabs.append(jnp.where(low, lo, hi).astype(BF16))
        halves.append(jnp.concatenate(slabs, axis=1))
    return jnp.concatenate(halves, axis=0)


def _swa_ffn_kernel(tiles_per_seq, sink_ref, swa0_ref, swa_ref, prev_ref, km_ref, vm_ref,
                    x_ref, og_ref, wo_ref, nf_ref, w1_ref, w2_ref, nl_ref, y_ref, os_ref):
    s = pl.program_id(0)
    ahead = jnp.minimum(s + 1, pl.num_programs(0) - 1)
    seq_start = lax.rem(ahead, tiles_per_seq) == 0
    cur = lax.rem(s, 2)
    k_lanes, v_lanes = slice(SQ, SQ + SKV), slice(SQ + SKV, SQ + 2 * SKV)

    def swa_tile_block(j, slot, tile_ref, first_prev):
        rows = slice(j * SWA_BLOCK, (j + 1) * SWA_BLOCK)
        if j == 0:
            kp, vp, has_prev = first_prev
        else:
            before = slice((j - 1) * SWA_BLOCK, j * SWA_BLOCK)
            kp, vp, has_prev = tile_ref[before, k_lanes], tile_ref[before, v_lanes], None
        os_ref[slot, rows, :] = _swa_block(sink_ref, tile_ref[rows, 0:SQ], tile_ref[rows, k_lanes], kp,
                                           tile_ref[rows, v_lanes], vp, km_ref[...], vm_ref[...], has_prev)

    n_blocks = x_ref.shape[0] // SWA_BLOCK

    @pl.when(s == 0)
    def _():
        stand_in = (swa0_ref[0:SWA_BLOCK, k_lanes], swa0_ref[0:SWA_BLOCK, v_lanes], s > 0)
        for j in range(n_blocks):
            swa_tile_block(j, 0, swa0_ref, stand_in)

    n_ff = D_FF // FF_CHUNK
    swa_per_ff = n_blocks // n_ff

    h = (x_ref[...]
         + jnp.dot(og_ref[...], wo_ref[0:GV, :], preferred_element_type=F32)
         + jnp.dot(os_ref[cur], wo_ref[GV:GV + SQ, :], preferred_element_type=F32))
    f = _rms(h, nf_ref[...]).astype(BF16)
    ff = None
    for c in range(n_ff):
        sl = slice(c * FF_CHUNK, (c + 1) * FF_CHUNK)
        a = jnp.maximum(jnp.dot(f, w1_ref[:, sl], preferred_element_type=F32), 0.0)
        for j in range(c * swa_per_ff, (c + 1) * swa_per_ff):
            swa_tile_block(j, 1 - cur, swa_ref,
                           (prev_ref[:, 0:SKV], prev_ref[:, SKV:], jnp.logical_not(seq_start)))
        d = jnp.dot((a * a).astype(BF16), w2_ref[sl, :], preferred_element_type=F32)
        ff = d if ff is None else ff + d
    y_ref[...] = _rms(h + ff, nl_ref[...])


def _swa_ffn(sinks, swa_in, km, vm, x2, og, wo, nf, w1, w2, nl, rows, tiles_per_seq):
    n = x2.shape[0]
    n_tiles = n // rows
    prev_per_tile = rows // SWA_BLOCK
    width = swa_in.shape[1]

    first_spec = pl.BlockSpec((rows, width), lambda i: (0, 0), pipeline_mode=pl.Buffered(1))
    ahead_spec = pl.BlockSpec((rows, width), lambda i: (jnp.minimum(i + 1, n_tiles - 1), 0))
    prev_spec = pl.BlockSpec((SWA_BLOCK, 2 * SKV),
                             lambda i: (jnp.minimum(i + 1, n_tiles - 1) * prev_per_tile - 1, SQ // (2 * SKV)))

    def ffn_spec(width):
        return pl.BlockSpec((rows, width), lambda i: (i, 0))

    def full(a):
        return pl.BlockSpec(a.shape, lambda i: (0,) * a.ndim, pipeline_mode=pl.Buffered(1))

    weights = (wo, nf, w1, w2, nl)
    return pl.pallas_call(
        functools.partial(_swa_ffn_kernel, tiles_per_seq),
        grid=(n_tiles,),
        in_specs=([pl.BlockSpec(memory_space=pltpu.SMEM), first_spec, ahead_spec, prev_spec,
                   full(km), full(vm), ffn_spec(D_MODEL), ffn_spec(GV)] + [full(a) for a in weights]),
        out_specs=ffn_spec(D_MODEL),
        out_shape=jax.ShapeDtypeStruct((n, D_MODEL), F32),
        scratch_shapes=[pltpu.VMEM((2, rows, SQ), BF16)],
        compiler_params=pltpu.CompilerParams(dimension_semantics=("arbitrary",),
                                             vmem_limit_bytes=VMEM_LIMIT),
        name="swa_ffn",
    )(sinks, swa_in, swa_in, swa_in, km, vm, x2, og, *weights)


def _rope_cos_sin(first_pos, n_pos):
    inv_freq = 1.0 / (ROPE_THETA ** (jnp.arange(0, ROPE_DIM, 2, dtype=F32) / ROPE_DIM))
    ang = jnp.arange(first_pos, first_pos + n_pos, dtype=jnp.int32).astype(F32)[:, None] * inv_freq[None, :]
    return jnp.concatenate([jnp.cos(ang), jnp.sin(ang)], axis=1)


def _rope_spread():
    half = ROPE_DIM // 2
    lane = jnp.arange(2 * LANES)
    d = lane % SWA_HD
    which = lane // LANES
    src = which * half + d % half
    spread = jnp.where((d < ROPE_DIM)[None, :] & (jnp.arange(2 * half)[:, None] == src[None, :]), 1.0, 0.0)
    unrotated = jnp.where((which == 0) & (d >= ROPE_DIM), 1.0, 0.0)
    return jnp.concatenate([spread, spread], axis=0).astype(BF16), unrotated[None, :].astype(F32)


def _swa_head_order():
    order = []
    for t in range(SWA_HEADS // 2):
        order += [t, t + SWA_HEADS // 2]
    return order


def kernel(x, meta_tokens, norm_mix_w, w_in, w_gate_up, b_gate, gla_norm_w, sinks, w_out, norm_ff_w,
           w_ff1, w_ff2, final_norm_w):
    b, s, d = x.shape
    assert d == D_MODEL and s % ROWS_IN == 0 and s % ROWS_OUT == 0 and w_in.shape[0] == 1
    assert s % ROWS_GLA == 0 and ROWS_GLA % GLA_CHUNK == 0
    assert ROWS_OUT % (SWA_BLOCK * (D_FF // FF_CHUNK)) == 0

    bg = b_gate[0].reshape(1, GQ)
    nw = norm_mix_w[0].reshape(1, D_MODEL)
    cq = jnp.full((1, LANES), SWA_HD ** -0.5 * LOG2_E, F32)

    spread, unrot = _rope_spread()
    cs_meta = _rope_cos_sin(0, N_META)
    cs_tok = _rope_cos_sin(N_META, s)

    x2 = x.reshape(b * s, d)
    gla_in, g, swa_in, w1, w2, wo, km, vm, gm, m_sk, m_sv = _in_projection(
        x2, meta_tokens.astype(F32), w_in[0].T, w_gate_up[0], (nw, bg, cq), cs_tok, cs_meta, spread, unrot,
        w_ff1[0], w_ff2[0], w_out[0], ROWS_IN)

    def seq(a):
        return a.reshape(b, s, a.shape[-1])

    o_gla = _gla_mixer(seq(gla_in), seq(g), km, vm, gm, gla_norm_w[0].reshape(1, GLA_DV), ROWS_GLA)
    y = _swa_ffn(sinks[0].astype(F32), swa_in, m_sk, m_sv, x2, o_gla.reshape(b * s, GV), wo,
                 norm_ff_w[0].reshape(1, D_MODEL), w1, w2, final_norm_w.reshape(1, D_MODEL),
                 ROWS_OUT, s // ROWS_OUT)
    return y.reshape(b, s, d)
```

```python
import functools

import jax
import jax.numpy as jnp
from jax import lax
from jax.experimental import pallas as pl
from jax.experimental.pallas import tpu as pltpu

F32 = jnp.float32
BF16 = jnp.bfloat16

D_MODEL = 1024
N_META = 16
GLA_HEADS = 4
GLA_DK = 64
GLA_DV = 128
GLA_RANK = 16
GLA_TAU = 16.0
GLA_CHUNK = 64
GLA_LEVELS = 6
SWA_HEADS = 8
SWA_KV_HEADS = 2
SWA_HD = 64
SWA_BLOCK = 128
SWA_HALF = SWA_BLOCK // 2
ROPE_DIM = 16
ROPE_THETA = 500000.0
D_FF = 4096
EPS = 1e-5
LOG2_E = 1.4426950408889634

LANES = 128
GQ = GLA_HEADS * GLA_DK
GV = GLA_HEADS * GLA_DV
SQ = SWA_HEADS * SWA_HD
SKV = SWA_KV_HEADS * SWA_HD
FF_CHUNK = 1024

ROWS_IN = 1024
ROWS_IN_SUB = 512
W_SQ = 2 * GQ + 2 * GV
W_SK = W_SQ + SQ
W_LR = W_SK + 2 * SKV
ROWS_GLA = 1024
ROWS_OUT = 512
VMEM_LIMIT = 56 * 1024 * 1024


def _nt_dot(a, b):
    return lax.dot_general(a, b, (((1,), (1,)), ((), ())), preferred_element_type=F32)


def _tn_dot(a, b):
    return lax.dot_general(a, b, (((0,), (0,)), ((), ())), preferred_element_type=F32)


def _rms(x, w):
    return x * lax.rsqrt(jnp.mean(x * x, axis=-1, keepdims=True) + EPS) * w


def _project(normed, n_rows, w_ref, bg_ref, cq_ref, cs_ref, spread_ref, unrot_ref, gla_ref, g_ref, swa_ref,
             after=None):
    half = ROPE_DIM // 2
    n_sub = n_rows // ROWS_IN_SUB if n_rows > ROWS_IN_SUB else 1
    sub = n_rows // n_sub
    for i in range(n_sub):
        rows = slice(i * sub, (i + 1) * sub)
        u = normed(rows)

        def proj(lo, width):
            return jnp.dot(u, w_ref[:, lo:lo + width], preferred_element_type=F32)


        cs = cs_ref[rows, :]
        hi = cs.astype(BF16)
        lo = (cs - hi.astype(F32)).astype(BF16)
        pat = jnp.dot(jnp.concatenate([hi, lo], axis=1), spread_ref[...], preferred_element_type=F32) + unrot_ref[...]
        cos, sin = pat[:, 0:LANES], pat[:, LANES:]
        d = lax.broadcasted_iota(jnp.int32, cos.shape, 1) & (SWA_HD - 1)

        def rope(t):
            ts = t * sin
            turned = jnp.where(d < half, -pltpu.roll(ts, LANES - half, axis=1),
                               jnp.where(d < ROPE_DIM, pltpu.roll(ts, half, axis=1), 0.0))
            return t * cos + turned

        sq = proj(W_SQ, SQ)
        for s in range(SQ // LANES):
            sl = slice(s * LANES, (s + 1) * LANES)
            swa_ref[rows, sl] = (rope(sq[:, sl]) * cq_ref[...]).astype(BF16)
        kv = proj(W_SK, 2 * SKV)
        swa_ref[rows, SQ:SQ + SKV] = rope(kv[:, :SKV]).astype(BF16)
        swa_ref[rows, SQ + SKV:SQ + 2 * SKV] = kv[:, SKV:].astype(BF16)

        z = proj(W_LR, GQ) + bg_ref[...]
        g_ref[rows, :] = (jnp.minimum(z, 0.0) - jnp.log1p(jnp.exp(-jnp.abs(z)))) * (LOG2_E / GLA_TAU)

        qk = proj(0, 2 * GQ)
        gla_ref[rows, 0:GQ] = (qk[:, :GQ] * (GLA_DK ** -0.5)).astype(BF16)
        gla_ref[rows, GQ:2 * GQ] = qk[:, GQ:].astype(BF16)
        gla_ref[rows, 2 * GQ:2 * GQ + GV] = proj(2 * GQ, GV).astype(BF16)
        gla_ref[rows, 2 * GQ + GV:2 * GQ + 2 * GV] = proj(2 * GQ + GV, GV).astype(BF16)
        if after is not None:
            after(rows)


PROJ_WIDTHS = (2 * GQ + 2 * GV, GQ, SQ + 2 * SKV)
PROJ_DTYPES = (BF16, F32, BF16)


def _meta_prep_kernel(x_ref, wt_ref, nw_ref, wgu_ref, bg_ref, cq_ref, cs_ref, spread_ref, unrot_ref,
                      w_ref, km_ref, vm_ref, gm_ref, sk_ref, sv_ref, gla_s, g_s, swa_s):
    o_lr = 2 * GQ + 2 * GV
    o_sq = o_lr + GLA_RANK
    o_sk = o_sq + SQ
    half = SWA_HEADS // 2

    def rows(lo, n):
        return wt_ref[lo:lo + n, :].astype(BF16)

    slabs = [rows(c * LANES, LANES) for c in range(o_lr // LANES)]
    slabs += [jnp.concatenate([rows(o_sq + t * SWA_HD, SWA_HD), rows(o_sq + (t + half) * SWA_HD, SWA_HD)], axis=0)
              for t in range(half)]
    slabs += [rows(o_sk + c * LANES, LANES) for c in range(2 * SKV // LANES)]
    for c, slab in enumerate(slabs):
        w_ref[:, c * LANES:(c + 1) * LANES] = slab.T
    w_gate = lax.dot_general(wt_ref[o_lr:o_sq, :], wgu_ref[...], (((0,), (0,)), ((), ())),
                             precision=lax.Precision.HIGHEST, preferred_element_type=F32)
    w_ref[:, W_LR:W_LR + GQ] = w_gate.astype(BF16)
    _project(lambda rows: _rms(x_ref[rows, :], nw_ref[...]).astype(BF16), x_ref.shape[0],
             w_ref, bg_ref, cq_ref, cs_ref, spread_ref, unrot_ref, gla_s, g_s, swa_s)
    pad = GLA_CHUNK - N_META
    for ref, val in ((km_ref, gla_s[:, GQ:2 * GQ]), (vm_ref, gla_s[:, 2 * GQ:2 * GQ + GV]), (gm_ref, g_s[...])):
        ref[0:pad, :] = jnp.zeros((pad, ref.shape[1]), ref.dtype)
        ref[pad:, :] = val
    sk_ref[...] = swa_s[:, SQ:SQ + SKV]
    sv_ref[...] = swa_s[:, SQ + SKV:]


def _inproj_kernel(x0_ref, xa_ref, xm_ref, wt_ref, wgu_ref, nw_ref, bg_ref, cq_ref, cs_ref, csm_ref, spread_ref, unrot_ref,
                   w1_ref, w2_ref, wo_ref, *refs):
    *proj_refs, w1b_ref, w2b_ref, wob_ref, km_ref, vm_ref, gm_ref, sk_ref, sv_ref, u_ref, w_ref, gla_s, g_s, swa_s = refs

    @pl.when(pl.program_id(0) == 0)
    def _():
        _meta_prep_kernel(xm_ref, wt_ref, nw_ref, wgu_ref, bg_ref, cq_ref, csm_ref, spread_ref, unrot_ref,
                          w_ref, km_ref, vm_ref, gm_ref, sk_ref, sv_ref, gla_s, g_s, swa_s)
        u_ref[...] = _rms(x0_ref[...], nw_ref[...]).astype(BF16)

    def norm_ahead(rows):
        u_ref[rows, :] = _rms(xa_ref[rows, :], nw_ref[...]).astype(BF16)

    _project(lambda rows: u_ref[rows, :], xa_ref.shape[0], w_ref, bg_ref, cq_ref,
             cs_ref, spread_ref, unrot_ref, *proj_refs, after=norm_ahead)
    w1b_ref[...] = w1_ref[...].astype(BF16)
    w2b_ref[...] = w2_ref[...].astype(BF16)
    wob_ref[...] = wo_ref[...].astype(BF16)


def _in_projection(x2, meta_tokens, w_in_t, w_gate_up, small, cs, cs_meta, spread, unrot, w1, w2, wo, rows):
    n = x2.shape[0]
    n_meta = meta_tokens.shape[0]
    steps = n // rows
    nw, bg, cq = small
    nblk_pos = cs.shape[0] // rows
    assert w1.shape[0] % steps == 0 and w2.shape[0] % steps == 0 and wo.shape[0] == steps * SWA_HD

    def row_spec(width):
        return pl.BlockSpec((rows, width), lambda i: (i, 0))

    def full(a):
        return pl.BlockSpec(a.shape, lambda i: (0,) * a.ndim)

    def slice_spec(a):
        return pl.BlockSpec((a.shape[0] // steps, a.shape[1]), lambda i: (i, 0))

    n_gla = GV // SWA_HD
    half = SWA_HEADS // 2

    def wo_src(i):
        j = i - n_gla
        return (jnp.where(j < 0, i, n_gla + j // 2 + half * (j % 2)), 0)

    x_first = pl.BlockSpec((rows, D_MODEL), lambda i: (0, 0), pipeline_mode=pl.Buffered(1))
    x_ahead = pl.BlockSpec((rows, D_MODEL), lambda i: (jnp.minimum(i + 1, steps - 1), 0))
    def once(a):
        return pl.BlockSpec(a.shape, lambda i: (0,) * a.ndim, pipeline_mode=pl.Buffered(1))

    in_specs = ([x_first, x_ahead, once(meta_tokens), once(w_in_t), once(w_gate_up), full(nw), full(bg), full(cq),
                 pl.BlockSpec((rows, cs.shape[1]), lambda i: (i % nblk_pos, 0)), once(cs_meta), full(spread), full(unrot),
                 slice_spec(w1), slice_spec(w2), pl.BlockSpec((SWA_HD, wo.shape[1]), wo_src)])
    weights_out = [jax.ShapeDtypeStruct(a.shape, BF16) for a in (w1, w2, wo)]
    meta_out = [jax.ShapeDtypeStruct((GLA_CHUNK, GQ), BF16), jax.ShapeDtypeStruct((GLA_CHUNK, GV), BF16),
                jax.ShapeDtypeStruct((GLA_CHUNK, GQ), F32),
                jax.ShapeDtypeStruct((n_meta, SKV), BF16), jax.ShapeDtypeStruct((n_meta, SKV), BF16)]
    return pl.pallas_call(
        _inproj_kernel,
        grid=(steps,),
        in_specs=in_specs,
        out_specs=([row_spec(w) for w in PROJ_WIDTHS] + [slice_spec(w1), slice_spec(w2), slice_spec(wo)]
                   + [full(o) for o in meta_out]),
        out_shape=[jax.ShapeDtypeStruct((n, w), d) for w, d in zip(PROJ_WIDTHS, PROJ_DTYPES)] + weights_out + meta_out,
        scratch_shapes=([pltpu.VMEM((rows, D_MODEL), BF16),
                         pltpu.VMEM((D_MODEL, W_LR + GQ), BF16)]
                        + [pltpu.VMEM((n_meta, w), d) for w, d in zip(PROJ_WIDTHS, PROJ_DTYPES)]),
        compiler_params=pltpu.CompilerParams(dimension_semantics=("arbitrary",),
                                             vmem_limit_bytes=VMEM_LIMIT),
        name="in_projection",
    )(x2, x2, meta_tokens, w_in_t, w_gate_up, nw, bg, cq, cs, cs_meta, spread, unrot, w1, w2, wo)


def _block_cumsums(g):
    rows = g.shape[0]
    row = lax.broadcasted_iota(jnp.int32, g.shape, 0)
    w, t = [g], [g]
    for k in range(GLA_LEVELS):
        s = 1 << k
        upper = (row & s) != 0
        below = pltpu.roll(t[k], s, axis=0)
        above = pltpu.roll(t[k], rows - s, axis=0)
        w.append(w[k] + jnp.where(upper, below, 0.0))
        t.append(t[k] + jnp.where(upper, below, above))
    return w, t


def _gla_levels():
    ri = lax.broadcasted_iota(jnp.int32, (GLA_CHUNK, LANES), 0)
    ci = lax.broadcasted_iota(jnp.int32, (GLA_CHUNK, LANES), 1) & (GLA_CHUNK - 1)
    x = ri ^ ci
    level = jnp.zeros_like(x)
    for b in range(GLA_LEVELS):
        level = level + (x >= (1 << b)).astype(jnp.int32)
    return jnp.where(ci > ri, -1, level)


def _gla_chunk(q, k, v, g, st, level):
    w, t = _block_cumsums(g)
    n_slab = GQ // LANES

    def block_diag_t(m):
        r = m.shape[0]
        m2 = jnp.concatenate([m, m], axis=0).T
        same_head = ((lax.broadcasted_iota(jnp.int32, m2.shape, 0) < GLA_DK)
                     == (lax.broadcasted_iota(jnp.int32, m2.shape, 1) < r))
        return jnp.where(same_head, m2, jnp.zeros_like(m2))

    o = None
    if level is not None:
        a = [jnp.zeros((GLA_CHUNK, LANES), F32) for _ in range(n_slab)]
        for lv in range(GLA_LEVELS + 1):
            if lv == 0:
                qt, kt = q.astype(BF16), k.astype(BF16)
            else:
                qt = (q * jnp.exp2(w[lv - 1])).astype(BF16)
                kt = (k * jnp.exp2(t[lv - 1] - w[lv - 1])).astype(BF16)
            for s in range(n_slab):
                sl = slice(s * LANES, (s + 1) * LANES)
                scores = jnp.dot(qt[:, sl], block_diag_t(kt[:, sl]), preferred_element_type=F32)
                a[s] = jnp.where(level == lv, scores, a[s])

        qb = (q * jnp.exp2(w[GLA_LEVELS])).astype(BF16)
        st_b = st.astype(BF16)
        outs = []
        for s in range(n_slab):
            sl = slice(s * LANES, (s + 1) * LANES)
            v_ab = v[:, (2 * s) * GLA_DV:(2 * s + 2) * GLA_DV]
            first = lax.broadcasted_iota(jnp.int32, v_ab.shape, 1) < GLA_DV
            zero = jnp.zeros_like(v_ab)
            v_bd = jnp.concatenate([jnp.where(first, v_ab, zero), jnp.where(first, zero, v_ab)], axis=0)
            outs.append(jnp.dot(jnp.concatenate([a[s].astype(BF16), qb[:, sl]], axis=1),
                                jnp.concatenate([v_bd, block_diag_t(st_b[:, sl])], axis=0),
                                preferred_element_type=F32))
        o = jnp.concatenate(outs, axis=1)

    tot = t[GLA_LEVELS]
    ku = (k * jnp.exp2(tot - w[GLA_LEVELS])).astype(BF16)
    decayed = st * jnp.exp2(tot[0:1, :])
    lo = lax.broadcasted_iota(jnp.int32, (GLA_DV, LANES), 1) < GLA_DK
    new_st = []
    for s in range(n_slab):
        sl = slice(s * LANES, (s + 1) * LANES)
        upd = _tn_dot(v[:, (2 * s) * GLA_DV:(2 * s + 2) * GLA_DV], ku[:, sl])
        new_st.append(decayed[:, sl] + jnp.where(lo, upd[:GLA_DV], upd[GLA_DV:]))
    return o, jnp.concatenate(new_st, axis=1)


def _gla_gate(o, r, nw):
    out = []
    for h in range(GLA_HEADS):
        sl = slice(h * GLA_DV, (h + 1) * GLA_DV)
        rh = r[:, sl].astype(F32)
        out.append((_rms(o[:, sl], nw) * (rh * jax.nn.sigmoid(rh))).astype(BF16))
    return jnp.concatenate(out, axis=1)


def _gla_kernel(in_ref, g_ref, km_ref, vm_ref, gm_ref, nw_ref, o_ref, st_ref):
    @pl.when(pl.program_id(1) == 0)
    def _():
        st_ref[...] = jnp.zeros(st_ref.shape, F32)
        _, st0 = _gla_chunk(None, km_ref[...].astype(F32), vm_ref[...], gm_ref[...], st_ref[...], None)
        st_ref[...] = st0

    nw = nw_ref[...]
    level = _gla_levels()
    st = st_ref[...]
    for c in range(in_ref.shape[1] // GLA_CHUNK):
        rows = slice(c * GLA_CHUNK, (c + 1) * GLA_CHUNK)
        o, st = _gla_chunk(in_ref[0, rows, 0:GQ].astype(F32), in_ref[0, rows, GQ:2 * GQ].astype(F32),
                           in_ref[0, rows, 2 * GQ:2 * GQ + GV], g_ref[0, rows, :], st, level)
        o_ref[0, rows, :] = _gla_gate(o, in_ref[0, rows, 2 * GQ + GV:2 * GQ + 2 * GV], nw)
    st_ref[...] = st


def _gla_mixer(gla_in, g, km, vm, gm, gla_norm_w, rows):
    b, s, _ = gla_in.shape

    def seq_spec(width):
        return pl.BlockSpec((1, rows, width), lambda i, j: (i, j, 0))

    def full(a):
        return pl.BlockSpec(a.shape, lambda i, j: (0,) * a.ndim)

    return pl.pallas_call(
        _gla_kernel,
        grid=(b, s // rows),
        in_specs=[seq_spec(gla_in.shape[-1]), seq_spec(GQ), full(km), full(vm), full(gm), full(gla_norm_w)],
        out_specs=seq_spec(GV),
        out_shape=jax.ShapeDtypeStruct((b, s, GV), BF16),
        scratch_shapes=[pltpu.VMEM((GLA_DV, GQ), F32)],
        compiler_params=pltpu.CompilerParams(dimension_semantics=("arbitrary", "arbitrary"),
                                             vmem_limit_bytes=VMEM_LIMIT),
        name="gla_mixer",
    )(gla_in, g, km, vm, gm, gla_norm_w)


def _swa_block(sink_ref, q, kc, kp, vc, vp, km, vm, has_prev):
    n_win = SWA_BLOCK + SWA_HALF
    n_keys = n_win + N_META
    r = lax.broadcasted_iota(jnp.int32, (SWA_HALF, n_keys), 0)
    c = lax.broadcasted_iota(jnp.int32, (SWA_HALF, n_keys), 1)
    in_band = (c > r) & (c <= r + SWA_BLOCK)
    lane = lax.broadcasted_iota(jnp.int32, (SWA_HALF, LANES), 1)
    low = lane < SWA_HD

    halves = []
    for half in range(2):
        rows = slice(half * SWA_HALF, (half + 1) * SWA_HALF)
        if half == 0:
            k_all = jnp.concatenate([kp, kc[:SWA_HALF], km], axis=0)
            v_all = jnp.concatenate([vp, vc[:SWA_HALF], vm], axis=0)
            n_prev = SWA_BLOCK
        else:
            k_all = jnp.concatenate([kp[SWA_HALF:], kc, km], axis=0)
            v_all = jnp.concatenate([vp[SWA_HALF:], vc, vm], axis=0)
            n_prev = SWA_HALF
        valid = (c >= n_win) | in_band
        if has_prev is not None:
            valid = (c >= n_win) | (in_band & (has_prev | (c >= n_prev)))
        bias = jnp.where(valid, 0.0, -jnp.inf)
        pieces = []
        for t in range(SWA_HEADS // 2):
            qt = q[rows, t * LANES:(t + 1) * LANES]
            pieces += [jnp.where(low, qt, jnp.zeros_like(qt)), jnp.where(low, jnp.zeros_like(qt), qt)]
        q_all = jnp.concatenate(pieces, axis=0)
        s_all = _nt_dot(q_all, k_all)
        p, denom = [], []
        for i in range(SWA_HEADS):
            s = s_all[i * SWA_HALF:(i + 1) * SWA_HALF] + bias
            sink = sink_ref[(i // 2) + (i % 2) * (SWA_HEADS // 2)] * LOG2_E
            m = jnp.maximum(jnp.max(s, axis=-1, keepdims=True), sink)
            p.append(jnp.exp2(s - m))
            denom.append(jnp.sum(p[i], axis=-1, keepdims=True) + jnp.exp2(sink - m))
        o = jnp.dot(jnp.concatenate(p, axis=0).astype(BF16), v_all, preferred_element_type=F32)
        slabs = []
        for t in range(SWA_HEADS // 2):
            lo = o[(2 * t) * SWA_HALF:(2 * t + 1) * SWA_HALF] / denom[2 * t]
            hi = o[(2 * t + 1) * SWA_HALF:(2 * t + 2) * SWA_HALF] / denom[2 * t + 1]
            slabs.append(jnp.where(low, lo, hi).astype(BF16))
        halves.append(jnp.concatenate(slabs, axis=1))
    return jnp.concatenate(halves, axis=0)


def _swa_ffn_kernel(tiles_per_seq, sink_ref, swa0_ref, swa_ref, prev_ref, km_ref, vm_ref,
                    x_ref, og_ref, wo_ref, nf_ref, w1_ref, w2_ref, nl_ref, y_ref, os_ref):
    s = pl.program_id(0)
    ahead = jnp.minimum(s + 1, pl.num_programs(0) - 1)
    seq_start = lax.rem(ahead, tiles_per_seq) == 0
    cur = lax.rem(s, 2)
    k_lanes, v_lanes = slice(SQ, SQ + SKV), slice(SQ + SKV, SQ + 2 * SKV)

    def swa_tile_block(j, slot, tile_ref, first_prev):
        rows = slice(j * SWA_BLOCK, (j + 1) * SWA_BLOCK)
        if j == 0:
            kp, vp, has_prev = first_prev
        else:
            before = slice((j - 1) * SWA_BLOCK, j * SWA_BLOCK)
            kp, vp, has_prev = tile_ref[before, k_lanes], tile_ref[before, v_lanes], None
        os_ref[slot, rows, :] = _swa_block(sink_ref, tile_ref[rows, 0:SQ], tile_ref[rows, k_lanes], kp,
                                           tile_ref[rows, v_lanes], vp, km_ref[...], vm_ref[...], has_prev)

    n_blocks = x_ref.shape[0] // SWA_BLOCK

    @pl.when(s == 0)
    def _():
        stand_in = (swa0_ref[0:SWA_BLOCK, k_lanes], swa0_ref[0:SWA_BLOCK, v_lanes], s > 0)
        for j in range(n_blocks):
            swa_tile_block(j, 0, swa0_ref, stand_in)

    n_ff = D_FF // FF_CHUNK
    swa_per_ff = n_blocks // n_ff

    h = (x_ref[...]
         + jnp.dot(og_ref[...], wo_ref[0:GV, :], preferred_element_type=F32)
         + jnp.dot(os_ref[cur], wo_ref[GV:GV + SQ, :], preferred_element_type=F32))
    f = _rms(h, nf_ref[...]).astype(BF16)
    ff = None
    for c in range(n_ff):
        sl = slice(c * FF_CHUNK, (c + 1) * FF_CHUNK)
        a = jnp.maximum(jnp.dot(f, w1_ref[:, sl], preferred_element_type=F32), 0.0)
        for j in range(c * swa_per_ff, (c + 1) * swa_per_ff):
            swa_tile_block(j, 1 - cur, swa_ref,
                           (prev_ref[:, 0:SKV], prev_ref[:, SKV:], jnp.logical_not(seq_start)))
        d = jnp.dot((a * a).astype(BF16), w2_ref[sl, :], preferred_element_type=F32)
        ff = d if ff is None else ff + d
    y_ref[...] = _rms(h + ff, nl_ref[...])


def _swa_ffn(sinks, swa_in, km, vm, x2, og, wo, nf, w1, w2, nl, rows, tiles_per_seq):
    n = x2.shape[0]
    n_tiles = n // rows
    prev_per_tile = rows // SWA_BLOCK
    width = swa_in.shape[1]

    first_spec = pl.BlockSpec((rows, width), lambda i: (0, 0), pipeline_mode=pl.Buffered(1))
    ahead_spec = pl.BlockSpec((rows, width), lambda i: (jnp.minimum(i + 1, n_tiles - 1), 0))
    prev_spec = pl.BlockSpec((SWA_BLOCK, 2 * SKV),
                             lambda i: (jnp.minimum(i + 1, n_tiles - 1) * prev_per_tile - 1, SQ // (2 * SKV)))

    def ffn_spec(width):
        return pl.BlockSpec((rows, width), lambda i: (i, 0))

    def full(a):
        return pl.BlockSpec(a.shape, lambda i: (0,) * a.ndim, pipeline_mode=pl.Buffered(1))

    weights = (wo, nf, w1, w2, nl)
    return pl.pallas_call(
        functools.partial(_swa_ffn_kernel, tiles_per_seq),
        grid=(n_tiles,),
        in_specs=([pl.BlockSpec(memory_space=pltpu.SMEM), first_spec, ahead_spec, prev_spec,
                   full(km), full(vm), ffn_spec(D_MODEL), ffn_spec(GV)] + [full(a) for a in weights]),
        out_specs=ffn_spec(D_MODEL),
        out_shape=jax.ShapeDtypeStruct((n, D_MODEL), F32),
        scratch_shapes=[pltpu.VMEM((2, rows, SQ), BF16)],
        compiler_params=pltpu.CompilerParams(dimension_semantics=("arbitrary",),
                                             vmem_limit_bytes=VMEM_LIMIT),
        name="swa_ffn",
    )(sinks, swa_in, swa_in, swa_in, km, vm, x2, og, *weights)


def _rope_cos_sin(first_pos, n_pos):
    inv_freq = 1.0 / (ROPE_THETA ** (jnp.arange(0, ROPE_DIM, 2, dtype=F32) / ROPE_DIM))
    ang = jnp.arange(first_pos, first_pos + n_pos, dtype=jnp.int32).astype(F32)[:, None] * inv_freq[None, :]
    return jnp.concatenate([jnp.cos(ang), jnp.sin(ang)], axis=1)


def _rope_spread():
    half = ROPE_DIM // 2
    lane = jnp.arange(2 * LANES)
    d = lane % SWA_HD
    which = lane // LANES
    src = which * half + d % half
    spread = jnp.where((d < ROPE_DIM)[None, :] & (jnp.arange(2 * half)[:, None] == src[None, :]), 1.0, 0.0)
    unrotated = jnp.where((which == 0) & (d >= ROPE_DIM), 1.0, 0.0)
    return jnp.concatenate([spread, spread], axis=0).astype(BF16), unrotated[None, :].astype(F32)


def _swa_head_order():
    order = []
    for t in range(SWA_HEADS // 2):
        order += [t, t + SWA_HEADS // 2]
    return order


def kernel(x, meta_tokens, norm_mix_w, w_in, w_gate_up, b_gate, gla_norm_w, sinks, w_out, norm_ff_w,
           w_ff1, w_ff2, final_norm_w):
    b, s, d = x.shape
    assert d == D_MODEL and s % ROWS_IN == 0 and s % ROWS_OUT == 0 and w_in.shape[0] == 1
    assert s % ROWS_GLA == 0 and ROWS_GLA % GLA_CHUNK == 0
    assert ROWS_OUT % (SWA_BLOCK * (D_FF // FF_CHUNK)) == 0

    bg = b_gate[0].reshape(1, GQ)
    nw = norm_mix_w[0].reshape(1, D_MODEL)
    cq = jnp.full((1, LANES), SWA_HD ** -0.5 * LOG2_E, F32)

    spread, unrot = _rope_spread()
    cs_meta = _rope_cos_sin(0, N_META)
    cs_tok = _rope_cos_sin(N_META, s)

    x2 = x.reshape(b * s, d)
    gla_in, g, swa_in, w1, w2, wo, km, vm, gm, m_sk, m_sv = _in_projection(
        x2, meta_tokens.astype(F32), w_in[0].T, w_gate_up[0], (nw, bg, cq), cs_tok, cs_meta, spread, unrot,
        w_ff1[0], w_ff2[0], w_out[0], ROWS_IN)

    def seq(a):
        return a.reshape(b, s, a.shape[-1])

    o_gla = _gla_mixer(seq(gla_in), seq(g), km, vm, gm, gla_norm_w[0].reshape(1, GLA_DV), ROWS_GLA)
    y = _swa_ffn(sinks[0].astype(F32), swa_in, m_sk, m_sv, x2, o_gla.reshape(b * s, GV), wo,
                 norm_ff_w[0].reshape(1, D_MODEL), w1, w2, final_norm_w.reshape(1, D_MODEL),
                 ROWS_OUT, s // ROWS_OUT)
    return y.reshape(b, s, d)
```

```python
import functools

import jax
import jax.numpy as jnp
from jax import lax
from jax.experimental import pallas as pl
from jax.experimental.pallas import tpu as pltpu

F32 = jnp.float32
BF16 = jnp.bfloat16

D_MODEL = 1024
N_META = 16
GLA_HEADS = 4
GLA_DK = 64
GLA_DV = 128
GLA_RANK = 16
GLA_TAU = 16.0
GLA_CHUNK = 64
GLA_LEVELS = 6
SWA_HEADS = 8
SWA_KV_HEADS = 2
SWA_HD = 64
SWA_BLOCK = 128
SWA_HALF = SWA_BLOCK // 2
ROPE_DIM = 16
ROPE_THETA = 500000.0
D_FF = 4096
EPS = 1e-5
LOG2_E = 1.4426950408889634

LANES = 128
GQ = GLA_HEADS * GLA_DK
GV = GLA_HEADS * GLA_DV
SQ = SWA_HEADS * SWA_HD
SKV = SWA_KV_HEADS * SWA_HD
FF_CHUNK = 1024

ROWS_IN = 1024
ROWS_IN_SUB = 256
W_SQ = 2 * GQ + 2 * GV
W_SK = W_SQ + SQ
W_LR = W_SK + 2 * SKV
ROWS_GLA = 1024
ROWS_OUT = 512
VMEM_LIMIT = 56 * 1024 * 1024


def _nt_dot(a, b):
    return lax.dot_general(a, b, (((1,), (1,)), ((), ())), preferred_element_type=F32)


def _tn_dot(a, b):
    return lax.dot_general(a, b, (((0,), (0,)), ((), ())), preferred_element_type=F32)


def _rms(x, w):
    return x * lax.rsqrt(jnp.mean(x * x, axis=-1, keepdims=True) + EPS) * w


def _project(normed, n_rows, w_ref, bg_ref, cq_ref, cs_ref, spread_ref, unrot_ref, gla_ref, g_ref, swa_ref,
             after=None):
    half = ROPE_DIM // 2
    n_sub = n_rows // ROWS_IN_SUB if n_rows > ROWS_IN_SUB else 1
    sub = n_rows // n_sub
    for i in range(n_sub):
        rows = slice(i * sub, (i + 1) * sub)
        u = normed(rows)

        def proj(lo, width):
            return jnp.dot(u, w_ref[:, lo:lo + width], preferred_element_type=F32)


        cs = cs_ref[rows, :]
        hi = cs.astype(BF16)
        lo = (cs - hi.astype(F32)).astype(BF16)
        pat = jnp.dot(jnp.concatenate([hi, lo], axis=1), spread_ref[...], preferred_element_type=F32) + unrot_ref[...]
        cos, sin = pat[:, 0:LANES], pat[:, LANES:]
        d = lax.broadcasted_iota(jnp.int32, cos.shape, 1) & (SWA_HD - 1)

        def rope(t):
            ts = t * sin
            turned = jnp.where(d < half, -pltpu.roll(ts, LANES - half, axis=1),
                               jnp.where(d < ROPE_DIM, pltpu.roll(ts, half, axis=1), 0.0))
            return t * cos + turned

        sq = proj(W_SQ, SQ)
        for s in range(SQ // LANES):
            sl = slice(s * LANES, (s + 1) * LANES)
            swa_ref[rows, sl] = (rope(sq[:, sl]) * cq_ref[...]).astype(BF16)
        kv = proj(W_SK, 2 * SKV)
        swa_ref[rows, SQ:SQ + SKV] = rope(kv[:, :SKV]).astype(BF16)
        swa_ref[rows, SQ + SKV:SQ + 2 * SKV] = kv[:, SKV:].astype(BF16)

        z = proj(W_LR, GQ) + bg_ref[...]
        g_ref[rows, :] = (jnp.minimum(z, 0.0) - jnp.log1p(jnp.exp(-jnp.abs(z)))) * (LOG2_E / GLA_TAU)

        qk = proj(0, 2 * GQ)
        gla_ref[rows, 0:GQ] = (qk[:, :GQ] * (GLA_DK ** -0.5)).astype(BF16)
        gla_ref[rows, GQ:2 * GQ] = qk[:, GQ:].astype(BF16)
        gla_ref[rows, 2 * GQ:2 * GQ + GV] = proj(2 * GQ, GV).astype(BF16)
        gla_ref[rows, 2 * GQ + GV:2 * GQ + 2 * GV] = proj(2 * GQ + GV, GV).astype(BF16)
        if after is not None:
            after(rows)


PROJ_WIDTHS = (2 * GQ + 2 * GV, GQ, SQ + 2 * SKV)
PROJ_DTYPES = (BF16, F32, BF16)


def _meta_prep_kernel(x_ref, wt_ref, nw_ref, wgu_ref, bg_ref, cq_ref, cs_ref, spread_ref, unrot_ref,
                      w_ref, km_ref, vm_ref, gm_ref, sk_ref, sv_ref, gla_s, g_s, swa_s):
    o_lr = 2 * GQ + 2 * GV
    o_sq = o_lr + GLA_RANK
    o_sk = o_sq + SQ
    half = SWA_HEADS // 2

    def rows(lo, n):
        return wt_ref[lo:lo + n, :].astype(BF16)

    slabs = [rows(c * LANES, LANES) for c in range(o_lr // LANES)]
    slabs += [jnp.concatenate([rows(o_sq + t * SWA_HD, SWA_HD), rows(o_sq + (t + half) * SWA_HD, SWA_HD)], axis=0)
              for t in range(half)]
    slabs += [rows(o_sk + c * LANES, LANES) for c in range(2 * SKV // LANES)]
    for c, slab in enumerate(slabs):
        w_ref[:, c * LANES:(c + 1) * LANES] = slab.T
    w_gate = lax.dot_general(wt_ref[o_lr:o_sq, :], wgu_ref[...], (((0,), (0,)), ((), ())),
                             precision=lax.Precision.HIGHEST, preferred_element_type=F32)
    w_ref[:, W_LR:W_LR + GQ] = w_gate.astype(BF16)
    _project(lambda rows: _rms(x_ref[rows, :], nw_ref[...]).astype(BF16), x_ref.shape[0],
             w_ref, bg_ref, cq_ref, cs_ref, spread_ref, unrot_ref, gla_s, g_s, swa_s)
    pad = GLA_CHUNK - N_META
    for ref, val in ((km_ref, gla_s[:, GQ:2 * GQ]), (vm_ref, gla_s[:, 2 * GQ:2 * GQ + GV]), (gm_ref, g_s[...])):
        ref[0:pad, :] = jnp.zeros((pad, ref.shape[1]), ref.dtype)
        ref[pad:, :] = val
    sk_ref[...] = swa_s[:, SQ:SQ + SKV]
    sv_ref[...] = swa_s[:, SQ + SKV:]


def _inproj_kernel(x0_ref, xa_ref, xm_ref, wt_ref, wgu_ref, nw_ref, bg_ref, cq_ref, cs_ref, csm_ref, spread_ref, unrot_ref,
                   w1_ref, w2_ref, wo_ref, *refs):
    *proj_refs, w1b_ref, w2b_ref, wob_ref, km_ref, vm_ref, gm_ref, sk_ref, sv_ref, u_ref, w_ref, gla_s, g_s, swa_s = refs

    @pl.when(pl.program_id(0) == 0)
    def _():
        _meta_prep_kernel(xm_ref, wt_ref, nw_ref, wgu_ref, bg_ref, cq_ref, csm_ref, spread_ref, unrot_ref,
                          w_ref, km_ref, vm_ref, gm_ref, sk_ref, sv_ref, gla_s, g_s, swa_s)
        u_ref[...] = _rms(x0_ref[...], nw_ref[...]).astype(BF16)

    def norm_ahead(rows):
        u_ref[rows, :] = _rms(xa_ref[rows, :], nw_ref[...]).astype(BF16)

    _project(lambda rows: u_ref[rows, :], xa_ref.shape[0], w_ref, bg_ref, cq_ref,
             cs_ref, spread_ref, unrot_ref, *proj_refs, after=norm_ahead)
    w1b_ref[...] = w1_ref[...].astype(BF16)
    w2b_ref[...] = w2_ref[...].astype(BF16)
    wob_ref[...] = wo_ref[...].astype(BF16)


def _in_projection(x2, meta_tokens, w_in_t, w_gate_up, small, cs, cs_meta, spread, unrot, w1, w2, wo, rows):
    n = x2.shape[0]
    n_meta = meta_tokens.shape[0]
    steps = n // rows
    nw, bg, cq = small
    nblk_pos = cs.shape[0] // rows
    assert w1.shape[0] % steps == 0 and w2.shape[0] % steps == 0 and wo.shape[0] == steps * SWA_HD

    def row_spec(width):
        return pl.BlockSpec((rows, width), lambda i: (i, 0))

    def full(a):
        return pl.BlockSpec(a.shape, lambda i: (0,) * a.ndim)

    def slice_spec(a):
        return pl.BlockSpec((a.shape[0] // steps, a.shape[1]), lambda i: (i, 0))

    n_gla = GV // SWA_HD
    half = SWA_HEADS // 2

    def wo_src(i):
        j = i - n_gla
        return (jnp.where(j < 0, i, n_gla + j // 2 + half * (j % 2)), 0)

    x_first = pl.BlockSpec((rows, D_MODEL), lambda i: (0, 0), pipeline_mode=pl.Buffered(1))
    x_ahead = pl.BlockSpec((rows, D_MODEL), lambda i: (jnp.minimum(i + 1, steps - 1), 0))
    def once(a):
        return pl.BlockSpec(a.shape, lambda i: (0,) * a.ndim, pipeline_mode=pl.Buffered(1))

    in_specs = ([x_first, x_ahead, once(meta_tokens), once(w_in_t), once(w_gate_up), full(nw), full(bg), full(cq),
                 pl.BlockSpec((rows, cs.shape[1]), lambda i: (i % nblk_pos, 0)), once(cs_meta), full(spread), full(unrot),
                 slice_spec(w1), slice_spec(w2), pl.BlockSpec((SWA_HD, wo.shape[1]), wo_src)])
    weights_out = [jax.ShapeDtypeStruct(a.shape, BF16) for a in (w1, w2, wo)]
    meta_out = [jax.ShapeDtypeStruct((GLA_CHUNK, GQ), BF16), jax.ShapeDtypeStruct((GLA_CHUNK, GV), BF16),
                jax.ShapeDtypeStruct((GLA_CHUNK, GQ), F32),
                jax.ShapeDtypeStruct((n_meta, SKV), BF16), jax.ShapeDtypeStruct((n_meta, SKV), BF16)]
    return pl.pallas_call(
        _inproj_kernel,
        grid=(steps,),
        in_specs=in_specs,
        out_specs=([row_spec(w) for w in PROJ_WIDTHS] + [slice_spec(w1), slice_spec(w2), slice_spec(wo)]
                   + [full(o) for o in meta_out]),
        out_shape=[jax.ShapeDtypeStruct((n, w), d) for w, d in zip(PROJ_WIDTHS, PROJ_DTYPES)] + weights_out + meta_out,
        scratch_shapes=([pltpu.VMEM((rows, D_MODEL), BF16),
                         pltpu.VMEM((D_MODEL, W_LR + GQ), BF16)]
                        + [pltpu.VMEM((n_meta, w), d) for w, d in zip(PROJ_WIDTHS, PROJ_DTYPES)]),
        compiler_params=pltpu.CompilerParams(dimension_semantics=("arbitrary",),
                                             vmem_limit_bytes=VMEM_LIMIT),
        name="in_projection",
    )(x2, x2, meta_tokens, w_in_t, w_gate_up, nw, bg, cq, cs, cs_meta, spread, unrot, w1, w2, wo)


def _block_cumsums(g):
    rows = g.shape[0]
    row = lax.broadcasted_iota(jnp.int32, g.shape, 0)
    w, t = [g], [g]
    for k in range(GLA_LEVELS):
        s = 1 << k
        upper = (row & s) != 0
        below = pltpu.roll(t[k], s, axis=0)
        above = pltpu.roll(t[k], rows - s, axis=0)
        w.append(w[k] + jnp.where(upper, below, 0.0))
        t.append(t[k] + jnp.where(upper, below, above))
    return w, t


def _gla_levels():
    ri = lax.broadcasted_iota(jnp.int32, (GLA_CHUNK, LANES), 0)
    ci = lax.broadcasted_iota(jnp.int32, (GLA_CHUNK, LANES), 1) & (GLA_CHUNK - 1)
    x = ri ^ ci
    level = jnp.zeros_like(x)
    for b in range(GLA_LEVELS):
        level = level + (x >= (1 << b)).astype(jnp.int32)
    return jnp.where(ci > ri, -1, level)


def _gla_chunk(q, k, v, g, st, level):
    w, t = _block_cumsums(g)
    n_slab = GQ // LANES

    def block_diag_t(m):
        r = m.shape[0]
        m2 = jnp.concatenate([m, m], axis=0).T
        same_head = ((lax.broadcasted_iota(jnp.int32, m2.shape, 0) < GLA_DK)
                     == (lax.broadcasted_iota(jnp.int32, m2.shape, 1) < r))
        return jnp.where(same_head, m2, jnp.zeros_like(m2))

    o = None
    if level is not None:
        a = [jnp.zeros((GLA_CHUNK, LANES), F32) for _ in range(n_slab)]
        for lv in range(GLA_LEVELS + 1):
            if lv == 0:
                qt, kt = q.astype(BF16), k.astype(BF16)
            else:
                qt = (q * jnp.exp2(w[lv - 1])).astype(BF16)
                kt = (k * jnp.exp2(t[lv - 1] - w[lv - 1])).astype(BF16)
            for s in range(n_slab):
                sl = slice(s * LANES, (s + 1) * LANES)
                scores = jnp.dot(qt[:, sl], block_diag_t(kt[:, sl]), preferred_element_type=F32)
                a[s] = jnp.where(level == lv, scores, a[s])

        qb = (q * jnp.exp2(w[GLA_LEVELS])).astype(BF16)
        st_b = st.astype(BF16)
        outs = []
        for s in range(n_slab):
            sl = slice(s * LANES, (s + 1) * LANES)
            v_ab = v[:, (2 * s) * GLA_DV:(2 * s + 2) * GLA_DV]
            first = lax.broadcasted_iota(jnp.int32, v_ab.shape, 1) < GLA_DV
            zero = jnp.zeros_like(v_ab)
            v_bd = jnp.concatenate([jnp.where(first, v_ab, zero), jnp.where(first, zero, v_ab)], axis=0)
            outs.append(jnp.dot(jnp.concatenate([a[s].astype(BF16), qb[:, sl]], axis=1),
                                jnp.concatenate([v_bd, block_diag_t(st_b[:, sl])], axis=0),
                                preferred_element_type=F32))
        o = jnp.concatenate(outs, axis=1)

    tot = t[GLA_LEVELS]
    ku = (k * jnp.exp2(tot - w[GLA_LEVELS])).astype(BF16)
    decayed = st * jnp.exp2(tot[0:1, :])
    lo = lax.broadcasted_iota(jnp.int32, (GLA_DV, LANES), 1) < GLA_DK
    new_st = []
    for s in range(n_slab):
        sl = slice(s * LANES, (s + 1) * LANES)
        upd = _tn_dot(v[:, (2 * s) * GLA_DV:(2 * s + 2) * GLA_DV], ku[:, sl])
        new_st.append(decayed[:, sl] + jnp.where(lo, upd[:GLA_DV], upd[GLA_DV:]))
    return o, jnp.concatenate(new_st, axis=1)


def _gla_gate(o, r, nw):
    out = []
    for h in range(GLA_HEADS):
        sl = slice(h * GLA_DV, (h + 1) * GLA_DV)
        rh = r[:, sl].astype(F32)
        out.append((_rms(o[:, sl], nw) * (rh * jax.nn.sigmoid(rh))).astype(BF16))
    return jnp.concatenate(out, axis=1)


def _gla_kernel(in_ref, g_ref, km_ref, vm_ref, gm_ref, nw_ref, o_ref, st_ref):
    @pl.when(pl.program_id(1) == 0)
    def _():
        st_ref[...] = jnp.zeros(st_ref.shape, F32)
        _, st0 = _gla_chunk(None, km_ref[...].astype(F32), vm_ref[...], gm_ref[...], st_ref[...], None)
        st_ref[...] = st0

    nw = nw_ref[...]
    level = _gla_levels()
    st = st_ref[...]
    for c in range(in_ref.shape[1] // GLA_CHUNK):
        rows = slice(c * GLA_CHUNK, (c + 1) * GLA_CHUNK)
        o, st = _gla_chunk(in_ref[0, rows, 0:GQ].astype(F32), in_ref[0, rows, GQ:2 * GQ].astype(F32),
                           in_ref[0, rows, 2 * GQ:2 * GQ + GV], g_ref[0, rows, :], st, level)
        o_ref[0, rows, :] = _gla_gate(o, in_ref[0, rows, 2 * GQ + GV:2 * GQ + 2 * GV], nw)
    st_ref[...] = st


def _gla_mixer(gla_in, g, km, vm, gm, gla_norm_w, rows):
    b, s, _ = gla_in.shape

    def seq_spec(width):
        return pl.BlockSpec((1, rows, width), lambda i, j: (i, j, 0))

    def full(a):
        return pl.BlockSpec(a.shape, lambda i, j: (0,) * a.ndim)

    return pl.pallas_call(
        _gla_kernel,
        grid=(b, s // rows),
        in_specs=[seq_spec(gla_in.shape[-1]), seq_spec(GQ), full(km), full(vm), full(gm), full(gla_norm_w)],
        out_specs=seq_spec(GV),
        out_shape=jax.ShapeDtypeStruct((b, s, GV), BF16),
        scratch_shapes=[pltpu.VMEM((GLA_DV, GQ), F32)],
        compiler_params=pltpu.CompilerParams(dimension_semantics=("arbitrary", "arbitrary"),
                                             vmem_limit_bytes=VMEM_LIMIT),
        name="gla_mixer",
    )(gla_in, g, km, vm, gm, gla_norm_w)


def _swa_block(sink_ref, q, kc, kp, vc, vp, km, vm, has_prev):
    n_win = SWA_BLOCK + SWA_HALF
    n_keys = n_win + N_META
    r = lax.broadcasted_iota(jnp.int32, (SWA_HALF, n_keys), 0)
    c = lax.broadcasted_iota(jnp.int32, (SWA_HALF, n_keys), 1)
    in_band = (c > r) & (c <= r + SWA_BLOCK)
    lane = lax.broadcasted_iota(jnp.int32, (SWA_HALF, LANES), 1)
    low = lane < SWA_HD

    halves = []
    for half in range(2):
        rows = slice(half * SWA_HALF, (half + 1) * SWA_HALF)
        if half == 0:
            k_all = jnp.concatenate([kp, kc[:SWA_HALF], km], axis=0)
            v_all = jnp.concatenate([vp, vc[:SWA_HALF], vm], axis=0)
            n_prev = SWA_BLOCK
        else:
            k_all = jnp.concatenate([kp[SWA_HALF:], kc, km], axis=0)
            v_all = jnp.concatenate([vp[SWA_HALF:], vc, vm], axis=0)
            n_prev = SWA_HALF
        valid = (c >= n_win) | in_band
        if has_prev is not None:
            valid = (c >= n_win) | (in_band & (has_prev | (c >= n_prev)))
        bias = jnp.where(valid, 0.0, -jnp.inf)
        pieces = []
        for t in range(SWA_HEADS // 2):
            qt = q[rows, t * LANES:(t + 1) * LANES]
            pieces += [jnp.where(low, qt, jnp.zeros_like(qt)), jnp.where(low, jnp.zeros_like(qt), qt)]
        q_all = jnp.concatenate(pieces, axis=0)
        s_all = _nt_dot(q_all, k_all)
        p, denom = [], []
        for i in range(SWA_HEADS):
            s = s_all[i * SWA_HALF:(i + 1) * SWA_HALF] + bias
            sink = sink_ref[(i // 2) + (i % 2) * (SWA_HEADS // 2)] * LOG2_E
            m = jnp.maximum(jnp.max(s, axis=-1, keepdims=True), sink)
            p.append(jnp.exp2(s - m))
            denom.append(jnp.sum(p[i], axis=-1, keepdims=True) + jnp.exp2(sink - m))
        o = jnp.dot(jnp.concatenate(p, axis=0).astype(BF16), v_all, preferred_element_type=F32)
        slabs = []
        for t in range(SWA_HEADS // 2):
            lo = o[(2 * t) * SWA_HALF:(2 * t + 1) * SWA_HALF] / denom[2 * t]
            hi = o[(2 * t + 1) * SWA_HALF:(2 * t + 2) * SWA_HALF] / denom[2 * t + 1]
            slabs.append(jnp.where(low, lo, hi).astype(BF16))
        halves.append(jnp.concatenate(slabs, axis=1))
    return jnp.concatenate(halves, axis=0)


def _swa_ffn_kernel(tiles_per_seq, sink_ref, swa0_ref, swa_ref, prev_ref, km_ref, vm_ref,
                    x_ref, og_ref, wo_hbm, nf_ref, w1_hbm, w2_hbm, nl_ref, y_ref,
                    os_ref, wo_ref, w1_ref, w2_ref, w_sem):
    s = pl.program_id(0)
    ahead = jnp.minimum(s + 1, pl.num_programs(0) - 1)
    seq_start = lax.rem(ahead, tiles_per_seq) == 0
    cur = lax.rem(s, 2)
    k_lanes, v_lanes = slice(SQ, SQ + SKV), slice(SQ + SKV, SQ + 2 * SKV)

    def swa_tile_block(j, slot, tile_ref, first_prev):
        rows = slice(j * SWA_BLOCK, (j + 1) * SWA_BLOCK)
        if j == 0:
            kp, vp, has_prev = first_prev
        else:
            before = slice((j - 1) * SWA_BLOCK, j * SWA_BLOCK)
            kp, vp, has_prev = tile_ref[before, k_lanes], tile_ref[before, v_lanes], None
        os_ref[slot, rows, :] = _swa_block(sink_ref, tile_ref[rows, 0:SQ], tile_ref[rows, k_lanes], kp,
                                           tile_ref[rows, v_lanes], vp, km_ref[...], vm_ref[...], has_prev)

    n_blocks = x_ref.shape[0] // SWA_BLOCK

    @pl.when(s == 0)
    def _():
        copies = [pltpu.make_async_copy(src, dst, w_sem.at[i])
                  for i, (src, dst) in enumerate(((wo_hbm, wo_ref), (w1_hbm, w1_ref), (w2_hbm, w2_ref)))]
        for copy in copies:
            copy.start()
        stand_in = (swa0_ref[0:SWA_BLOCK, k_lanes], swa0_ref[0:SWA_BLOCK, v_lanes], s > 0)
        for j in range(n_blocks):
            swa_tile_block(j, 0, swa0_ref, stand_in)
        for copy in copies:
            copy.wait()

    n_ff = D_FF // FF_CHUNK
    swa_per_ff = n_blocks // n_ff

    h = (x_ref[...]
         + jnp.dot(og_ref[...], wo_ref[0:GV, :], preferred_element_type=F32)
         + jnp.dot(os_ref[cur], wo_ref[GV:GV + SQ, :], preferred_element_type=F32))
    f = _rms(h, nf_ref[...]).astype(BF16)
    ff = None
    for c in range(n_ff):
        sl = slice(c * FF_CHUNK, (c + 1) * FF_CHUNK)
        a = jnp.maximum(jnp.dot(f, w1_ref[:, sl], preferred_element_type=F32), 0.0)
        for j in range(c * swa_per_ff, (c + 1) * swa_per_ff):
            swa_tile_block(j, 1 - cur, swa_ref,
                           (prev_ref[:, 0:SKV], prev_ref[:, SKV:], jnp.logical_not(seq_start)))
        d = jnp.dot((a * a).astype(BF16), w2_ref[sl, :], preferred_element_type=F32)
        ff = d if ff is None else ff + d
    y_ref[...] = _rms(h + ff, nl_ref[...])


def _swa_ffn(sinks, swa_in, km, vm, x2, og, wo, nf, w1, w2, nl, rows, tiles_per_seq):
    n = x2.shape[0]
    n_tiles = n // rows
    prev_per_tile = rows // SWA_BLOCK
    width = swa_in.shape[1]

    first_spec = pl.BlockSpec((rows, width), lambda i: (0, 0), pipeline_mode=pl.Buffered(1))
    ahead_spec = pl.BlockSpec((rows, width), lambda i: (jnp.minimum(i + 1, n_tiles - 1), 0))
    prev_spec = pl.BlockSpec((SWA_BLOCK, 2 * SKV),
                             lambda i: (jnp.minimum(i + 1, n_tiles - 1) * prev_per_tile - 1, SQ // (2 * SKV)))

    def ffn_spec(width):
        return pl.BlockSpec((rows, width), lambda i: (i, 0))

    def full(a):
        return pl.BlockSpec(a.shape, lambda i: (0,) * a.ndim, pipeline_mode=pl.Buffered(1))

    in_hbm = pl.BlockSpec(memory_space=pl.ANY)
    weights = (wo, nf, w1, w2, nl)
    weight_specs = [in_hbm, full(nf), in_hbm, in_hbm, full(nl)]
    return pl.pallas_call(
        functools.partial(_swa_ffn_kernel, tiles_per_seq),
        grid=(n_tiles,),
        in_specs=([pl.BlockSpec(memory_space=pltpu.SMEM), first_spec, ahead_spec, prev_spec,
                   full(km), full(vm), ffn_spec(D_MODEL), ffn_spec(GV)] + weight_specs),
        out_specs=ffn_spec(D_MODEL),
        out_shape=jax.ShapeDtypeStruct((n, D_MODEL), F32),
        scratch_shapes=([pltpu.VMEM((2, rows, SQ), BF16)]
                        + [pltpu.VMEM(a.shape, a.dtype) for a in (wo, w1, w2)]
                        + [pltpu.SemaphoreType.DMA((3,))]),
        compiler_params=pltpu.CompilerParams(dimension_semantics=("arbitrary",),
                                             vmem_limit_bytes=VMEM_LIMIT),
        name="swa_ffn",
    )(sinks, swa_in, swa_in, swa_in, km, vm, x2, og, *weights)


def _rope_cos_sin(first_pos, n_pos):
    inv_freq = 1.0 / (ROPE_THETA ** (jnp.arange(0, ROPE_DIM, 2, dtype=F32) / ROPE_DIM))
    ang = jnp.arange(first_pos, first_pos + n_pos, dtype=jnp.int32).astype(F32)[:, None] * inv_freq[None, :]
    return jnp.concatenate([jnp.cos(ang), jnp.sin(ang)], axis=1)


def _rope_spread():
    half = ROPE_DIM // 2
    lane = jnp.arange(2 * LANES)
    d = lane % SWA_HD
    which = lane // LANES
    src = which * half + d % half
    spread = jnp.where((d < ROPE_DIM)[None, :] & (jnp.arange(2 * half)[:, None] == src[None, :]), 1.0, 0.0)
    unrotated = jnp.where((which == 0) & (d >= ROPE_DIM), 1.0, 0.0)
    return jnp.concatenate([spread, spread], axis=0).astype(BF16), unrotated[None, :].astype(F32)


def _swa_head_order():
    order = []
    for t in range(SWA_HEADS // 2):
        order += [t, t + SWA_HEADS // 2]
    return order


def kernel(x, meta_tokens, norm_mix_w, w_in, w_gate_up, b_gate, gla_norm_w, sinks, w_out, norm_ff_w,
           w_ff1, w_ff2, final_norm_w):
    b, s, d = x.shape
    assert d == D_MODEL and s % ROWS_IN == 0 and s % ROWS_OUT == 0 and w_in.shape[0] == 1
    assert s % ROWS_GLA == 0 and ROWS_GLA % GLA_CHUNK == 0
    assert ROWS_OUT % (SWA_BLOCK * (D_FF // FF_CHUNK)) == 0

    bg = b_gate[0].reshape(1, GQ)
    nw = norm_mix_w[0].reshape(1, D_MODEL)
    cq = jnp.full((1, LANES), SWA_HD ** -0.5 * LOG2_E, F32)

    spread, unrot = _rope_spread()
    cs_meta = _rope_cos_sin(0, N_META)
    cs_tok = _rope_cos_sin(N_META, s)

    x2 = x.reshape(b * s, d)
    gla_in, g, swa_in, w1, w2, wo, km, vm, gm, m_sk, m_sv = _in_projection(
        x2, meta_tokens.astype(F32), w_in[0].T, w_gate_up[0], (nw, bg, cq), cs_tok, cs_meta, spread, unrot,
        w_ff1[0], w_ff2[0], w_out[0], ROWS_IN)

    def seq(a):
        return a.reshape(b, s, a.shape[-1])

    o_gla = _gla_mixer(seq(gla_in), seq(g), km, vm, gm, gla_norm_w[0].reshape(1, GLA_DV), ROWS_GLA)
    y = _swa_ffn(sinks[0].astype(F32), swa_in, m_sk, m_sv, x2, o_gla.reshape(b * s, GV), wo,
                 norm_ff_w[0].reshape(1, D_MODEL), w1, w2, final_norm_w.reshape(1, D_MODEL),
                 ROWS_OUT, s // ROWS_OUT)
    return y.reshape(b, s, d)
```

```python
import functools

import jax
import jax.numpy as jnp
from jax import lax
from jax.experimental import pallas as pl
from jax.experimental.pallas import tpu as pltpu

F32 = jnp.float32
BF16 = jnp.bfloat16

D_MODEL = 1024
N_META = 16
GLA_HEADS = 4
GLA_DK = 64
GLA_DV = 128
GLA_RANK = 16
GLA_TAU = 16.0
GLA_CHUNK = 64
GLA_LEVELS = 6
SWA_HEADS = 8
SWA_KV_HEADS = 2
SWA_HD = 64
SWA_BLOCK = 128
SWA_HALF = SWA_BLOCK // 2
ROPE_DIM = 16
ROPE_THETA = 500000.0
D_FF = 4096
EPS = 1e-5
LOG2_E = 1.4426950408889634

LANES = 128
GQ = GLA_HEADS * GLA_DK
GV = GLA_HEADS * GLA_DV
SQ = SWA_HEADS * SWA_HD
SKV = SWA_KV_HEADS * SWA_HD
FF_CHUNK = 2048

ROWS_IN = 1024
ROWS_IN_SUB = 256
W_SQ = 2 * GQ + 2 * GV
W_SK = W_SQ + SQ
W_LR = W_SK + 2 * SKV
ROWS_GLA = 1024
ROWS_OUT = 512
VMEM_LIMIT = 56 * 1024 * 1024


def _nt_dot(a, b):
    return lax.dot_general(a, b, (((1,), (1,)), ((), ())), preferred_element_type=F32)


def _tn_dot(a, b):
    return lax.dot_general(a, b, (((0,), (0,)), ((), ())), preferred_element_type=F32)


def _rms(x, w):
    return x * lax.rsqrt(jnp.mean(x * x, axis=-1, keepdims=True) + EPS) * w


def _project(normed, n_rows, w_ref, bg_ref, cq_ref, cs_ref, spread_ref, unrot_ref, gla_ref, g_ref, swa_ref,
             after=None):
    half = ROPE_DIM // 2
    n_sub = n_rows // ROWS_IN_SUB if n_rows > ROWS_IN_SUB else 1
    sub = n_rows // n_sub
    for i in range(n_sub):
        rows = slice(i * sub, (i + 1) * sub)
        u = normed(rows)

        def proj(lo, width):
            return jnp.dot(u, w_ref[:, lo:lo + width], preferred_element_type=F32)


        cs = cs_ref[rows, :]
        hi = cs.astype(BF16)
        lo = (cs - hi.astype(F32)).astype(BF16)
        pat = jnp.dot(jnp.concatenate([hi, lo], axis=1), spread_ref[...], preferred_element_type=F32) + unrot_ref[...]
        cos, sin = pat[:, 0:LANES], pat[:, LANES:]
        d = lax.broadcasted_iota(jnp.int32, cos.shape, 1) & (SWA_HD - 1)

        def rope(t):
            ts = t * sin
            turned = jnp.where(d < half, -pltpu.roll(ts, LANES - half, axis=1),
                               jnp.where(d < ROPE_DIM, pltpu.roll(ts, half, axis=1), 0.0))
            return t * cos + turned

        sq = proj(W_SQ, SQ)
        for s in range(SQ // LANES):
            sl = slice(s * LANES, (s + 1) * LANES)
            swa_ref[rows, sl] = (rope(sq[:, sl]) * cq_ref[...]).astype(BF16)
        kv = proj(W_SK, 2 * SKV)
        swa_ref[rows, SQ:SQ + SKV] = rope(kv[:, :SKV]).astype(BF16)
        swa_ref[rows, SQ + SKV:SQ + 2 * SKV] = kv[:, SKV:].astype(BF16)

        z = proj(W_LR, GQ) + bg_ref[...]
        g_ref[rows, :] = (jnp.minimum(z, 0.0) - jnp.log1p(jnp.exp(-jnp.abs(z)))) * (LOG2_E / GLA_TAU)

        qk = proj(0, 2 * GQ)
        gla_ref[rows, 0:GQ] = (qk[:, :GQ] * (GLA_DK ** -0.5)).astype(BF16)
        gla_ref[rows, GQ:2 * GQ] = qk[:, GQ:].astype(BF16)
        gla_ref[rows, 2 * GQ:2 * GQ + GV] = proj(2 * GQ, GV).astype(BF16)
        gla_ref[rows, 2 * GQ + GV:2 * GQ + 2 * GV] = proj(2 * GQ + GV, GV).astype(BF16)
        if after is not None:
            after(rows)


PROJ_WIDTHS = (2 * GQ + 2 * GV, GQ, SQ + 2 * SKV)
PROJ_DTYPES = (BF16, F32, BF16)


def _meta_prep_kernel(x_ref, wt_ref, nw_ref, wgu_ref, bg_ref, cq_ref, cs_ref, spread_ref, unrot_ref,
                      w_ref, km_ref, vm_ref, gm_ref, sk_ref, sv_ref, gla_s, g_s, swa_s):
    o_lr = 2 * GQ + 2 * GV
    o_sq = o_lr + GLA_RANK
    o_sk = o_sq + SQ
    half = SWA_HEADS // 2

    def rows(lo, n):
        return wt_ref[lo:lo + n, :].astype(BF16)

    slabs = [rows(c * LANES, LANES) for c in range(o_lr // LANES)]
    slabs += [jnp.concatenate([rows(o_sq + t * SWA_HD, SWA_HD), rows(o_sq + (t + half) * SWA_HD, SWA_HD)], axis=0)
              for t in range(half)]
    slabs += [rows(o_sk + c * LANES, LANES) for c in range(2 * SKV // LANES)]
    for c, slab in enumerate(slabs):
        w_ref[:, c * LANES:(c + 1) * LANES] = slab.T
    w_gate = lax.dot_general(wt_ref[o_lr:o_sq, :], wgu_ref[...], (((0,), (0,)), ((), ())),
                             precision=lax.Precision.HIGHEST, preferred_element_type=F32)
    w_ref[:, W_LR:W_LR + GQ] = w_gate.astype(BF16)
    _project(lambda rows: _rms(x_ref[rows, :], nw_ref[...]).astype(BF16), x_ref.shape[0],
             w_ref, bg_ref, cq_ref, cs_ref, spread_ref, unrot_ref, gla_s, g_s, swa_s)
    pad = GLA_CHUNK - N_META
    for ref, val in ((km_ref, gla_s[:, GQ:2 * GQ]), (vm_ref, gla_s[:, 2 * GQ:2 * GQ + GV]), (gm_ref, g_s[...])):
        ref[0:pad, :] = jnp.zeros((pad, ref.shape[1]), ref.dtype)
        ref[pad:, :] = val
    sk_ref[...] = swa_s[:, SQ:SQ + SKV]
    sv_ref[...] = swa_s[:, SQ + SKV:]


def _inproj_kernel(x0_ref, xa_ref, xm_ref, wt_ref, wgu_ref, nw_ref, bg_ref, cq_ref, cs_ref, csm_ref, spread_ref, unrot_ref,
                   w1_ref, w2_ref, wo_ref, *refs):
    *proj_refs, w1b_ref, w2b_ref, wob_ref, km_ref, vm_ref, gm_ref, sk_ref, sv_ref, u_ref, w_ref, gla_s, g_s, swa_s = refs

    @pl.when(pl.program_id(0) == 0)
    def _():
        _meta_prep_kernel(xm_ref, wt_ref, nw_ref, wgu_ref, bg_ref, cq_ref, csm_ref, spread_ref, unrot_ref,
                          w_ref, km_ref, vm_ref, gm_ref, sk_ref, sv_ref, gla_s, g_s, swa_s)
        u_ref[...] = _rms(x0_ref[...], nw_ref[...]).astype(BF16)

    def norm_ahead(rows):
        u_ref[rows, :] = _rms(xa_ref[rows, :], nw_ref[...]).astype(BF16)

    _project(lambda rows: u_ref[rows, :], xa_ref.shape[0], w_ref, bg_ref, cq_ref,
             cs_ref, spread_ref, unrot_ref, *proj_refs, after=norm_ahead)
    w1b_ref[...] = w1_ref[...].astype(BF16)
    w2b_ref[...] = w2_ref[...].astype(BF16)
    wob_ref[...] = wo_ref[...].astype(BF16)


def _in_projection(x2, meta_tokens, w_in_t, w_gate_up, small, cs, cs_meta, spread, unrot, w1, w2, wo, rows):
    n = x2.shape[0]
    n_meta = meta_tokens.shape[0]
    steps = n // rows
    nw, bg, cq = small
    nblk_pos = cs.shape[0] // rows
    assert w1.shape[0] % steps == 0 and w2.shape[0] % steps == 0 and wo.shape[0] == steps * SWA_HD

    def row_spec(width):
        return pl.BlockSpec((rows, width), lambda i: (i, 0))

    def full(a):
        return pl.BlockSpec(a.shape, lambda i: (0,) * a.ndim)

    def slice_spec(a):
        return pl.BlockSpec((a.shape[0] // steps, a.shape[1]), lambda i: (i, 0))

    n_gla = GV // SWA_HD
    half = SWA_HEADS // 2

    def wo_src(i):
        j = i - n_gla
        return (jnp.where(j < 0, i, n_gla + j // 2 + half * (j % 2)), 0)

    x_first = pl.BlockSpec((rows, D_MODEL), lambda i: (0, 0), pipeline_mode=pl.Buffered(1))
    x_ahead = pl.BlockSpec((rows, D_MODEL), lambda i: (jnp.minimum(i + 1, steps - 1), 0))
    def once(a):
        return pl.BlockSpec(a.shape, lambda i: (0,) * a.ndim, pipeline_mode=pl.Buffered(1))

    in_specs = ([x_first, x_ahead, once(meta_tokens), once(w_in_t), once(w_gate_up), full(nw), full(bg), full(cq),
                 pl.BlockSpec((rows, cs.shape[1]), lambda i: (i % nblk_pos, 0)), once(cs_meta), full(spread), full(unrot),
                 slice_spec(w1), slice_spec(w2), pl.BlockSpec((SWA_HD, wo.shape[1]), wo_src)])
    weights_out = [jax.ShapeDtypeStruct(a.shape, BF16) for a in (w1, w2, wo)]
    meta_out = [jax.ShapeDtypeStruct((GLA_CHUNK, GQ), BF16), jax.ShapeDtypeStruct((GLA_CHUNK, GV), BF16),
                jax.ShapeDtypeStruct((GLA_CHUNK, GQ), F32),
                jax.ShapeDtypeStruct((n_meta, SKV), BF16), jax.ShapeDtypeStruct((n_meta, SKV), BF16)]
    return pl.pallas_call(
        _inproj_kernel,
        grid=(steps,),
        in_specs=in_specs,
        out_specs=([row_spec(w) for w in PROJ_WIDTHS] + [slice_spec(w1), slice_spec(w2), slice_spec(wo)]
                   + [full(o) for o in meta_out]),
        out_shape=[jax.ShapeDtypeStruct((n, w), d) for w, d in zip(PROJ_WIDTHS, PROJ_DTYPES)] + weights_out + meta_out,
        scratch_shapes=([pltpu.VMEM((rows, D_MODEL), BF16),
                         pltpu.VMEM((D_MODEL, W_LR + GQ), BF16)]
                        + [pltpu.VMEM((n_meta, w), d) for w, d in zip(PROJ_WIDTHS, PROJ_DTYPES)]),
        compiler_params=pltpu.CompilerParams(dimension_semantics=("arbitrary",),
                                             vmem_limit_bytes=VMEM_LIMIT),
        name="in_projection",
    )(x2, x2, meta_tokens, w_in_t, w_gate_up, nw, bg, cq, cs, cs_meta, spread, unrot, w1, w2, wo)


def _block_cumsums(g):
    rows = g.shape[0]
    row = lax.broadcasted_iota(jnp.int32, g.shape, 0)
    w, t = [g], [g]
    for k in range(GLA_LEVELS):
        s = 1 << k
        upper = (row & s) != 0
        below = pltpu.roll(t[k], s, axis=0)
        above = pltpu.roll(t[k], rows - s, axis=0)
        w.append(w[k] + jnp.where(upper, below, 0.0))
        t.append(t[k] + jnp.where(upper, below, above))
    return w, t


def _gla_levels():
    ri = lax.broadcasted_iota(jnp.int32, (GLA_CHUNK, LANES), 0)
    ci = lax.broadcasted_iota(jnp.int32, (GLA_CHUNK, LANES), 1) & (GLA_CHUNK - 1)
    x = ri ^ ci
    level = jnp.zeros_like(x)
    for b in range(GLA_LEVELS):
        level = level + (x >= (1 << b)).astype(jnp.int32)
    return jnp.where(ci > ri, -1, level)


def _gla_chunk(q, k, v, g, st, level):
    w, t = _block_cumsums(g)
    n_slab = GQ // LANES

    def block_diag_t(m):
        r = m.shape[0]
        m2 = jnp.concatenate([m, m], axis=0).T
        same_head = ((lax.broadcasted_iota(jnp.int32, m2.shape, 0) < GLA_DK)
                     == (lax.broadcasted_iota(jnp.int32, m2.shape, 1) < r))
        return jnp.where(same_head, m2, jnp.zeros_like(m2))

    o = None
    if level is not None:
        a = [jnp.zeros((GLA_CHUNK, LANES), F32) for _ in range(n_slab)]
        for lv in range(GLA_LEVELS + 1):
            if lv == 0:
                qt, kt = q.astype(BF16), k.astype(BF16)
            else:
                qt = (q * jnp.exp2(w[lv - 1])).astype(BF16)
                kt = (k * jnp.exp2(t[lv - 1] - w[lv - 1])).astype(BF16)
            for s in range(n_slab):
                sl = slice(s * LANES, (s + 1) * LANES)
                scores = jnp.dot(qt[:, sl], block_diag_t(kt[:, sl]), preferred_element_type=F32)
                a[s] = jnp.where(level == lv, scores, a[s])

        qb = (q * jnp.exp2(w[GLA_LEVELS])).astype(BF16)
        st_b = st.astype(BF16)
        outs = []
        for s in range(n_slab):
            sl = slice(s * LANES, (s + 1) * LANES)
            v_ab = v[:, (2 * s) * GLA_DV:(2 * s + 2) * GLA_DV]
            first = lax.broadcasted_iota(jnp.int32, v_ab.shape, 1) < GLA_DV
            zero = jnp.zeros_like(v_ab)
            v_bd = jnp.concatenate([jnp.where(first, v_ab, zero), jnp.where(first, zero, v_ab)], axis=0)
            outs.append(jnp.dot(jnp.concatenate([a[s].astype(BF16), qb[:, sl]], axis=1),
                                jnp.concatenate([v_bd, block_diag_t(st_b[:, sl])], axis=0),
                                preferred_element_type=F32))
        o = jnp.concatenate(outs, axis=1)

    tot = t[GLA_LEVELS]
    ku = (k * jnp.exp2(tot - w[GLA_LEVELS])).astype(BF16)
    decayed = st * jnp.exp2(tot[0:1, :])
    lo = lax.broadcasted_iota(jnp.int32, (GLA_DV, LANES), 1) < GLA_DK
    new_st = []
    for s in range(n_slab):
        sl = slice(s * LANES, (s + 1) * LANES)
        upd = _tn_dot(v[:, (2 * s) * GLA_DV:(2 * s + 2) * GLA_DV], ku[:, sl])
        new_st.append(decayed[:, sl] + jnp.where(lo, upd[:GLA_DV], upd[GLA_DV:]))
    return o, jnp.concatenate(new_st, axis=1)


def _gla_gate(o, r, nw):
    out = []
    for h in range(GLA_HEADS):
        sl = slice(h * GLA_DV, (h + 1) * GLA_DV)
        rh = r[:, sl].astype(F32)
        out.append((_rms(o[:, sl], nw) * (rh * jax.nn.sigmoid(rh))).astype(BF16))
    return jnp.concatenate(out, axis=1)


def _gla_kernel(in_ref, g_ref, km_ref, vm_ref, gm_ref, nw_ref, o_ref, st_ref):
    @pl.when(pl.program_id(1) == 0)
    def _():
        st_ref[...] = jnp.zeros(st_ref.shape, F32)
        _, st0 = _gla_chunk(None, km_ref[...].astype(F32), vm_ref[...], gm_ref[...], st_ref[...], None)
        st_ref[...] = st0

    nw = nw_ref[...]
    level = _gla_levels()
    st = st_ref[...]
    for c in range(in_ref.shape[1] // GLA_CHUNK):
        rows = slice(c * GLA_CHUNK, (c + 1) * GLA_CHUNK)
        o, st = _gla_chunk(in_ref[0, rows, 0:GQ].astype(F32), in_ref[0, rows, GQ:2 * GQ].astype(F32),
                           in_ref[0, rows, 2 * GQ:2 * GQ + GV], g_ref[0, rows, :], st, level)
        o_ref[0, rows, :] = _gla_gate(o, in_ref[0, rows, 2 * GQ + GV:2 * GQ + 2 * GV], nw)
    st_ref[...] = st


def _gla_mixer(gla_in, g, km, vm, gm, gla_norm_w, rows):
    b, s, _ = gla_in.shape

    def seq_spec(width):
        return pl.BlockSpec((1, rows, width), lambda i, j: (i, j, 0))

    def full(a):
        return pl.BlockSpec(a.shape, lambda i, j: (0,) * a.ndim)

    return pl.pallas_call(
        _gla_kernel,
        grid=(b, s // rows),
        in_specs=[seq_spec(gla_in.shape[-1]), seq_spec(GQ), full(km), full(vm), full(gm), full(gla_norm_w)],
        out_specs=seq_spec(GV),
        out_shape=jax.ShapeDtypeStruct((b, s, GV), BF16),
        scratch_shapes=[pltpu.VMEM((GLA_DV, GQ), F32)],
        compiler_params=pltpu.CompilerParams(dimension_semantics=("arbitrary", "arbitrary"),
                                             vmem_limit_bytes=VMEM_LIMIT),
        name="gla_mixer",
    )(gla_in, g, km, vm, gm, gla_norm_w)


def _swa_block(sink_ref, q, kc, kp, vc, vp, km, vm, has_prev):
    n_win = SWA_BLOCK + SWA_HALF
    n_keys = n_win + N_META
    r = lax.broadcasted_iota(jnp.int32, (SWA_HALF, n_keys), 0)
    c = lax.broadcasted_iota(jnp.int32, (SWA_HALF, n_keys), 1)
    in_band = (c > r) & (c <= r + SWA_BLOCK)
    lane = lax.broadcasted_iota(jnp.int32, (SWA_HALF, LANES), 1)
    low = lane < SWA_HD

    halves = []
    for half in range(2):
        rows = slice(half * SWA_HALF, (half + 1) * SWA_HALF)
        if half == 0:
            k_all = jnp.concatenate([kp, kc[:SWA_HALF], km], axis=0)
            v_all = jnp.concatenate([vp, vc[:SWA_HALF], vm], axis=0)
            n_prev = SWA_BLOCK
        else:
            k_all = jnp.concatenate([kp[SWA_HALF:], kc, km], axis=0)
            v_all = jnp.concatenate([vp[SWA_HALF:], vc, vm], axis=0)
            n_prev = SWA_HALF
        valid = (c >= n_win) | in_band
        if has_prev is not None:
            valid = (c >= n_win) | (in_band & (has_prev | (c >= n_prev)))
        bias = jnp.where(valid, 0.0, -jnp.inf)
        pieces = []
        for t in range(SWA_HEADS // 2):
            qt = q[rows, t * LANES:(t + 1) * LANES]
            pieces += [jnp.where(low, qt, jnp.zeros_like(qt)), jnp.where(low, jnp.zeros_like(qt), qt)]
        q_all = jnp.concatenate(pieces, axis=0)
        s_all = _nt_dot(q_all, k_all)
        p, denom = [], []
        for i in range(SWA_HEADS):
            s = s_all[i * SWA_HALF:(i + 1) * SWA_HALF] + bias
            sink = sink_ref[(i // 2) + (i % 2) * (SWA_HEADS // 2)] * LOG2_E
            m = jnp.maximum(jnp.max(s, axis=-1, keepdims=True), sink)
            p.append(jnp.exp2(s - m))
            denom.append(jnp.sum(p[i], axis=-1, keepdims=True) + jnp.exp2(sink - m))
        o = jnp.dot(jnp.concatenate(p, axis=0).astype(BF16), v_all, preferred_element_type=F32)
        slabs = []
        for t in range(SWA_HEADS // 2):
            lo = o[(2 * t) * SWA_HALF:(2 * t + 1) * SWA_HALF] / denom[2 * t]
            hi = o[(2 * t + 1) * SWA_HALF:(2 * t + 2) * SWA_HALF] / denom[2 * t + 1]
            slabs.append(jnp.where(low, lo, hi).astype(BF16))
        halves.append(jnp.concatenate(slabs, axis=1))
    return jnp.concatenate(halves, axis=0)


def _swa_ffn_kernel(tiles_per_seq, sink_ref, swa0_ref, swa_ref, prev_ref, km_ref, vm_ref,
                    x_ref, og_ref, wo_ref, nf_ref, w1_ref, w2_ref, nl_ref, y_ref, os_ref):
    s = pl.program_id(0)
    ahead = jnp.minimum(s + 1, pl.num_programs(0) - 1)
    seq_start = lax.rem(ahead, tiles_per_seq) == 0
    cur = lax.rem(s, 2)
    k_lanes, v_lanes = slice(SQ, SQ + SKV), slice(SQ + SKV, SQ + 2 * SKV)

    def swa_tile_block(j, slot, tile_ref, first_prev):
        rows = slice(j * SWA_BLOCK, (j + 1) * SWA_BLOCK)
        if j == 0:
            kp, vp, has_prev = first_prev
        else:
            before = slice((j - 1) * SWA_BLOCK, j * SWA_BLOCK)
            kp, vp, has_prev = tile_ref[before, k_lanes], tile_ref[before, v_lanes], None
        os_ref[slot, rows, :] = _swa_block(sink_ref, tile_ref[rows, 0:SQ], tile_ref[rows, k_lanes], kp,
                                           tile_ref[rows, v_lanes], vp, km_ref[...], vm_ref[...], has_prev)

    n_blocks = x_ref.shape[0] // SWA_BLOCK

    @pl.when(s == 0)
    def _():
        stand_in = (swa0_ref[0:SWA_BLOCK, k_lanes], swa0_ref[0:SWA_BLOCK, v_lanes], s > 0)
        for j in range(n_blocks):
            swa_tile_block(j, 0, swa0_ref, stand_in)

    n_ff = D_FF // FF_CHUNK
    swa_per_ff = n_blocks // n_ff

    h = (x_ref[...]
         + jnp.dot(og_ref[...], wo_ref[0:GV, :], preferred_element_type=F32)
         + jnp.dot(os_ref[cur], wo_ref[GV:GV + SQ, :], preferred_element_type=F32))
    f = _rms(h, nf_ref[...]).astype(BF16)
    ff = None
    for c in range(n_ff):
        sl = slice(c * FF_CHUNK, (c + 1) * FF_CHUNK)
        a = jnp.maximum(jnp.dot(f, w1_ref[:, sl], preferred_element_type=F32), 0.0)
        for j in range(c * swa_per_ff, (c + 1) * swa_per_ff):
            swa_tile_block(j, 1 - cur, swa_ref,
                           (prev_ref[:, 0:SKV], prev_ref[:, SKV:], jnp.logical_not(seq_start)))
        d = jnp.dot((a * a).astype(BF16), w2_ref[sl, :], preferred_element_type=F32)
        ff = d if ff is None else ff + d
    y_ref[...] = _rms(h + ff, nl_ref[...])


def _swa_ffn(sinks, swa_in, km, vm, x2, og, wo, nf, w1, w2, nl, rows, tiles_per_seq):
    n = x2.shape[0]
    n_tiles = n // rows
    prev_per_tile = rows // SWA_BLOCK
    width = swa_in.shape[1]

    first_spec = pl.BlockSpec((rows, width), lambda i: (0, 0), pipeline_mode=pl.Buffered(1))
    ahead_spec = pl.BlockSpec((rows, width), lambda i: (jnp.minimum(i + 1, n_tiles - 1), 0))
    prev_spec = pl.BlockSpec((SWA_BLOCK, 2 * SKV),
                             lambda i: (jnp.minimum(i + 1, n_tiles - 1) * prev_per_tile - 1, SQ // (2 * SKV)))

    def ffn_spec(width):
        return pl.BlockSpec((rows, width), lambda i: (i, 0))

    def full(a):
        return pl.BlockSpec(a.shape, lambda i: (0,) * a.ndim, pipeline_mode=pl.Buffered(1))

    weights = (wo, nf, w1, w2, nl)
    return pl.pallas_call(
        functools.partial(_swa_ffn_kernel, tiles_per_seq),
        grid=(n_tiles,),
        in_specs=([pl.BlockSpec(memory_space=pltpu.SMEM), first_spec, ahead_spec, prev_spec,
                   full(km), full(vm), ffn_spec(D_MODEL), ffn_spec(GV)] + [full(a) for a in weights]),
        out_specs=ffn_spec(D_MODEL),
        out_shape=jax.ShapeDtypeStruct((n, D_MODEL), F32),
        scratch_shapes=[pltpu.VMEM((2, rows, SQ), BF16)],
        compiler_params=pltpu.CompilerParams(dimension_semantics=("arbitrary",),
                                             vmem_limit_bytes=VMEM_LIMIT),
        name="swa_ffn",
    )(sinks, swa_in, swa_in, swa_in, km, vm, x2, og, *weights)


def _rope_cos_sin(first_pos, n_pos):
    inv_freq = 1.0 / (ROPE_THETA ** (jnp.arange(0, ROPE_DIM, 2, dtype=F32) / ROPE_DIM))
    ang = jnp.arange(first_pos, first_pos + n_pos, dtype=jnp.int32).astype(F32)[:, None] * inv_freq[None, :]
    return jnp.concatenate([jnp.cos(ang), jnp.sin(ang)], axis=1)


def _rope_spread():
    half = ROPE_DIM // 2
    lane = jnp.arange(2 * LANES)
    d = lane % SWA_HD
    which = lane // LANES
    src = which * half + d % half
    spread = jnp.where((d < ROPE_DIM)[None, :] & (jnp.arange(2 * half)[:, None] == src[None, :]), 1.0, 0.0)
    unrotated = jnp.where((which == 0) & (d >= ROPE_DIM), 1.0, 0.0)
    return jnp.concatenate([spread, spread], axis=0).astype(BF16), unrotated[None, :].astype(F32)


def _swa_head_order():
    order = []
    for t in range(SWA_HEADS // 2):
        order += [t, t + SWA_HEADS // 2]
    return order


def kernel(x, meta_tokens, norm_mix_w, w_in, w_gate_up, b_gate, gla_norm_w, sinks, w_out, norm_ff_w,
           w_ff1, w_ff2, final_norm_w):
    b, s, d = x.shape
    assert d == D_MODEL and s % ROWS_IN == 0 and s % ROWS_OUT == 0 and w_in.shape[0] == 1
    assert s % ROWS_GLA == 0 and ROWS_GLA % GLA_CHUNK == 0
    assert ROWS_OUT % (SWA_BLOCK * (D_FF // FF_CHUNK)) == 0

    bg = b_gate[0].reshape(1, GQ)
    nw = norm_mix_w[0].reshape(1, D_MODEL)
    cq = jnp.full((1, LANES), SWA_HD ** -0.5 * LOG2_E, F32)

    spread, unrot = _rope_spread()
    cs_meta = _rope_cos_sin(0, N_META)
    cs_tok = _rope_cos_sin(N_META, s)

    x2 = x.reshape(b * s, d)
    gla_in, g, swa_in, w1, w2, wo, km, vm, gm, m_sk, m_sv = _in_projection(
        x2, meta_tokens.astype(F32), w_in[0].T, w_gate_up[0], (nw, bg, cq), cs_tok, cs_meta, spread, unrot,
        w_ff1[0], w_ff2[0], w_out[0], ROWS_IN)

    def seq(a):
        return a.reshape(b, s, a.shape[-1])

    o_gla = _gla_mixer(seq(gla_in), seq(g), km, vm, gm, gla_norm_w[0].reshape(1, GLA_DV), ROWS_GLA)
    y = _swa_ffn(sinks[0].astype(F32), swa_in, m_sk, m_sv, x2, o_gla.reshape(b * s, GV), wo,
                 norm_ff_w[0].reshape(1, D_MODEL), w1, w2, final_norm_w.reshape(1, D_MODEL),
                 ROWS_OUT, s // ROWS_OUT)
    return y.reshape(b, s, d)
```

```python
import functools

import jax
import jax.numpy as jnp
from jax import lax
from jax.experimental import pallas as pl
from jax.experimental.pallas import tpu as pltpu

F32 = jnp.float32
BF16 = jnp.bfloat16

D_MODEL = 1024
N_META = 16
GLA_HEADS = 4
GLA_DK = 64
GLA_DV = 128
GLA_RANK = 16
GLA_TAU = 16.0
GLA_CHUNK = 64
GLA_LEVELS = 6
SWA_HEADS = 8
SWA_KV_HEADS = 2
SWA_HD = 64
SWA_BLOCK = 128
SWA_HALF = SWA_BLOCK // 2
ROPE_DIM = 16
ROPE_THETA = 500000.0
D_FF = 4096
EPS = 1e-5
LOG2_E = 1.4426950408889634

LANES = 128
GQ = GLA_HEADS * GLA_DK
GV = GLA_HEADS * GLA_DV
SQ = SWA_HEADS * SWA_HD
SKV = SWA_KV_HEADS * SWA_HD
FF_CHUNK = 4096

ROWS_IN = 1024
ROWS_IN_SUB = 256
W_SQ = 2 * GQ + 2 * GV
W_SK = W_SQ + SQ
W_LR = W_SK + 2 * SKV
ROWS_GLA = 1024
ROWS_OUT = 512
VMEM_LIMIT = 56 * 1024 * 1024


def _nt_dot(a, b):
    return lax.dot_general(a, b, (((1,), (1,)), ((), ())), preferred_element_type=F32)


def _tn_dot(a, b):
    return lax.dot_general(a, b, (((0,), (0,)), ((), ())), preferred_element_type=F32)


def _rms(x, w):
    return x * lax.rsqrt(jnp.mean(x * x, axis=-1, keepdims=True) + EPS) * w


def _project(normed, n_rows, w_ref, bg_ref, cq_ref, cs_ref, spread_ref, unrot_ref, gla_ref, g_ref, swa_ref,
             after=None):
    half = ROPE_DIM // 2
    n_sub = n_rows // ROWS_IN_SUB if n_rows > ROWS_IN_SUB else 1
    sub = n_rows // n_sub
    for i in range(n_sub):
        rows = slice(i * sub, (i + 1) * sub)
        u = normed(rows)

        def proj(lo, width):
            return jnp.dot(u, w_ref[:, lo:lo + width], preferred_element_type=F32)


        cs = cs_ref[rows, :]
        hi = cs.astype(BF16)
        lo = (cs - hi.astype(F32)).astype(BF16)
        pat = jnp.dot(jnp.concatenate([hi, lo], axis=1), spread_ref[...], preferred_element_type=F32) + unrot_ref[...]
        cos, sin = pat[:, 0:LANES], pat[:, LANES:]
        d = lax.broadcasted_iota(jnp.int32, cos.shape, 1) & (SWA_HD - 1)

        def rope(t):
            ts = t * sin
            turned = jnp.where(d < half, -pltpu.roll(ts, LANES - half, axis=1),
                               jnp.where(d < ROPE_DIM, pltpu.roll(ts, half, axis=1), 0.0))
            return t * cos + turned

        sq = proj(W_SQ, SQ)
        for s in range(SQ // LANES):
            sl = slice(s * LANES, (s + 1) * LANES)
            swa_ref[rows, sl] = (rope(sq[:, sl]) * cq_ref[...]).astype(BF16)
        kv = proj(W_SK, 2 * SKV)
        swa_ref[rows, SQ:SQ + SKV] = rope(kv[:, :SKV]).astype(BF16)
        swa_ref[rows, SQ + SKV:SQ + 2 * SKV] = kv[:, SKV:].astype(BF16)

        z = proj(W_LR, GQ) + bg_ref[...]
        g_ref[rows, :] = (jnp.minimum(z, 0.0) - jnp.log1p(jnp.exp(-jnp.abs(z)))) * (LOG2_E / GLA_TAU)

        qk = proj(0, 2 * GQ)
        gla_ref[rows, 0:GQ] = (qk[:, :GQ] * (GLA_DK ** -0.5)).astype(BF16)
        gla_ref[rows, GQ:2 * GQ] = qk[:, GQ:].astype(BF16)
        gla_ref[rows, 2 * GQ:2 * GQ + GV] = proj(2 * GQ, GV).astype(BF16)
        gla_ref[rows, 2 * GQ + GV:2 * GQ + 2 * GV] = proj(2 * GQ + GV, GV).astype(BF16)
        if after is not None:
            after(rows)


PROJ_WIDTHS = (2 * GQ + 2 * GV, GQ, SQ + 2 * SKV)
PROJ_DTYPES = (BF16, F32, BF16)


def _meta_prep_kernel(x_ref, wt_ref, nw_ref, wgu_ref, bg_ref, cq_ref, cs_ref, spread_ref, unrot_ref,
                      w_ref, km_ref, vm_ref, gm_ref, sk_ref, sv_ref, gla_s, g_s, swa_s):
    o_lr = 2 * GQ + 2 * GV
    o_sq = o_lr + GLA_RANK
    o_sk = o_sq + SQ
    half = SWA_HEADS // 2

    def rows(lo, n):
        return wt_ref[lo:lo + n, :].astype(BF16)

    slabs = [rows(c * LANES, LANES) for c in range(o_lr // LANES)]
    slabs += [jnp.concatenate([rows(o_sq + t * SWA_HD, SWA_HD), rows(o_sq + (t + half) * SWA_HD, SWA_HD)], axis=0)
              for t in range(half)]
    slabs += [rows(o_sk + c * LANES, LANES) for c in range(2 * SKV // LANES)]
    for c, slab in enumerate(slabs):
        w_ref[:, c * LANES:(c + 1) * LANES] = slab.T
    w_gate = lax.dot_general(wt_ref[o_lr:o_sq, :], wgu_ref[...], (((0,), (0,)), ((), ())),
                             precision=lax.Precision.HIGHEST, preferred_element_type=F32)
    w_ref[:, W_LR:W_LR + GQ] = w_gate.astype(BF16)
    _project(lambda rows: _rms(x_ref[rows, :], nw_ref[...]).astype(BF16), x_ref.shape[0],
             w_ref, bg_ref, cq_ref, cs_ref, spread_ref, unrot_ref, gla_s, g_s, swa_s)
    pad = GLA_CHUNK - N_META
    for ref, val in ((km_ref, gla_s[:, GQ:2 * GQ]), (vm_ref, gla_s[:, 2 * GQ:2 * GQ + GV]), (gm_ref, g_s[...])):
        ref[0:pad, :] = jnp.zeros((pad, ref.shape[1]), ref.dtype)
        ref[pad:, :] = val
    sk_ref[...] = swa_s[:, SQ:SQ + SKV]
    sv_ref[...] = swa_s[:, SQ + SKV:]


def _inproj_kernel(x0_ref, xa_ref, xm_ref, wt_ref, wgu_ref, nw_ref, bg_ref, cq_ref, cs_ref, csm_ref, spread_ref, unrot_ref,
                   w1_ref, w2_ref, wo_ref, *refs):
    *proj_refs, w1b_ref, w2b_ref, wob_ref, km_ref, vm_ref, gm_ref, sk_ref, sv_ref, u_ref, w_ref, gla_s, g_s, swa_s = refs

    @pl.when(pl.program_id(0) == 0)
    def _():
        _meta_prep_kernel(xm_ref, wt_ref, nw_ref, wgu_ref, bg_ref, cq_ref, csm_ref, spread_ref, unrot_ref,
                          w_ref, km_ref, vm_ref, gm_ref, sk_ref, sv_ref, gla_s, g_s, swa_s)
        u_ref[...] = _rms(x0_ref[...], nw_ref[...]).astype(BF16)

    def norm_ahead(rows):
        u_ref[rows, :] = _rms(xa_ref[rows, :], nw_ref[...]).astype(BF16)

    _project(lambda rows: u_ref[rows, :], xa_ref.shape[0], w_ref, bg_ref, cq_ref,
             cs_ref, spread_ref, unrot_ref, *proj_refs, after=norm_ahead)
    w1b_ref[...] = w1_ref[...].astype(BF16)
    w2b_ref[...] = w2_ref[...].astype(BF16)
    wob_ref[...] = wo_ref[...].astype(BF16)


def _in_projection(x2, meta_tokens, w_in_t, w_gate_up, small, cs, cs_meta, spread, unrot, w1, w2, wo, rows):
    n = x2.shape[0]
    n_meta = meta_tokens.shape[0]
    steps = n // rows
    nw, bg, cq = small
    nblk_pos = cs.shape[0] // rows
    assert w1.shape[0] % steps == 0 and w2.shape[0] % steps == 0 and wo.shape[0] == steps * SWA_HD

    def row_spec(width):
        return pl.BlockSpec((rows, width), lambda i: (i, 0))

    def full(a):
        return pl.BlockSpec(a.shape, lambda i: (0,) * a.ndim)

    def slice_spec(a):
        return pl.BlockSpec((a.shape[0] // steps, a.shape[1]), lambda i: (i, 0))

    n_gla = GV // SWA_HD
    half = SWA_HEADS // 2

    def wo_src(i):
        j = i - n_gla
        return (jnp.where(j < 0, i, n_gla + j // 2 + half * (j % 2)), 0)

    x_first = pl.BlockSpec((rows, D_MODEL), lambda i: (0, 0), pipeline_mode=pl.Buffered(1))
    x_ahead = pl.BlockSpec((rows, D_MODEL), lambda i: (jnp.minimum(i + 1, steps - 1), 0))
    def once(a):
        return pl.BlockSpec(a.shape, lambda i: (0,) * a.ndim, pipeline_mode=pl.Buffered(1))

    in_specs = ([x_first, x_ahead, once(meta_tokens), once(w_in_t), once(w_gate_up), full(nw), full(bg), full(cq),
                 pl.BlockSpec((rows, cs.shape[1]), lambda i: (i % nblk_pos, 0)), once(cs_meta), full(spread), full(unrot),
                 slice_spec(w1), slice_spec(w2), pl.BlockSpec((SWA_HD, wo.shape[1]), wo_src)])
    weights_out = [jax.ShapeDtypeStruct(a.shape, BF16) for a in (w1, w2, wo)]
    meta_out = [jax.ShapeDtypeStruct((GLA_CHUNK, GQ), BF16), jax.ShapeDtypeStruct((GLA_CHUNK, GV), BF16),
                jax.ShapeDtypeStruct((GLA_CHUNK, GQ), F32),
                jax.ShapeDtypeStruct((n_meta, SKV), BF16), jax.ShapeDtypeStruct((n_meta, SKV), BF16)]
    return pl.pallas_call(
        _inproj_kernel,
        grid=(steps,),
        in_specs=in_specs,
        out_specs=([row_spec(w) for w in PROJ_WIDTHS] + [slice_spec(w1), slice_spec(w2), slice_spec(wo)]
                   + [full(o) for o in meta_out]),
        out_shape=[jax.ShapeDtypeStruct((n, w), d) for w, d in zip(PROJ_WIDTHS, PROJ_DTYPES)] + weights_out + meta_out,
        scratch_shapes=([pltpu.VMEM((rows, D_MODEL), BF16),
                         pltpu.VMEM((D_MODEL, W_LR + GQ), BF16)]
                        + [pltpu.VMEM((n_meta, w), d) for w, d in zip(PROJ_WIDTHS, PROJ_DTYPES)]),
        compiler_params=pltpu.CompilerParams(dimension_semantics=("arbitrary",),
                                             vmem_limit_bytes=VMEM_LIMIT),
        name="in_projection",
    )(x2, x2, meta_tokens, w_in_t, w_gate_up, nw, bg, cq, cs, cs_meta, spread, unrot, w1, w2, wo)


def _block_cumsums(g):
    rows = g.shape[0]
    row = lax.broadcasted_iota(jnp.int32, g.shape, 0)
    w, t = [g], [g]
    for k in range(GLA_LEVELS):
        s = 1 << k
        upper = (row & s) != 0
        below = pltpu.roll(t[k], s, axis=0)
        above = pltpu.roll(t[k], rows - s, axis=0)
        w.append(w[k] + jnp.where(upper, below, 0.0))
        t.append(t[k] + jnp.where(upper, below, above))
    return w, t


def _gla_levels():
    ri = lax.broadcasted_iota(jnp.int32, (GLA_CHUNK, LANES), 0)
    ci = lax.broadcasted_iota(jnp.int32, (GLA_CHUNK, LANES), 1) & (GLA_CHUNK - 1)
    x = ri ^ ci
    level = jnp.zeros_like(x)
    for b in range(GLA_LEVELS):
        level = level + (x >= (1 << b)).astype(jnp.int32)
    return jnp.where(ci > ri, -1, level)


def _gla_chunk(q, k, v, g, st, level):
    w, t = _block_cumsums(g)
    n_slab = GQ // LANES

    def block_diag_t(m):
        r = m.shape[0]
        m2 = jnp.concatenate([m, m], axis=0).T
        same_head = ((lax.broadcasted_iota(jnp.int32, m2.shape, 0) < GLA_DK)
                     == (lax.broadcasted_iota(jnp.int32, m2.shape, 1) < r))
        return jnp.where(same_head, m2, jnp.zeros_like(m2))

    o = None
    if level is not None:
        a = [jnp.zeros((GLA_CHUNK, LANES), F32) for _ in range(n_slab)]
        for lv in range(GLA_LEVELS + 1):
            if lv == 0:
                qt, kt = q.astype(BF16), k.astype(BF16)
            else:
                qt = (q * jnp.exp2(w[lv - 1])).astype(BF16)
                kt = (k * jnp.exp2(t[lv - 1] - w[lv - 1])).astype(BF16)
            for s in range(n_slab):
                sl = slice(s * LANES, (s + 1) * LANES)
                scores = jnp.dot(qt[:, sl], block_diag_t(kt[:, sl]), preferred_element_type=F32)
                a[s] = jnp.where(level == lv, scores, a[s])

        qb = (q * jnp.exp2(w[GLA_LEVELS])).astype(BF16)
        st_b = st.astype(BF16)
        outs = []
        for s in range(n_slab):
            sl = slice(s * LANES, (s + 1) * LANES)
            v_ab = v[:, (2 * s) * GLA_DV:(2 * s + 2) * GLA_DV]
            first = lax.broadcasted_iota(jnp.int32, v_ab.shape, 1) < GLA_DV
            zero = jnp.zeros_like(v_ab)
            v_bd = jnp.concatenate([jnp.where(first, v_ab, zero), jnp.where(first, zero, v_ab)], axis=0)
            outs.append(jnp.dot(jnp.concatenate([a[s].astype(BF16), qb[:, sl]], axis=1),
                                jnp.concatenate([v_bd, block_diag_t(st_b[:, sl])], axis=0),
                                preferred_element_type=F32))
        o = jnp.concatenate(outs, axis=1)

    tot = t[GLA_LEVELS]
    ku = (k * jnp.exp2(tot - w[GLA_LEVELS])).astype(BF16)
    decayed = st * jnp.exp2(tot[0:1, :])
    lo = lax.broadcasted_iota(jnp.int32, (GLA_DV, LANES), 1) < GLA_DK
    new_st = []
    for s in range(n_slab):
        sl = slice(s * LANES, (s + 1) * LANES)
        upd = _tn_dot(v[:, (2 * s) * GLA_DV:(2 * s + 2) * GLA_DV], ku[:, sl])
        new_st.append(decayed[:, sl] + jnp.where(lo, upd[:GLA_DV], upd[GLA_DV:]))
    return o, jnp.concatenate(new_st, axis=1)


def _gla_gate(o, r, nw):
    out = []
    for h in range(GLA_HEADS):
        sl = slice(h * GLA_DV, (h + 1) * GLA_DV)
        rh = r[:, sl].astype(F32)
        out.append((_rms(o[:, sl], nw) * (rh * jax.nn.sigmoid(rh))).astype(BF16))
    return jnp.concatenate(out, axis=1)


def _gla_kernel(in_ref, g_ref, km_ref, vm_ref, gm_ref, nw_ref, o_ref, st_ref):
    @pl.when(pl.program_id(1) == 0)
    def _():
        st_ref[...] = jnp.zeros(st_ref.shape, F32)
        _, st0 = _gla_chunk(None, km_ref[...].astype(F32), vm_ref[...], gm_ref[...], st_ref[...], None)
        st_ref[...] = st0

    nw = nw_ref[...]
    level = _gla_levels()
    st = st_ref[...]
    for c in range(in_ref.shape[1] // GLA_CHUNK):
        rows = slice(c * GLA_CHUNK, (c + 1) * GLA_CHUNK)
        o, st = _gla_chunk(in_ref[0, rows, 0:GQ].astype(F32), in_ref[0, rows, GQ:2 * GQ].astype(F32),
                           in_ref[0, rows, 2 * GQ:2 * GQ + GV], g_ref[0, rows, :], st, level)
        o_ref[0, rows, :] = _gla_gate(o, in_ref[0, rows, 2 * GQ + GV:2 * GQ + 2 * GV], nw)
    st_ref[...] = st


def _gla_mixer(gla_in, g, km, vm, gm, gla_norm_w, rows):
    b, s, _ = gla_in.shape

    def seq_spec(width):
        return pl.BlockSpec((1, rows, width), lambda i, j: (i, j, 0))

    def full(a):
        return pl.BlockSpec(a.shape, lambda i, j: (0,) * a.ndim)

    return pl.pallas_call(
        _gla_kernel,
        grid=(b, s // rows),
        in_specs=[seq_spec(gla_in.shape[-1]), seq_spec(GQ), full(km), full(vm), full(gm), full(gla_norm_w)],
        out_specs=seq_spec(GV),
        out_shape=jax.ShapeDtypeStruct((b, s, GV), BF16),
        scratch_shapes=[pltpu.VMEM((GLA_DV, GQ), F32)],
        compiler_params=pltpu.CompilerParams(dimension_semantics=("arbitrary", "arbitrary"),
                                             vmem_limit_bytes=VMEM_LIMIT),
        name="gla_mixer",
    )(gla_in, g, km, vm, gm, gla_norm_w)


def _swa_block(sink_ref, q, kc, kp, vc, vp, km, vm, has_prev):
    n_win = SWA_BLOCK + SWA_HALF
    n_keys = n_win + N_META
    r = lax.broadcasted_iota(jnp.int32, (SWA_HALF, n_keys), 0)
    c = lax.broadcasted_iota(jnp.int32, (SWA_HALF, n_keys), 1)
    in_band = (c > r) & (c <= r + SWA_BLOCK)
    lane = lax.broadcasted_iota(jnp.int32, (SWA_HALF, LANES), 1)
    low = lane < SWA_HD

    halves = []
    for half in range(2):
        rows = slice(half * SWA_HALF, (half + 1) * SWA_HALF)
        if half == 0:
            k_all = jnp.concatenate([kp, kc[:SWA_HALF], km], axis=0)
            v_all = jnp.concatenate([vp, vc[:SWA_HALF], vm], axis=0)
            n_prev = SWA_BLOCK
        else:
            k_all = jnp.concatenate([kp[SWA_HALF:], kc, km], axis=0)
            v_all = jnp.concatenate([vp[SWA_HALF:], vc, vm], axis=0)
            n_prev = SWA_HALF
        valid = (c >= n_win) | in_band
        if has_prev is not None:
            valid = (c >= n_win) | (in_band & (has_prev | (c >= n_prev)))
        bias = jnp.where(valid, 0.0, -jnp.inf)
        pieces = []
        for t in range(SWA_HEADS // 2):
            qt = q[rows, t * LANES:(t + 1) * LANES]
            pieces += [jnp.where(low, qt, jnp.zeros_like(qt)), jnp.where(low, jnp.zeros_like(qt), qt)]
        q_all = jnp.concatenate(pieces, axis=0)
        s_all = _nt_dot(q_all, k_all)
        p, denom = [], []
        for i in range(SWA_HEADS):
            s = s_all[i * SWA_HALF:(i + 1) * SWA_HALF] + bias
            sink = sink_ref[(i // 2) + (i % 2) * (SWA_HEADS // 2)] * LOG2_E
            m = jnp.maximum(jnp.max(s, axis=-1, keepdims=True), sink)
            p.append(jnp.exp2(s - m))
            denom.append(jnp.sum(p[i], axis=-1, keepdims=True) + jnp.exp2(sink - m))
        o = jnp.dot(jnp.concatenate(p, axis=0).astype(BF16), v_all, preferred_element_type=F32)
        slabs = []
        for t in range(SWA_HEADS // 2):
            lo = o[(2 * t) * SWA_HALF:(2 * t + 1) * SWA_HALF] / denom[2 * t]
            hi = o[(2 * t + 1) * SWA_HALF:(2 * t + 2) * SWA_HALF] / denom[2 * t + 1]
            slabs.append(jnp.where(low, lo, hi).astype(BF16))
        halves.append(jnp.concatenate(slabs, axis=1))
    return jnp.concatenate(halves, axis=0)


def _swa_ffn_kernel(tiles_per_seq, sink_ref, swa0_ref, swa_ref, prev_ref, km_ref, vm_ref,
                    x_ref, og_ref, wo_ref, nf_ref, w1_ref, w2_ref, nl_ref, y_ref, os_ref):
    s = pl.program_id(0)
    ahead = jnp.minimum(s + 1, pl.num_programs(0) - 1)
    seq_start = lax.rem(ahead, tiles_per_seq) == 0
    cur = lax.rem(s, 2)
    k_lanes, v_lanes = slice(SQ, SQ + SKV), slice(SQ + SKV, SQ + 2 * SKV)

    def swa_tile_block(j, slot, tile_ref, first_prev):
        rows = slice(j * SWA_BLOCK, (j + 1) * SWA_BLOCK)
        if j == 0:
            kp, vp, has_prev = first_prev
        else:
            before = slice((j - 1) * SWA_BLOCK, j * SWA_BLOCK)
            kp, vp, has_prev = tile_ref[before, k_lanes], tile_ref[before, v_lanes], None
        os_ref[slot, rows, :] = _swa_block(sink_ref, tile_ref[rows, 0:SQ], tile_ref[rows, k_lanes], kp,
                                           tile_ref[rows, v_lanes], vp, km_ref[...], vm_ref[...], has_prev)

    n_blocks = x_ref.shape[0] // SWA_BLOCK

    @pl.when(s == 0)
    def _():
        stand_in = (swa0_ref[0:SWA_BLOCK, k_lanes], swa0_ref[0:SWA_BLOCK, v_lanes], s > 0)
        for j in range(n_blocks):
            swa_tile_block(j, 0, swa0_ref, stand_in)

    n_ff = D_FF // FF_CHUNK
    swa_per_ff = n_blocks // n_ff

    h = (x_ref[...]
         + jnp.dot(og_ref[...], wo_ref[0:GV, :], preferred_element_type=F32)
         + jnp.dot(os_ref[cur], wo_ref[GV:GV + SQ, :], preferred_element_type=F32))
    f = _rms(h, nf_ref[...]).astype(BF16)
    ff = None
    for c in range(n_ff):
        sl = slice(c * FF_CHUNK, (c + 1) * FF_CHUNK)
        a = jnp.maximum(jnp.dot(f, w1_ref[:, sl], preferred_element_type=F32), 0.0)
        for j in range(c * swa_per_ff, (c + 1) * swa_per_ff):
            swa_tile_block(j, 1 - cur, swa_ref,
                           (prev_ref[:, 0:SKV], prev_ref[:, SKV:], jnp.logical_not(seq_start)))
        d = jnp.dot((a * a).astype(BF16), w2_ref[sl, :], preferred_element_type=F32)
        ff = d if ff is None else ff + d
    y_ref[...] = _rms(h + ff, nl_ref[...])


def _swa_ffn(sinks, swa_in, km, vm, x2, og, wo, nf, w1, w2, nl, rows, tiles_per_seq):
    n = x2.shape[0]
    n_tiles = n // rows
    prev_per_tile = rows // SWA_BLOCK
    width = swa_in.shape[1]

    first_spec = pl.BlockSpec((rows, width), lambda i: (0, 0), pipeline_mode=pl.Buffered(1))
    ahead_spec = pl.BlockSpec((rows, width), lambda i: (jnp.minimum(i + 1, n_tiles - 1), 0))
    prev_spec = pl.BlockSpec((SWA_BLOCK, 2 * SKV),
                             lambda i: (jnp.minimum(i + 1, n_tiles - 1) * prev_per_tile - 1, SQ // (2 * SKV)))

    def ffn_spec(width):
        return pl.BlockSpec((rows, width), lambda i: (i, 0))

    def full(a):
        return pl.BlockSpec(a.shape, lambda i: (0,) * a.ndim, pipeline_mode=pl.Buffered(1))

    weights = (wo, nf, w1, w2, nl)
    return pl.pallas_call(
        functools.partial(_swa_ffn_kernel, tiles_per_seq),
        grid=(n_tiles,),
        in_specs=([pl.BlockSpec(memory_space=pltpu.SMEM), first_spec, ahead_spec, prev_spec,
                   full(km), full(vm), ffn_spec(D_MODEL), ffn_spec(GV)] + [full(a) for a in weights]),
        out_specs=ffn_spec(D_MODEL),
        out_shape=jax.ShapeDtypeStruct((n, D_MODEL), F32),
        scratch_shapes=[pltpu.VMEM((2, rows, SQ), BF16)],
        compiler_params=pltpu.CompilerParams(dimension_semantics=("arbitrary",),
                                             vmem_limit_bytes=VMEM_LIMIT),
        name="swa_ffn",
    )(sinks, swa_in, swa_in, swa_in, km, vm, x2, og, *weights)


def _rope_cos_sin(first_pos, n_pos):
    inv_freq = 1.0 / (ROPE_THETA ** (jnp.arange(0, ROPE_DIM, 2, dtype=F32) / ROPE_DIM))
    ang = jnp.arange(first_pos, first_pos + n_pos, dtype=jnp.int32).astype(F32)[:, None] * inv_freq[None, :]
    return jnp.concatenate([jnp.cos(ang), jnp.sin(ang)], axis=1)


def _rope_spread():
    half = ROPE_DIM // 2
    lane = jnp.arange(2 * LANES)
    d = lane % SWA_HD
    which = lane // LANES
    src = which * half + d % half
    spread = jnp.where((d < ROPE_DIM)[None, :] & (jnp.arange(2 * half)[:, None] == src[None, :]), 1.0, 0.0)
    unrotated = jnp.where((which == 0) & (d >= ROPE_DIM), 1.0, 0.0)
    return jnp.concatenate([spread, spread], axis=0).astype(BF16), unrotated[None, :].astype(F32)


def _swa_head_order():
    order = []
    for t in range(SWA_HEADS // 2):
        order += [t, t + SWA_HEADS // 2]
    return order


def kernel(x, meta_tokens, norm_mix_w, w_in, w_gate_up, b_gate, gla_norm_w, sinks, w_out, norm_ff_w,
           w_ff1, w_ff2, final_norm_w):
    b, s, d = x.shape
    assert d == D_MODEL and s % ROWS_IN == 0 and s % ROWS_OUT == 0 and w_in.shape[0] == 1
    assert s % ROWS_GLA == 0 and ROWS_GLA % GLA_CHUNK == 0
    assert ROWS_OUT % (SWA_BLOCK * (D_FF // FF_CHUNK)) == 0

    bg = b_gate[0].reshape(1, GQ)
    nw = norm_mix_w[0].reshape(1, D_MODEL)
    cq = jnp.full((1, LANES), SWA_HD ** -0.5 * LOG2_E, F32)

    spread, unrot = _rope_spread()
    cs_meta = _rope_cos_sin(0, N_META)
    cs_tok = _rope_cos_sin(N_META, s)

    x2 = x.reshape(b * s, d)
    gla_in, g, swa_in, w1, w2, wo, km, vm, gm, m_sk, m_sv = _in_projection(
        x2, meta_tokens.astype(F32), w_in[0].T, w_gate_up[0], (nw, bg, cq), cs_tok, cs_meta, spread, unrot,
        w_ff1[0], w_ff2[0], w_out[0], ROWS_IN)

    def seq(a):
        return a.reshape(b, s, a.shape[-1])

    o_gla = _gla_mixer(seq(gla_in), seq(g), km, vm, gm, gla_norm_w[0].reshape(1, GLA_DV), ROWS_GLA)
    y = _swa_ffn(sinks[0].astype(F32), swa_in, m_sk, m_sv, x2, o_gla.reshape(b * s, GV), wo,
                 norm_ff_w[0].reshape(1, D_MODEL), w1, w2, final_norm_w.reshape(1, D_MODEL),
                 ROWS_OUT, s // ROWS_OUT)
    return y.reshape(b, s, d)
```
